```python
import math
import jax, jax.numpy as jnp
from jax import lax
import numpy as np

D_MODEL = 1024
BATCH = 4
SEQ = 4096
DEPTH = 2

D_MIX = 1024
W_GRP = D_MIX // 4
HEAD_DIM = 64
N_HEADS = W_GRP // HEAD_DIM
NSA_DK = 64
CMP_STRIDE = 16
CMP_BLOCK = 2 * CMP_STRIDE
CMP_HIDDEN = 128
SEL_BLOCK = 64
N_SEL = 16
WINDOW = 512
Q_BLOCK = 128
REL_BUCKETS = 32
REL_MAX_EXACT = REL_BUCKETS // 2
REL_MAX_DIST = 128
SGU_CHUNK = 128
CONV_WIDTH = 31
RWKV_LORA = 32
RWKV_GN_EPS = 64e-5
RWKV_SHIFT_W = 3 * W_GRP + 2 * RWKV_LORA
PLE_DIM = 256
NORM_EPS = 1e-6
NEG = -1e30
FORCE = 1e4

PROJ_WIDTHS = [
    W_GRP, NSA_DK, NSA_DK, NSA_DK, NSA_DK, NSA_DK, NSA_DK, 3 * N_HEADS, W_GRP,
    W_GRP, W_GRP, W_GRP,
    W_GRP, W_GRP, W_GRP,
    RWKV_SHIFT_W, W_GRP,
]
N_IN = sum(PROJ_WIDTHS)

kernel_name = "hybrid_nsa_gmlp_conformer_rwkv7_block"


def rms_norm(x, g):
    xf = x.astype(jnp.float32)
    return xf * lax.rsqrt(jnp.mean(xf * xf, -1, keepdims=True) + NORM_EPS) * g


def layer_norm(x, g, b, eps=1e-5):
    xf = x.astype(jnp.float32)
    mu = jnp.mean(xf, -1, keepdims=True)
    var = jnp.mean(jnp.square(xf - mu), -1, keepdims=True)
    return (xf - mu) * lax.rsqrt(var + eps) * g + b


def rel_bucket(dist):
    n = jnp.maximum(dist, 0)
    nf = jnp.maximum(n, REL_MAX_EXACT).astype(jnp.float32)
    large = REL_MAX_EXACT + (jnp.log(nf / REL_MAX_EXACT) / math.log(REL_MAX_DIST / REL_MAX_EXACT)
                             * (REL_BUCKETS - REL_MAX_EXACT)).astype(jnp.int32)
    large = jnp.minimum(large, REL_BUCKETS - 1)
    return jnp.where(n < REL_MAX_EXACT, n, large)


def masked_softmax(logits, mask):
    p = jax.nn.softmax(jnp.where(mask, logits.astype(jnp.float32), NEG), axis=-1)
    return jnp.where(mask, p, 0.0)


def nsa_compress(k, pos, w1, w2):
    B, S, dk = k.shape
    ch = k.reshape(B, S // CMP_STRIDE, CMP_STRIDE, dk)
    blk = jnp.concatenate([ch[:, :-1], ch[:, 1:]], axis=2) + pos
    flat = blk.reshape(B, -1, CMP_BLOCK * dk)
    return jax.nn.silu(flat @ w1) @ w2


def nsa_attention(q, kc, vc, k_sel, v_sel, k_win, v_win, gates, rel_bias):
    B, S, H, dk = q.shape
    scale = dk ** -0.5
    n_cmp = kc.shape[1]
    n_blk = S // SEL_BLOCK
    n_top = min(N_SEL, n_blk)
    ratio = SEL_BLOCK // CMP_STRIDE
    cmp_end = jnp.arange(n_cmp) * CMP_STRIDE + CMP_BLOCK - 1
    ks_blk = k_sel.reshape(B, n_blk, SEL_BLOCK, dk)
    vs_blk = v_sel.reshape(B, n_blk, SEL_BLOCK, dk)
    kw_pad = jnp.pad(k_win, ((0, 0), (WINDOW, 0), (0, 0)))
    vw_pad = jnp.pad(v_win, ((0, 0), (WINDOW, 0), (0, 0)))
    gather = jax.vmap(lambda blocks, idx: blocks[idx])
    j = jnp.arange(n_blk)

    def one_block(c):
        t0 = c * Q_BLOCK
        t = t0 + jnp.arange(Q_BLOCK)
        qc = lax.dynamic_slice_in_dim(q, t0, Q_BLOCK, axis=1).astype(jnp.float32)
        gc = jax.nn.sigmoid(lax.dynamic_slice_in_dim(gates, t0, Q_BLOCK, axis=1).astype(jnp.float32))
        d_c = t[:, None] - cmp_end[None, :]
        b_c = jnp.moveaxis(rel_bias[rel_bucket(d_c)], -1, 0)
        s_c = jnp.einsum('bqhd,bnd->bhqn', qc, kc) * scale + b_c
        p_c = masked_softmax(s_c, d_c >= 0)
        o_c = jnp.einsum('bhqn,bnd->bqhd', p_c, vc)
        pc = jnp.pad(p_c.sum(1), ((0, 0), (0, 0), (1, ratio)))
        imp = (pc[..., :n_blk * ratio].reshape(B, Q_BLOCK, n_blk, ratio).sum(-1)
               + pc[..., ratio::ratio][..., :n_blk])
        cur = t // SEL_BLOCK
        causal_blk = j[None, :] <= cur[:, None]
        forced = (j[None, :] == 0) | (j[None, :] == cur[:, None]) | (j[None, :] == cur[:, None] - 1)
        imp = jnp.where(causal_blk, jnp.where(forced, FORCE, imp), NEG)
        top_val, top_idx = lax.top_k(imp, n_top)
        ksel = gather(ks_blk, top_idx)
        vsel = gather(vs_blk, top_idx)
        kpos = top_idx[..., None] * SEL_BLOCK + jnp.arange(SEL_BLOCK)
        d_s = t[None, :, None, None] - kpos
        m_s = (top_val[..., None] > 0.5 * NEG) & (d_s >= 0)
        b_s = jnp.moveaxis(rel_bias[rel_bucket(d_s)], -1, 1)
        s_s = jnp.einsum('bqhd,bqnkd->bhqnk', qc, ksel) * scale + b_s
        p_s = masked_softmax(s_s.reshape(B, H, Q_BLOCK, -1), m_s.reshape(B, 1, Q_BLOCK, -1))
        o_s = jnp.einsum('bhqm,bqmd->bqhd', p_s, vsel.reshape(B, Q_BLOCK, -1, dk))
        kw = lax.dynamic_slice_in_dim(kw_pad, t0, Q_BLOCK + WINDOW, axis=1)
        vw = lax.dynamic_slice_in_dim(vw_pad, t0, Q_BLOCK + WINDOW, axis=1)
        kpos_w = t0 - WINDOW + jnp.arange(Q_BLOCK + WINDOW)
        d_w = t[:, None] - kpos_w[None, :]
        m_w = (d_w >= 0) & (d_w < WINDOW) & (kpos_w[None, :] >= 0)
        b_w = jnp.moveaxis(rel_bias[rel_bucket(d_w)], -1, 0)
        s_w = jnp.einsum('bqhd,bkd->bhqk', qc, kw) * scale + b_w
        p_w = masked_softmax(s_w, m_w)
        o_w = jnp.einsum('bhqk,bkd->bqhd', p_w, vw)
        o = gc[..., 0:1] * o_c + gc[..., 1:2] * o_s + gc[..., 2:3] * o_w
        return o.reshape(B, Q_BLOCK, H * dk)

    out = lax.map(one_block, jnp.arange(S // Q_BLOCK))
    return jnp.moveaxis(out, 0, 1).reshape(B, S, H * dk)


def spatial_gating(u, v, ln_g, ln_b, w_s, b_s):
    B, S, C = u.shape
    u = jax.nn.gelu(u)
    v = layer_norm(jax.nn.gelu(v), ln_g, ln_b)
    vc = v.reshape(B, S // SGU_CHUNK, SGU_CHUNK, N_HEADS, C // N_HEADS)
    w = w_s * jnp.tril(jnp.ones((SGU_CHUNK, SGU_CHUNK), w_s.dtype))
    sv = jnp.einsum('hts,bcshd->bcthd', w, vc) + b_s.T[None, None, :, :, None]
    return u * sv.reshape(B, S, C)


def conv_module(a, b, w_dw, b_dw, ln_g, ln_b, w_pw, b_pw):
    x = a * jax.nn.sigmoid(b)
    C = x.shape[-1]
    y = lax.conv_general_dilated(x, w_dw[:, None, :].astype(x.dtype), window_strides=(1,),
                                 padding=((CONV_WIDTH - 1, 0),),
                                 dimension_numbers=('NWC', 'WIO', 'NWC'),
                                 feature_group_count=C) + b_dw
    y = jax.nn.silu(layer_norm(y, ln_g, ln_b))
    return y @ w_pw + b_pw


def token_shift(z, mu):
    prev = jnp.pad(z, ((0, 0), (1, 0), (0, 0)))[:, :-1]
    return z + mu * (prev - z)


def rwkv7_time_mix(xs, w0, w_up, a0, a_up, k_k, k_a, r_k, gn_g, gn_b):
    B, S, _ = xs.shape
    C, H, N, L = W_GRP, N_HEADS, HEAD_DIM, RWKV_LORA
    r, k, v = xs[..., :C], xs[..., C:2 * C], xs[..., 2 * C:3 * C]
    wd, ad = xs[..., 3 * C:3 * C + L], xs[..., 3 * C + L:]
    w_log = -jax.nn.softplus(-(w0 + jnp.tanh(wd) @ w_up)) - 0.5
    decay = jnp.exp(-jnp.exp(w_log.astype(jnp.float32)))
    a = jax.nn.sigmoid(a0 + ad @ a_up)
    kk = (k * k_k).astype(jnp.float32).reshape(B, S, H, N)
    kk = kk / jnp.maximum(jnp.sqrt(jnp.sum(kk * kk, -1, keepdims=True)), 1e-12)
    k = k * (1 + (a - 1) * k_a)

    def heads(z):
        return jnp.moveaxis(z.astype(jnp.float32).reshape(B, S, H, N), 1, 0)

    def step(state, inp):
        r_t, w_t, k_t, v_t, kk_t, a_t = inp
        sa = jnp.einsum('bhvk,bhk->bhv', state, -kk_t)
        state = (state * w_t[:, :, None, :]
                 + sa[..., None] * (kk_t * a_t)[:, :, None, :]
                 + v_t[..., None] * k_t[:, :, None, :])
        return state, jnp.einsum('bhvk,bhk->bhv', state, r_t)

    state0 = jnp.zeros((B, H, N, N), jnp.float32)
    _, y = lax.scan(step, state0, (heads(r), heads(decay), heads(k), heads(v), heads(kk), heads(a)))
    y = jnp.moveaxis(y, 0, 1)
    mu = jnp.mean(y, -1, keepdims=True)
    var = jnp.mean(jnp.square(y - mu), -1, keepdims=True)
    yn = ((y - mu) * lax.rsqrt(var + RWKV_GN_EPS)).reshape(B, S, C) * gn_g + gn_b
    rh = r.astype(jnp.float32).reshape(B, S, H, N)
    kh = k.astype(jnp.float32).reshape(B, S, H, N)
    vh = v.astype(jnp.float32).reshape(B, S, H, N)
    bonus = (jnp.sum(rh * kh * r_k, -1, keepdims=True) * vh).reshape(B, S, C)
    return yn + bonus


def setup_inputs(seed: int = 0) -> dict:
    key = jax.random.key(seed)
    ks = iter(jax.random.split(key, 40))
    nrm = lambda shape, s=1.0: s * jax.random.normal(next(ks), shape, jnp.float32)
    D, L, C, H = DEPTH, RWKV_LORA, W_GRP, N_HEADS
    return {
        "x": nrm((BATCH, SEQ, D_MODEL)),
        "p": nrm((DEPTH, BATCH, SEQ, PLE_DIM)),
        "rel_bias": nrm((REL_BUCKETS, N_HEADS), 0.5),
        "w_in": nrm((D, D_MODEL, N_IN), D_MODEL ** -0.5),
        "w_out": nrm((D, D_MIX, D_MODEL), D_MIX ** -0.5),
        "g_pre": 1.0 + nrm((D, D_MODEL), 0.05),
        "g_post": 1.0 + nrm((D, D_MODEL), 0.05),
        "nsa_pos": nrm((D, 2, CMP_BLOCK, NSA_DK), 0.1),
        "nsa_w1": nrm((D, 2, CMP_BLOCK * NSA_DK, CMP_HIDDEN), (CMP_BLOCK * NSA_DK) ** -0.5),
        "nsa_w2": nrm((D, 2, CMP_HIDDEN, NSA_DK), CMP_HIDDEN ** -0.5),
        "sgu_ln_g": 1.0 + nrm((D, C), 0.05),
        "sgu_ln_b": nrm((D, C), 0.02),
        "sgu_w": nrm((D, H, SGU_CHUNK, SGU_CHUNK), SGU_CHUNK ** -0.5),
        "sgu_b": 1.0 + nrm((D, H, SGU_CHUNK), 0.05),
        "conv_w": nrm((D, CONV_WIDTH, C), CONV_WIDTH ** -0.5),
        "conv_b": nrm((D, C), 0.02),
        "conv_ln_g": 1.0 + nrm((D, C), 0.05),
        "conv_ln_b": nrm((D, C), 0.02),
        "conv_pw": nrm((D, C, C), C ** -0.5),
        "conv_pw_b": nrm((D, C), 0.02),
        "rwkv_mu": jax.random.uniform(next(ks), (D, RWKV_SHIFT_W), jnp.float32),
        "rwkv_w0": -2.0 + nrm((D, C), 1.0),
        "rwkv_w_up": nrm((D, L, C), 0.5 * L ** -0.5),
        "rwkv_a0": nrm((D, C), 0.1),
        "rwkv_a_up": nrm((D, L, C), 0.5 * L ** -0.5),
        "rwkv_k_k": 0.85 + nrm((D, C), 0.05),
        "rwkv_k_a": 1.0 + nrm((D, C), 0.05),
        "rwkv_r_k": nrm((D, H, HEAD_DIM), 0.1),
        "rwkv_gn_g": 1.0 + nrm((D, C), 0.05),
        "rwkv_gn_b": nrm((D, C), 0.02),
        "ple_proj": nrm((D, PLE_DIM, D_MODEL), PLE_DIM ** -0.5),
        "ple_gate": nrm((D, D_MODEL, D_MODEL), D_MODEL ** -0.5),
    }


def reference(x, p, rel_bias, w_in, w_out, g_pre, g_post, nsa_pos, nsa_w1, nsa_w2,
              sgu_ln_g, sgu_ln_b, sgu_w, sgu_b, conv_w, conv_b, conv_ln_g, conv_ln_b,
              conv_pw, conv_pw_b, rwkv_mu, rwkv_w0, rwkv_w_up, rwkv_a0, rwkv_a_up,
              rwkv_k_k, rwkv_k_a, rwkv_r_k, rwkv_gn_g, rwkv_gn_b, ple_proj, ple_gate):
    B, S, _ = x.shape
    split_at = np.cumsum(PROJ_WIDTHS)[:-1].tolist()
    for i in range(DEPTH):
        h = rms_norm(x, g_pre[i]).astype(x.dtype)
        proj = h @ w_in[i]
        (q, k_cmp, v_cmp, k_sel, v_sel, k_win, v_win, nsa_g, z_a,
         u, v, z_b, glu_a, glu_b, z_c, rw, z_d) = jnp.split(proj, split_at, axis=-1)
        kc = nsa_compress(k_cmp, nsa_pos[i, 0], nsa_w1[i, 0], nsa_w2[i, 0])
        vc = nsa_compress(v_cmp, nsa_pos[i, 1], nsa_w1[i, 1], nsa_w2[i, 1])
        y_a = nsa_attention(q.reshape(B, S, N_HEADS, HEAD_DIM), kc, vc, k_sel, v_sel, k_win, v_win,
                            nsa_g.reshape(B, S, N_HEADS, 3), rel_bias)
        y_b = spatial_gating(u, v, sgu_ln_g[i], sgu_ln_b[i], sgu_w[i], sgu_b[i])
        y_c = conv_module(glu_a, glu_b, conv_w[i], conv_b[i], conv_ln_g[i], conv_ln_b[i],
                          conv_pw[i], conv_pw_b[i])
        y_d = rwkv7_time_mix(token_shift(rw, rwkv_mu[i]), rwkv_w0[i], rwkv_w_up[i], rwkv_a0[i],
                             rwkv_a_up[i], rwkv_k_k[i], rwkv_k_a[i], rwkv_r_k[i],
                             rwkv_gn_g[i], rwkv_gn_b[i])
        mix = jnp.concatenate([y_a * jax.nn.silu(z_a), y_b * jax.nn.silu(z_b),
                               y_c * jax.nn.silu(z_c), y_d * jax.nn.silu(z_d)], axis=-1).astype(x.dtype)
        x = x + rms_norm(mix @ w_out[i], g_post[i]).astype(x.dtype)
        x = x + (jax.nn.sigmoid(x @ ple_gate[i]) * (p[i] @ ple_proj[i])).astype(x.dtype)
    return x
```

```python
import functools
import math

import numpy as np
import jax
import jax.numpy as jnp
from jax import lax
from jax.experimental import pallas as pl
from jax.experimental.pallas import tpu as pltpu

F32 = jnp.float32
BF16 = jnp.bfloat16

W_GRP = 256
HEAD_DIM = 64
N_HEADS = 4
NSA_DK = 64
CMP_STRIDE = 16
CMP_BLOCK = 32
CMP_HIDDEN = 128
SEL_BLOCK = 64
N_SEL = 16
WINDOW = 512
REL_BUCKETS = 32
REL_MAX_EXACT = 16
REL_MAX_DIST = 128
CONV_WIDTH = 31
RWKV_LORA = 32
RWKV_GN_EPS = 64e-5
NORM_EPS = 1e-6
LN_EPS = 1e-5
NEG = -1e30
FORCE = 1e4

LANES = 128
Q_TILE = 128
RW_CHUNK = 64
VMEM_LIMIT = 48 * 1024 * 1024

NSA_W, GM_W, CV_W, RW_W = 1024, 768, 768, 1152
N_PROJ = NSA_W + GM_W + CV_W + RW_W


def _proj_perm():
    o = {}
    names = ["q", "kc", "vc", "ks", "vs", "kw", "vw", "g", "za", "u", "v", "zb", "ga", "gb", "zc", "rw", "zd"]
    widths = [256, 64, 64, 64, 64, 64, 64, 12, 256, 256, 256, 256, 256, 256, 256, 832, 256]
    off = 0
    for n, w in zip(names, widths):
        o[n] = np.arange(off, off + w)
        off += w
    pad = lambda n: -np.ones((n,), np.int64)
    nsa = np.concatenate([o["q"], o["za"], o["ks"], o["vs"], o["kw"], o["vw"], o["kc"], o["vc"], o["g"], pad(116)])
    gm = np.concatenate([o["u"], o["v"], o["zb"]])
    cv = np.concatenate([o["ga"], o["gb"], o["zc"]])
    rw = np.concatenate([o["rw"], pad(64), o["zd"]])
    perm = np.concatenate([nsa, gm, cv, rw])
    assert perm.shape[0] == N_PROJ
    return perm


_PERM = _proj_perm()


def _rel_buckets(n):
    d = np.arange(n)
    nf = np.maximum(d, REL_MAX_EXACT).astype(np.float32)
    large = REL_MAX_EXACT + (np.log(nf / np.float32(REL_MAX_EXACT)) / np.float32(math.log(REL_MAX_DIST / REL_MAX_EXACT))
                             * np.float32(REL_BUCKETS - REL_MAX_EXACT)).astype(np.int32)
    large = np.minimum(large, REL_BUCKETS - 1)
    return np.where(d < REL_MAX_EXACT, d, large)


def _bdot(a, b):
    return jnp.dot(a.astype(BF16), b.astype(BF16), preferred_element_type=F32)


def _bdot_nt(a, b):
    return lax.dot_general(a.astype(BF16), b.astype(BF16), (((1,), (1,)), ((), ())), preferred_element_type=F32)


def _split2(a):
    hi = a.astype(BF16)
    lo = (a - hi.astype(F32)).astype(BF16)
    return hi, lo


def _split3(a):
    hi = a.astype(BF16)
    r = a - hi.astype(F32)
    mid = r.astype(BF16)
    lo = (r - mid.astype(F32)).astype(BF16)
    return hi, mid, lo


def _dot3(a, b):
    ah, al = _split2(a)
    bh, bl = _split2(b)
    d = lambda x, y: jnp.dot(x, y, preferred_element_type=F32)
    return d(ah, bh) + (d(ah, bl) + d(al, bh))


def _dot3_nt(a, b):
    ah, al = _split2(a)
    bh, bl = _split2(b)
    d = lambda x, y: lax.dot_general(x, y, (((1,), (1,)), ((), ())), preferred_element_type=F32)
    return d(ah, bh) + (d(ah, bl) + d(al, bh))


def _dot_sel_lhs(a01, b):
    a = a01.astype(BF16)
    bh, bm, bl = _split3(b)
    d = lambda y: jnp.dot(a, y, preferred_element_type=F32)
    return d(bh) + (d(bm) + d(bl))


def _dot_sel_rhs(a, b01):
    b = b01.astype(BF16)
    ah, am, al = _split3(a)
    d = lambda x: jnp.dot(x, b, preferred_element_type=F32)
    return d(ah) + (d(am) + d(al))


def _sigmoid(x):
    return 1.0 / (1.0 + jnp.exp(-x))


def _silu(x):
    return x * _sigmoid(x)


def _gelu_tanh(x):
    c = math.sqrt(2.0 / math.pi)
    return 0.5 * x * (1.0 + jnp.tanh(c * (x + 0.044715 * (x * x * x))))


def _softplus(x):
    return jnp.maximum(x, 0.0) + jnp.log(1.0 + jnp.exp(-jnp.abs(x)))


def _layer_norm(x, g, b):
    mu = jnp.mean(x, axis=-1, keepdims=True)
    xc = x - mu
    var = jnp.mean(xc * xc, axis=-1, keepdims=True)
    return xc * lax.rsqrt(var + LN_EPS) * g + b


def _block_diag(x, n):
    r, c = x.shape
    t = jnp.concatenate([x] * n, axis=0)
    ri = lax.broadcasted_iota(jnp.int32, t.shape, 0) // r
    ci = lax.broadcasted_iota(jnp.int32, t.shape, 1) // (c // n)
    return jnp.where(ri == ci, t, 0.0)


def _cparams(sem):
    return pltpu.CompilerParams(dimension_semantics=sem, vmem_limit_bytes=VMEM_LIMIT)


def _proj_kernel(x_ref, g_ref, w_ref, o_nsa, o_gm, o_cv, o_rw):
    x = x_ref[...]
    ms = jnp.mean(x * x, axis=-1, keepdims=True)
    h = (x * lax.rsqrt(ms + NORM_EPS) * g_ref[...]).astype(BF16)
    off = 0
    for o in (o_nsa, o_gm, o_cv, o_rw):
        wd = o.shape[-1]
        o[...] = jnp.dot(h, w_ref[:, off:off + wd], preferred_element_type=F32)
        off += wd


def _in_proj(x2, g_pre, w_p, tm=256):
    n, d = x2.shape
    widths = (NSA_W, GM_W, CV_W, RW_W)
    return pl.pallas_call(
        _proj_kernel,
        grid=(n // tm,),
        in_specs=[pl.BlockSpec((tm, d), lambda i: (i, 0)),
                  pl.BlockSpec((1, d), lambda i: (0, 0)),
                  pl.BlockSpec((d, N_PROJ), lambda i: (0, 0))],
        out_specs=[pl.BlockSpec((tm, w), lambda i: (i, 0)) for w in widths],
        out_shape=[jax.ShapeDtypeStruct((n, w), F32) for w in widths],
        compiler_params=_cparams(("parallel",)),
        name="in_proj",
    )(x2, g_pre.reshape(1, d), w_p)


def _cmp_kernel(ch_ref, pos_ref, w1_ref, w2_ref, o_ref, ot_ref):
    ch = ch_ref[...]
    w1 = w1_ref[...]
    half = ch.shape[1]
    nc = ch.shape[0]
    a = _dot3(ch, w1[:half])
    b = _dot3(ch, w1[half:])
    b_next = pltpu.roll(b, nc - 1, axis=0)
    posc = _dot3(jnp.broadcast_to(pos_ref[...], (8, 2 * half)), w1)[0:1]
    hid = a + b_next + posc
    out = _dot3(_silu(hid), w2_ref[...])
    o_ref[...] = out
    wide = jnp.concatenate([out, jnp.zeros_like(out)], axis=1)
    ot_ref[...] = wide.T[:out.shape[1]]


def _nsa_compress(chf, pos, w1, w2):
    _, b, nc, wdt = chf.shape
    dk = NSA_DK
    return pl.pallas_call(
        _cmp_kernel,
        grid=(2, b),
        in_specs=[pl.BlockSpec((None, None, nc, wdt), lambda s, i: (s, i, 0, 0)),
                  pl.BlockSpec((None, 1, 2 * wdt), lambda s, i: (s, 0, 0)),
                  pl.BlockSpec((None, 2 * wdt, CMP_HIDDEN), lambda s, i: (s, 0, 0)),
                  pl.BlockSpec((None, CMP_HIDDEN, dk), lambda s, i: (s, 0, 0))],
        out_specs=[pl.BlockSpec((None, None, nc, dk), lambda s, i: (s, i, 0, 0)),
                   pl.BlockSpec((None, None, dk, nc), lambda s, i: (s, i, 0, 0))],
        out_shape=[jax.ShapeDtypeStruct((2, b, nc, dk), F32),
                   jax.ShapeDtypeStruct((2, b, dk, nc), F32)],
        compiler_params=_cparams(("parallel", "parallel")),
        name="nsa_compress",
    )(chf, pos.reshape(2, 1, 2 * wdt), w1, w2)


def _nsa_kernel(q_ref, za_ref, g_ref, ks_ref, kw_ref, kc_ref, vct_ref, pt_ref, bt_ref, imat_ref,
                o_ref, ksb, vst, kwb, vwt, madd, m_s, l_s, acc_s, y_s, *, n_top, n_win):
    c = pl.program_id(1)
    n_qt = pl.num_programs(1)
    n_kt = ksb.shape[0]
    hd = HEAD_DIM

    @pl.when(c == 0)
    def _prep():
        def body(i, carry):
            r0 = pl.multiple_of(i * Q_TILE, Q_TILE)
            t = ks_ref[pl.ds(r0, Q_TILE), :]
            ksb[i] = t[:, :hd].astype(BF16)
            vst[i] = t.T[hd:, :].astype(BF16)
            t = kw_ref[pl.ds(r0, Q_TILE), :]
            kwb[i] = t[:, :hd].astype(BF16)
            vwt[i] = t.T[hd:, :].astype(BF16)
            return carry
        lax.fori_loop(0, n_kt, body, 0)

    q = q_ref[...] * (hd ** -0.5)
    qf = [q[:, hd * h:hd * (h + 1)] for h in range(N_HEADS)]
    qh = [x.astype(BF16) for x in qf]
    g_t = _sigmoid(g_ref[...]).T

    def gate_row(h, br):
        return g_t[3 * h + br:3 * h + br + 1, :]

    kc = kc_ref[...]
    vct = vct_ref[...]
    n_c = kc.shape[0]
    off = pl.multiple_of((n_qt - 1 - c) * (Q_TILE // CMP_STRIDE), 8)
    pc = jnp.zeros((n_c, Q_TILE), F32)
    for h in range(N_HEADS):
        bias = pt_ref[h, pl.ds(off, n_c), :]
        s = _dot3_nt(kc, qf[h]) + bias
        valid = bias > 0.5 * NEG
        m = jnp.max(s, axis=0, keepdims=True)
        p = jnp.where(valid, jnp.exp(s - m), 0.0)
        l = jnp.sum(p, axis=0, keepdims=True)
        pn = p * jnp.where(l > 0.0, 1.0 / l, 0.0)
        pc = pc + pn
        y_s[h] = gate_row(h, 0) * _bdot(vct, pn)

    n_blk = imat_ref.shape[0]
    imp = _dot_sel_lhs(imat_ref[...], pc)
    tl = lax.broadcasted_iota(jnp.int32, (n_blk, Q_TILE), 1)
    jb = lax.broadcasted_iota(jnp.int32, (n_blk, Q_TILE), 0)
    cur = c * (Q_TILE // SEL_BLOCK) + tl // SEL_BLOCK
    causal = jb <= cur
    forced = (jb == 0) | (jb == cur) | (jb == cur - 1)
    val = jnp.where(causal, jnp.where(forced, FORCE, imp), NEG)
    cnt = jnp.zeros((n_blk, Q_TILE), F32)
    for i in range(n_blk):
        vi = val[i:i + 1, :]
        ge = jnp.where(vi >= val, 1.0, 0.0)
        gt = jnp.where(vi > val, 1.0, 0.0)
        cnt = cnt + jnp.where(jb > i, ge, gt)
    sel = causal & (cnt < float(n_top))
    madd[...] = jnp.where(sel, 0.0, NEG)

    def reset_state():
        m_s[...] = jnp.full(m_s.shape, NEG, F32)
        l_s[...] = jnp.zeros(l_s.shape, F32)
        acc_s[...] = jnp.zeros(acc_s.shape, F32)

    def online(h, s, vt):
        m_old = m_s[h]
        m_new = jnp.maximum(m_old, jnp.max(s, axis=0, keepdims=True))
        alpha = jnp.exp(m_old - m_new)
        p = jnp.exp(s - m_new)
        l_s[h] = alpha * l_s[h] + jnp.sum(p, axis=0, keepdims=True)
        acc_s[h] = alpha * acc_s[h] + jnp.dot(vt, p.astype(BF16), preferred_element_type=F32)
        m_s[h] = m_new

    def finish(br):
        for h in range(N_HEADS):
            y_s[h] = y_s[h] + (gate_row(h, br) / l_s[h]) * acc_s[h]

    reset_state()
    half = Q_TILE // SEL_BLOCK

    def sel_body(kt, carry):
        idx = jnp.minimum(c - kt, 2)
        rows = [jnp.broadcast_to(madd[pl.ds(half * kt + r, 1), :], (SEL_BLOCK, Q_TILE)) for r in range(half)]
        mt = jnp.concatenate(rows, axis=0)
        k_t = ksb[kt]
        v_t = vst[kt]
        for h in range(N_HEADS):
            s = lax.dot_general(k_t, qh[h], (((1,), (1,)), ((), ())), preferred_element_type=F32)
            online(h, s + bt_ref[idx, h] + mt, v_t)
        return carry
    lax.fori_loop(0, c + 1, sel_body, 0)
    finish(1)

    reset_state()

    def win_body(i, carry):
        kt = c - i
        k_t = kwb[kt]
        v_t = vwt[kt]
        for h in range(N_HEADS):
            s = lax.dot_general(k_t, qh[h], (((1,), (1,)), ((), ())), preferred_element_type=F32)
            online(h, s + bt_ref[i, h], v_t)
        return carry
    lax.fori_loop(0, jnp.minimum(c, n_win) + 1, win_body, 0)
    finish(2)

    y_t = jnp.concatenate([y_s[h] for h in range(N_HEADS)], axis=0)
    o_ref[...] = y_t.T * _silu(za_ref[...])


def _nsa_tables(rel_bias, s_len):
    n_qt = s_len // Q_TILE
    n_c = s_len // CMP_STRIDE
    n_win = WINDOW // Q_TILE
    bk = _rel_buckets(2 * Q_TILE)
    tab = rel_bias[bk].T
    tabc = tab - rel_bias[REL_BUCKETS - 1][:, None]
    kl = np.arange(Q_TILE)[:, None]
    tq = np.arange(Q_TILE)[None, :]
    d0 = tq - kl
    diag = jnp.where(d0 >= 0, tabc[:, np.clip(d0, 0, 2 * Q_TILE - 1)], NEG)
    prev = tabc[:, np.clip(Q_TILE + d0, 0, 2 * Q_TILE - 1)]
    zero = jnp.zeros_like(prev)
    edge = jnp.broadcast_to(jnp.where(kl > tq, 0.0, NEG).astype(F32), prev.shape)
    tiles = [diag, prev] + [zero] * (n_win - 2) + [edge]
    bt = jnp.stack(tiles, axis=0).astype(F32)
    per_q = Q_TILE // CMP_STRIDE
    n_rows = n_c + per_q * (n_qt - 1)
    r = np.arange(n_rows)[:, None]
    dc = tq - CMP_STRIDE * (r - per_q * (n_qt - 1)) - (CMP_BLOCK - 1)
    pt = jnp.where(dc >= 0, tabc[:, np.clip(dc, 0, Q_TILE)], NEG).astype(F32)
    n_blk = s_len // SEL_BLOCK
    ratio = SEL_BLOCK // CMP_STRIDE
    jj = np.arange(n_blk)[:, None]
    ii = np.arange(n_c)[None, :]
    imat = ((ii >= ratio * jj - 1) & (ii <= ratio * jj + ratio - 1) & (ii < n_c - 1)).astype(np.float32)
    return bt, pt, jnp.asarray(imat)


def _nsa_attention(nsa3, kc, vct, bt, pt, imat):
    b, s_len, _ = nsa3.shape
    n_qt = s_len // Q_TILE
    n_c = s_len // CMP_STRIDE
    n_blk = s_len // SEL_BLOCK
    n_win = WINDOW // Q_TILE
    hd = HEAD_DIM
    kern = functools.partial(_nsa_kernel, n_top=min(N_SEL, n_blk), n_win=n_win)
    return pl.pallas_call(
        kern,
        grid=(b, n_qt),
        in_specs=[pl.BlockSpec((None, Q_TILE, 256), lambda i, c: (i, c, 0)),
                  pl.BlockSpec((None, Q_TILE, 256), lambda i, c: (i, c, 1)),
                  pl.BlockSpec((None, Q_TILE, LANES), lambda i, c: (i, c, 7)),
                  pl.BlockSpec((None, s_len, LANES), lambda i, c: (i, 0, 4)),
                  pl.BlockSpec((None, s_len, LANES), lambda i, c: (i, 0, 5)),
                  pl.BlockSpec((None, n_c, hd), lambda i, c: (i, 0, 0)),
                  pl.BlockSpec((None, hd, n_c), lambda i, c: (i, 0, 0)),
                  pl.BlockSpec(pt.shape, lambda i, c: (0, 0, 0)),
                  pl.BlockSpec(bt.shape, lambda i, c: (0, 0, 0, 0)),
                  pl.BlockSpec(imat.shape, lambda i, c: (0, 0))],
        out_specs=pl.BlockSpec((None, Q_TILE, 256), lambda i, c: (i, c, 0)),
        out_shape=jax.ShapeDtypeStruct((b, s_len, 256), F32),
        scratch_shapes=[pltpu.VMEM((n_qt, Q_TILE, hd), BF16),
                        pltpu.VMEM((n_qt, hd, Q_TILE), BF16),
                        pltpu.VMEM((n_qt, Q_TILE, hd), BF16),
                        pltpu.VMEM((n_qt, hd, Q_TILE), BF16),
                        pltpu.VMEM((n_blk, Q_TILE), F32),
                        pltpu.VMEM((N_HEADS, 1, Q_TILE), F32),
                        pltpu.VMEM((N_HEADS, 1, Q_TILE), F32),
                        pltpu.VMEM((N_HEADS, hd, Q_TILE), F32),
                        pltpu.VMEM((N_HEADS, hd, Q_TILE), F32)],
        compiler_params=_cparams(("arbitrary", "arbitrary")),
        name="nsa_attention",
    )(nsa3, nsa3, nsa3, nsa3, nsa3, kc, vct, pt, bt, imat)


def _gc_kernel(gm_ref, cv_ref, halo_ref, wall_ref, sb_ref, lng_ref, lnb_ref,
               cw_ref, cb_ref, clg_ref, clb_ref, cpw_ref, cpb_ref, ob_ref, oc_ref, xs, *, n_ct):
    c = pl.program_id(0) % n_ct
    t_len = gm_ref.shape[0]
    w = W_GRP
    gm = gm_ref[...]
    u = _gelu_tanh(gm[:, :w])
    v = _layer_norm(_gelu_tanh(gm[:, w:2 * w]), lng_ref[...], lnb_ref[...])
    wall = wall_ref[...]
    ti = lax.broadcasted_iota(jnp.int32, wall.shape, 0)
    si = lax.broadcasted_iota(jnp.int32, wall.shape, 1) % t_len
    wall = jnp.where(si <= ti, wall, 0.0)
    sv = _bdot(wall, _block_diag(v, N_HEADS)) + sb_ref[...]
    ob_ref[...] = u * sv * _silu(gm[:, 2 * w:])
    cv = cv_ref[...]
    hl = halo_ref[...]
    hx = hl[:, :w] * _sigmoid(hl[:, w:2 * w])
    hrows = hl.shape[0]
    xs[0:hrows, :] = jnp.where(c == 0, 0.0, hx)
    xs[hrows:hrows + t_len, :] = cv[:, :w] * _sigmoid(cv[:, w:2 * w])
    cw = cw_ref[...]
    acc = jnp.zeros((t_len, w), F32)
    base = hrows - (CONV_WIDTH - 1)
    for j in range(CONV_WIDTH):
        acc = acc + cw[j:j + 1, :] * xs[base + j:base + j + t_len, :]
    y = _layer_norm(acc + cb_ref[...], clg_ref[...], clb_ref[...])
    y = _bdot(_silu(y), cpw_ref[...]) + cpb_ref[...]
    oc_ref[...] = y * _silu(cv[:, 2 * w:])


def _gmlp_conv(gm, cv, s_len, wall, sb, lng, lnb, cw, cb, clg, clb, cpw, cpb, t_len=128, halo=32):
    n = gm.shape[0]
    n_ct = s_len // t_len
    w = W_GRP
    row = lambda a: a.reshape(1, w)
    full = lambda a: pl.BlockSpec(a.shape, lambda i: (0,) * a.ndim)
    cwp = jnp.concatenate([cw, jnp.zeros((32 - CONV_WIDTH, w), F32)], axis=0)
    args = (wall, sb, row(lng), row(lnb), cwp, row(cb), row(clg), row(clb), cpw.astype(BF16), row(cpb))
    per = t_len // halo
    return pl.pallas_call(
        functools.partial(_gc_kernel, n_ct=n_ct),
        grid=(n // t_len,),
        in_specs=[pl.BlockSpec((t_len, GM_W), lambda i: (i, 0)),
                  pl.BlockSpec((t_len, CV_W), lambda i: (i, 0)),
                  pl.BlockSpec((halo, CV_W), lambda i: (jnp.maximum(i * per - 1, 0), 0))]
                 + [full(a) for a in args],
        out_specs=[pl.BlockSpec((t_len, w), lambda i: (i, 0))] * 2,
        out_shape=[jax.ShapeDtypeStruct((n, w), F32)] * 2,
        scratch_shapes=[pltpu.VMEM((halo + t_len, w), F32)],
        compiler_params=_cparams(("parallel",)),
        name="gmlp_conv",
    )(gm, cv, cv, *args)


def _rwkv_kernel(rw_ref, prev_ref, mu_ref, wup_ref, aup_ref, vec_ref, o_ref, st):
    c = pl.program_id(0)
    n_b = rw_ref.shape[0]
    L = rw_ref.shape[1]
    w = W_GRP
    nh = N_HEADS
    hd = HEAD_DIM

    @pl.when(c == 0)
    def _init():
        st[...] = jnp.zeros(st.shape, F32)

    vec = vec_ref[...]
    w0, a0, k_k, k_a, r_k, gn_g, gn_b = [vec[i:i + 1, :] for i in range(7)]
    mu = mu_ref[...]
    rows = lax.broadcasted_iota(jnp.int32, (L, w), 0)
    lane = lax.broadcasted_iota(jnp.int32, (L, w), 1)
    s_of = lane % hd
    ones_bd = jnp.where((lax.broadcasted_iota(jnp.int32, (w, w), 0) // hd)
                        == (lax.broadcasted_iota(jnp.int32, (w, w), 1) // hd), 1.0, 0.0)
    tri = jnp.where(lax.broadcasted_iota(jnp.int32, (L, L), 1) <= lax.broadcasted_iota(jnp.int32, (L, L), 0), 1.0, 0.0)
    bd_mask = (lax.broadcasted_iota(jnp.int32, (w, w), 0) // hd) == (lax.broadcasted_iota(jnp.int32, (w, w), 1) // hd)
    strict = s_of < rows
    incl = s_of <= rows
    eye_all = jnp.where(s_of == rows, 1.0, 0.0)

    def bd(x):
        return jnp.where(bd_mask, jnp.concatenate([x] * nh, axis=0), 0.0)

    for b in range(n_b):
        z = rw_ref[b]
        zrow = lax.broadcasted_iota(jnp.int32, z.shape, 0)
        last = jnp.where(c == 0, 0.0, prev_ref[b, prev_ref.shape[1] - 1:prev_ref.shape[1], :])
        zprev = jnp.where(zrow == 0, last, pltpu.roll(z, 1, axis=0))
        xs = z + mu * (zprev - z)
        r = xs[:, 0:w]
        k = xs[:, w:2 * w]
        v = xs[:, 2 * w:3 * w]
        wa = xs[:, 3 * w:3 * w + LANES]
        zd = z[:, 3 * w + LANES:]

        zz = w0 + _dot3(jnp.tanh(wa), wup_ref[...])
        lw = -jnp.exp(-_softplus(-zz) - 0.5)
        a = _sigmoid(a0 + _dot3(wa, aup_ref[...]))
        kkr = k * k_k
        kk = kkr / jnp.maximum(jnp.sqrt(_dot_sel_rhs(kkr * kkr, ones_bd)), 1e-12)
        k2 = k * (1.0 + (a - 1.0) * k_a)
        bb = kk * a

        cs = _dot_sel_lhs(tri, lw)
        g_t = jnp.exp(cs)
        g_prev = jnp.exp(cs - lw)
        g_inv = jnp.exp(-cs)
        g_last = g_t[L - 1:L, :]
        kq = kk * g_prev
        rq = r * g_t
        kt = k2 * g_inv
        bt = bb * g_inv

        lhs = jnp.concatenate([kq, rq], axis=0)
        ab_b = _dot3_nt(lhs, bd(bt))
        ab_k = _dot3_nt(lhs, bd(kt))
        a_b = jnp.where(strict, ab_b[:L], 0.0)
        b_b = jnp.where(incl, ab_b[L:], 0.0)
        a_k = jnp.where(strict, ab_k[:L], 0.0)
        b_k = jnp.where(incl, ab_k[L:], 0.0)

        npow = a_b
        tinv = eye_all - npow
        steps = int(math.log2(L)) - 1
        for i in range(steps):
            npow = _dot3(npow, bd(npow))
            tinv = tinv + _dot3(tinv, bd(npow))

        s0 = st[b]
        kh = _dot3_nt(lhs, s0)
        akv = _dot3(jnp.concatenate([a_k, b_k], axis=0), bd(v))
        u = _dot3(tinv, bd(kh[:L] + akv[:L]))
        y = kh[L:] + akv[L:] - _dot3(b_b, bd(u))

        vu_t = jnp.concatenate([v, u, jnp.zeros((w - 2 * L, w), F32)], axis=0).T
        kb = jnp.concatenate([kt, -bt, jnp.zeros((w - 2 * L, w), F32)], axis=0)
        d = _dot3(vu_t, kb)
        st[b] = g_last * (s0 + jnp.where(bd_mask, d, 0.0))

        mean = _dot_sel_rhs(y, ones_bd) * (1.0 / hd)
        yc = y - mean
        var = _dot_sel_rhs(yc * yc, ones_bd) * (1.0 / hd)
        yn = yc * lax.rsqrt(var + RWKV_GN_EPS) * gn_g + gn_b
        bonus = _dot_sel_rhs(r * k2 * r_k, ones_bd) * v
        o_ref[b] = (yn + bonus) * _silu(zd)


def _rwkv(rw3, mu_p, wup_p, aup_p, vec):
    b, s_len, _ = rw3.shape
    L = RW_CHUNK
    w = W_GRP
    full = lambda a: pl.BlockSpec(a.shape, lambda c: (0,) * a.ndim)
    return pl.pallas_call(
        _rwkv_kernel,
        grid=(s_len // L,),
        in_specs=[pl.BlockSpec((b, L, RW_W), lambda c: (0, c, 0)),
                  pl.BlockSpec((b, 8, RW_W), lambda c: (0, jnp.maximum(c * (L // 8) - 1, 0), 0)),
                  full(mu_p), full(wup_p), full(aup_p), full(vec)],
        out_specs=pl.BlockSpec((b, L, w), lambda c: (0, c, 0)),
        out_shape=jax.ShapeDtypeStruct((b, s_len, w), F32),
        scratch_shapes=[pltpu.VMEM((b, w, w), F32)],
        compiler_params=_cparams(("arbitrary",)),
        name="rwkv7",
    )(rw3, rw3, mu_p, wup_p, aup_p, vec)


def _out_kernel(ya_ref, yb_ref, yc_ref, yd_ref, x_ref, p_ref, wo_ref, gp_ref, pp_ref, pg_ref, o_ref):
    w = W_GRP
    acc = None
    for i, y in enumerate((ya_ref, yb_ref, yc_ref, yd_ref)):
        t = jnp.dot(y[...].astype(BF16), wo_ref[i * w:(i + 1) * w, :], preferred_element_type=F32)
        acc = t if acc is None else acc + t
    ms = jnp.mean(acc * acc, axis=-1, keepdims=True)
    x1 = x_ref[...] + acc * lax.rsqrt(ms + NORM_EPS) * gp_ref[...]
    gate = _sigmoid(jnp.dot(x1.astype(BF16), pg_ref[...], preferred_element_type=F32))
    pe = jnp.dot(p_ref[...].astype(BF16), pp_ref[...], preferred_element_type=F32)
    o_ref[...] = x1 + gate * pe


def _out_proj(ys, x2, p2, w_out, g_post, ple_proj, ple_gate, tm=256):
    n, d = x2.shape
    w = W_GRP
    full = lambda a: pl.BlockSpec(a.shape, lambda i: (0,) * a.ndim)
    wo = w_out.astype(BF16)
    gp = g_post.reshape(1, d)
    pp = ple_proj.astype(BF16)
    pg = ple_gate.astype(BF16)
    return pl.pallas_call(
        _out_kernel,
        grid=(n // tm,),
        in_specs=[pl.BlockSpec((tm, w), lambda i: (i, 0))] * 4
                 + [pl.BlockSpec((tm, d), lambda i: (i, 0)),
                    pl.BlockSpec((tm, p2.shape[1]), lambda i: (i, 0)),
                    full(wo), full(gp), full(pp), full(pg)],
        out_specs=pl.BlockSpec((tm, d), lambda i: (i, 0)),
        out_shape=jax.ShapeDtypeStruct((n, d), F32),
        compiler_params=_cparams(("parallel",)),
        name="out_proj",
    )(*ys, x2, p2, wo, gp, pp, pg)


def _layer(x2, p2, b, s_len, bt, pt, imat, w_in, w_out, g_pre, g_post, nsa_pos, nsa_w1, nsa_w2,
           sgu_ln_g, sgu_ln_b, sgu_w, sgu_b, conv_w, conv_b, conv_ln_g, conv_ln_b, conv_pw, conv_pw_b,
           rwkv_mu, rwkv_w0, rwkv_w_up, rwkv_a0, rwkv_a_up, rwkv_k_k, rwkv_k_a, rwkv_r_k,
           rwkv_gn_g, rwkv_gn_b, ple_proj, ple_gate):
    n = x2.shape[0]
    w = W_GRP
    w_p = jnp.where(_PERM[None, :] >= 0, w_in[:, np.maximum(_PERM, 0)], 0.0).astype(BF16)
    nsa, gm, cv, rw = _in_proj(x2, g_pre, w_p)

    n_c = s_len // CMP_STRIDE
    nsa3 = nsa.reshape(b, s_len, NSA_W)
    kvc = nsa3[:, :, 768:896].reshape(b, n_c, CMP_STRIDE, 2, NSA_DK)
    chf = jnp.transpose(kvc, (3, 0, 1, 2, 4)).reshape(2, b, n_c, CMP_STRIDE * NSA_DK)
    cmp_o, cmp_t = _nsa_compress(chf, nsa_pos, nsa_w1, nsa_w2)
    ya = _nsa_attention(nsa3, cmp_o[0], cmp_t[1], bt, pt, imat).reshape(n, w)

    t_len = sgu_w.shape[-1]
    wall = jnp.transpose(sgu_w, (1, 0, 2)).reshape(t_len, N_HEADS * t_len)
    sb = jnp.repeat(sgu_b.T, HEAD_DIM, axis=1)
    yb, yc = _gmlp_conv(gm, cv, s_len, wall, sb, sgu_ln_g, sgu_ln_b, conv_w, conv_b,
                        conv_ln_g, conv_ln_b, conv_pw, conv_pw_b, t_len=t_len)

    lora = RWKV_LORA
    mu_p = jnp.concatenate([rwkv_mu, jnp.zeros((RW_W - rwkv_mu.shape[0],), F32)]).reshape(1, RW_W)
    wup_p = jnp.zeros((LANES, w), F32).at[:lora].set(rwkv_w_up)
    aup_p = jnp.zeros((LANES, w), F32).at[lora:2 * lora].set(rwkv_a_up)
    vec = jnp.stack([rwkv_w0, rwkv_a0, rwkv_k_k, rwkv_k_a, rwkv_r_k.reshape(w), rwkv_gn_g, rwkv_gn_b,
                     jnp.zeros((w,), F32)], axis=0)
    yd = _rwkv(rw.reshape(b, s_len, RW_W), mu_p, wup_p, aup_p, vec).reshape(n, w)

    return _out_proj((ya, yb, yc, yd), x2, p2, w_out, g_post, ple_proj, ple_gate)


def kernel(x, p, rel_bias, w_in, w_out, g_pre, g_post, nsa_pos, nsa_w1, nsa_w2, sgu_ln_g, sgu_ln_b, sgu_w, sgu_b, conv_w, conv_b, conv_ln_g, conv_ln_b, conv_pw, conv_pw_b, rwkv_mu, rwkv_w0, rwkv_w_up, rwkv_a0, rwkv_a_up, rwkv_k_k, rwkv_k_a, rwkv_r_k, rwkv_gn_g, rwkv_gn_b, ple_proj, ple_gate):
    b, s_len, d = x.shape
    depth = w_in.shape[0]
    bt, pt, imat = _nsa_tables(rel_bias, s_len)
    x2 = x.reshape(b * s_len, d)
    per_layer = (w_in, w_out, g_pre, g_post, nsa_pos, nsa_w1, nsa_w2, sgu_ln_g, sgu_ln_b, sgu_w, sgu_b,
                 conv_w, conv_b, conv_ln_g, conv_ln_b, conv_pw, conv_pw_b, rwkv_mu, rwkv_w0, rwkv_w_up,
                 rwkv_a0, rwkv_a_up, rwkv_k_k, rwkv_k_a, rwkv_r_k, rwkv_gn_g, rwkv_gn_b, ple_proj, ple_gate)
    for i in range(depth):
        x2 = _layer(x2, p[i].reshape(b * s_len, -1), b, s_len, bt, pt, imat, *[a[i] for a in per_layer])
    return x2.reshape(b, s_len, d)
```

```python
import functools
import math

import numpy as np
import jax
import jax.numpy as jnp
from jax import lax
from jax.experimental import pallas as pl
from jax.experimental.pallas import tpu as pltpu

F32 = jnp.float32
BF16 = jnp.bfloat16

W_GRP = 256
HEAD_DIM = 64
N_HEADS = 4
NSA_DK = 64
CMP_STRIDE = 16
CMP_BLOCK = 32
CMP_HIDDEN = 128
SEL_BLOCK = 64
N_SEL = 16
WINDOW = 512
REL_BUCKETS = 32
REL_MAX_EXACT = 16
REL_MAX_DIST = 128
CONV_WIDTH = 31
RWKV_LORA = 32
RWKV_GN_EPS = 64e-5
NORM_EPS = 1e-6
LN_EPS = 1e-5
NEG = -1e30
FORCE = 1e4

LANES = 128
SUBLANES = 8
Q_TILE = 128
SEL_GROUP = 4
RW_CHUNK = 64
VMEM_LIMIT = 48 * 1024 * 1024

NSA_W, GM_W, CV_W, RW_W = 1024, 768, 768, 1152
N_PROJ = NSA_W + GM_W + CV_W + RW_W


def _proj_segments():
    names = ["q", "kc", "vc", "ks", "vs", "kw", "vw", "g", "za", "u", "v", "zb", "ga", "gb", "zc", "rw", "zd"]
    widths = [256, 64, 64, 64, 64, 64, 64, 12, 256, 256, 256, 256, 256, 256, 256, 832, 256]
    o, off = {}, 0
    for n, w in zip(names, widths):
        o[n] = (off, off + w)
        off += w
    return [o["q"], o["za"], (o["ks"][0], o["vw"][1]), (o["kc"][0], o["vc"][1]), o["g"], (None, 116),
            (o["u"][0], o["zc"][1]), o["rw"], (None, 64), o["zd"]]


_SEGMENTS = _proj_segments()


def _rel_buckets(n):
    d = np.arange(n)
    nf = np.maximum(d, REL_MAX_EXACT).astype(np.float32)
    large = REL_MAX_EXACT + (np.log(nf / np.float32(REL_MAX_EXACT)) / np.float32(math.log(REL_MAX_DIST / REL_MAX_EXACT))
                             * np.float32(REL_BUCKETS - REL_MAX_EXACT)).astype(np.int32)
    large = np.minimum(large, REL_BUCKETS - 1)
    return np.where(d < REL_MAX_EXACT, d, large)


def _bdot(a, b):
    return jnp.dot(a.astype(BF16), b.astype(BF16), preferred_element_type=F32)


def _nt(a, b):
    return lax.dot_general(a, b, (((1,), (1,)), ((), ())), preferred_element_type=F32)


def _split2(a):
    hi = a.astype(BF16)
    lo = (a - hi.astype(F32)).astype(BF16)
    return hi, lo


def _split3(a):
    hi = a.astype(BF16)
    r = a - hi.astype(F32)
    mid = r.astype(BF16)
    lo = (r - mid.astype(F32)).astype(BF16)
    return hi, mid, lo


def _dot3(a, b):
    ah, al = _split2(a)
    bh, bl = _split2(b)
    d = lambda x, y: jnp.dot(x, y, preferred_element_type=F32)
    return d(ah, bh) + (d(ah, bl) + d(al, bh))


def _dot3_nt(a, b):
    ah, al = _split2(a)
    bh, bl = _split2(b)
    return _nt(ah, bh) + (_nt(ah, bl) + _nt(al, bh))


def _dot_sel_lhs(a01, b):
    a = a01.astype(BF16)
    bh, bm, bl = _split3(b)
    d = lambda y: jnp.dot(a, y, preferred_element_type=F32)
    return d(bh) + (d(bm) + d(bl))


def _dot_sel_rhs(a, b01):
    b = b01.astype(BF16)
    ah, am, al = _split3(a)
    d = lambda x: jnp.dot(x, b, preferred_element_type=F32)
    return d(ah) + (d(am) + d(al))


def _sigmoid(x):
    return 1.0 / (1.0 + jnp.exp(-x))


def _silu(x):
    return x * _sigmoid(x)


def _gelu_tanh(x):
    c = math.sqrt(2.0 / math.pi)
    return 0.5 * x * (1.0 + jnp.tanh(c * (x + 0.044715 * (x * x * x))))


def _softplus(x):
    return jnp.maximum(x, 0.0) + jnp.log(1.0 + jnp.exp(-jnp.abs(x)))


def _layer_norm(x, g, b):
    mu = jnp.mean(x, axis=-1, keepdims=True)
    xc = x - mu
    var = jnp.mean(xc * xc, axis=-1, keepdims=True)
    return xc * lax.rsqrt(var + LN_EPS) * g + b


def _block_diag(x, n):
    r, c = x.shape
    t = jnp.concatenate([x] * n, axis=0)
    ri = lax.broadcasted_iota(jnp.int32, t.shape, 0) // r
    ci = lax.broadcasted_iota(jnp.int32, t.shape, 1) // (c // n)
    return jnp.where(ri == ci, t, 0.0)


def _cparams(sem):
    return pltpu.CompilerParams(dimension_semantics=sem, vmem_limit_bytes=VMEM_LIMIT)


def _proj_kernel(x_ref, g_ref, w_ref, o_nsa, o_gm, o_cv, o_rw):
    x = x_ref[...]
    ms = jnp.mean(x * x, axis=-1, keepdims=True)
    h = (x * lax.rsqrt(ms + NORM_EPS) * g_ref[...]).astype(BF16)
    off = 0
    for o in (o_nsa, o_gm, o_cv, o_rw):
        wd = o.shape[-1]
        o[...] = jnp.dot(h, w_ref[:, off:off + wd], preferred_element_type=F32)
        off += wd


def _in_proj(x2, g_pre, w_p, tm=256):
    n, d = x2.shape
    widths = (NSA_W, GM_W, CV_W, RW_W)
    return pl.pallas_call(
        _proj_kernel,
        grid=(n // tm,),
        in_specs=[pl.BlockSpec((tm, d), lambda i: (i, 0)),
                  pl.BlockSpec((1, d), lambda i: (0, 0)),
                  pl.BlockSpec((d, N_PROJ), lambda i: (0, 0))],
        out_specs=[pl.BlockSpec((tm, w), lambda i: (i, 0)) for w in widths],
        out_shape=[jax.ShapeDtypeStruct((n, w), F32) for w in widths],
        compiler_params=_cparams(("parallel",)),
        name="in_proj",
    )(x2, g_pre.reshape(1, d), w_p)


def _cmp_kernel(ch_ref, pos_ref, w1_ref, w2_ref, o_ref, ot_ref):
    ch = ch_ref[...]
    w1 = w1_ref[...]
    half = ch.shape[1]
    nc = ch.shape[0]
    a = _dot3(ch, w1[:half])
    b = _dot3(ch, w1[half:])
    b_next = pltpu.roll(b, nc - 1, axis=0)
    posc = _dot3(jnp.broadcast_to(pos_ref[...], (SUBLANES, 2 * half)), w1)[0:1]
    hid = a + b_next + posc
    out = _dot3(_silu(hid), w2_ref[...])
    o_ref[...] = out
    wide = jnp.concatenate([out, jnp.zeros_like(out)], axis=1)
    ot_ref[...] = wide.T[:out.shape[1]]


def _nsa_compress(chf, pos, w1, w2):
    _, b, nc, wdt = chf.shape
    dk = NSA_DK
    return pl.pallas_call(
        _cmp_kernel,
        grid=(2, b),
        in_specs=[pl.BlockSpec((None, None, nc, wdt), lambda s, i: (s, i, 0, 0)),
                  pl.BlockSpec((None, 1, 2 * wdt), lambda s, i: (s, 0, 0)),
                  pl.BlockSpec((None, 2 * wdt, CMP_HIDDEN), lambda s, i: (s, 0, 0)),
                  pl.BlockSpec((None, CMP_HIDDEN, dk), lambda s, i: (s, 0, 0))],
        out_specs=[pl.BlockSpec((None, None, nc, dk), lambda s, i: (s, i, 0, 0)),
                   pl.BlockSpec((None, None, dk, nc), lambda s, i: (s, i, 0, 0))],
        out_shape=[jax.ShapeDtypeStruct((2, b, nc, dk), F32),
                   jax.ShapeDtypeStruct((2, b, dk, nc), F32)],
        compiler_params=_cparams(("parallel", "parallel")),
        name="nsa_compress",
    )(chf, pos.reshape(2, 1, 2 * wdt), w1, w2)


def _nsa_kernel(q_ref, za_ref, g_ref, ks_ref, kw_ref, kc_ref, vct_ref, pt_ref, bt_ref, imat_ref,
                o_ref, ksb, vst, kwb, vwt, madd, m_s, l_s, acc_s, *, n_top, n_win):
    c = pl.program_id(1)
    n_qt = pl.num_programs(1)
    n_kt = ksb.shape[0]
    hd = HEAD_DIM
    nh = N_HEADS
    hq = nh * Q_TILE

    @pl.when(c == 0)
    def _prep():
        def body(i, carry):
            r0 = pl.multiple_of(i * Q_TILE, Q_TILE)
            t = ks_ref[pl.ds(r0, Q_TILE), :]
            ksb[i] = t[:, :hd].astype(BF16)
            vst[i] = t.T[hd:, :].astype(BF16)
            t = kw_ref[pl.ds(r0, Q_TILE), :]
            kwb[i] = t[:, :hd].astype(BF16)
            vwt[i] = t.T[hd:, :].astype(BF16)
            return carry
        lax.fori_loop(0, n_kt, body, 0)

    q = q_ref[...] * (hd ** -0.5)
    q_all = jnp.concatenate([q[:, hd * h:hd * (h + 1)] for h in range(nh)], axis=0)
    qb = q_all.astype(BF16)
    g_t = _sigmoid(g_ref[...]).T
    gate = [jnp.concatenate([g_t[3 * h + br:3 * h + br + 1, :] for h in range(nh)], axis=1) for br in range(3)]

    def lanes4(x):
        return jnp.concatenate([x] * nh, axis=1)

    kc = kc_ref[...]
    n_c = kc.shape[0]
    off = pl.multiple_of((n_qt - 1 - c) * (Q_TILE // CMP_STRIDE), SUBLANES)
    bias = pt_ref[pl.ds(off, n_c), :]
    s = _dot3_nt(kc, q_all) + bias
    valid = bias > 0.5 * NEG
    m = jnp.max(s, axis=0, keepdims=True)
    p = jnp.where(valid, jnp.exp(s - m), 0.0)
    l = jnp.sum(p, axis=0, keepdims=True)
    pn = p * jnp.where(l > 0.0, 1.0 / l, 0.0)
    y_acc = gate[0] * _bdot(vct_ref[...], pn)
    pc = pn[:, 0:Q_TILE]
    for h in range(1, nh):
        pc = pc + pn[:, h * Q_TILE:(h + 1) * Q_TILE]

    n_blk = imat_ref.shape[0]
    tl = lax.broadcasted_iota(jnp.int32, (n_blk, Q_TILE), 1)
    jb = lax.broadcasted_iota(jnp.int32, (n_blk, Q_TILE), 0)
    per_q = Q_TILE // SEL_BLOCK
    cur = c * per_q + tl // SEL_BLOCK
    causal = jb <= cur
    need_rank = (c + 1) * per_q > n_top

    @pl.when(jnp.logical_not(need_rank))
    def _all_causal():
        madd[...] = jnp.where(causal, 0.0, NEG)

    @pl.when(need_rank)
    def _rank():
        imp = _dot_sel_lhs(imat_ref[...], pc)
        forced = (jb == 0) | (jb == cur) | (jb == cur - 1)
        val = jnp.where(causal, jnp.where(forced, FORCE, imp), NEG)
        n_r = n_blk // SUBLANES
        blocks = [val[SUBLANES * r:SUBLANES * (r + 1), :] for r in range(n_r)]
        cnts = [jnp.zeros((SUBLANES, Q_TILE), F32) for _ in range(n_r)]
        jrow = lax.broadcasted_iota(jnp.int32, (SUBLANES, Q_TILE), 0)
        for i in range(n_blk):
            vi = jnp.broadcast_to(val[i:i + 1, :], (SUBLANES, Q_TILE))
            for r in range(n_r):
                if SUBLANES * r > i:
                    beats = vi >= blocks[r]
                elif SUBLANES * r + SUBLANES - 1 < i:
                    beats = vi > blocks[r]
                else:
                    ge = jnp.where(vi >= blocks[r], 1.0, 0.0)
                    gt = jnp.where(vi > blocks[r], 1.0, 0.0)
                    cnts[r] = cnts[r] + jnp.where(jrow + SUBLANES * r > i, ge, gt)
                    continue
                cnts[r] = cnts[r] + jnp.where(beats, 1.0, 0.0)
        cnt = jnp.concatenate(cnts, axis=0)
        madd[...] = jnp.where(causal & (cnt < float(n_top)), 0.0, NEG)

    def reset_state():
        m_s[...] = jnp.full(m_s.shape, NEG, F32)
        l_s[...] = jnp.zeros(l_s.shape, F32)
        acc_s[...] = jnp.zeros(acc_s.shape, F32)

    def update(ss, vts):
        m_old = m_s[...]
        m_new = m_old
        for s_i in ss:
            m_new = jnp.maximum(m_new, jnp.max(s_i, axis=0, keepdims=True))
        alpha = jnp.exp(m_old - m_new)
        l_new = alpha * l_s[...]
        pv = None
        for s_i, v_i in zip(ss, vts):
            p_i = jnp.exp(s_i - m_new)
            l_new = l_new + jnp.sum(p_i, axis=0, keepdims=True)
            t = jnp.dot(v_i, p_i.astype(BF16), preferred_element_type=F32)
            pv = t if pv is None else pv + t
        l_s[...] = l_new
        acc_s[...] = alpha * acc_s[...] + pv
        m_s[...] = m_new

    def sel_mask(kt, extra):
        rows = [jnp.broadcast_to(madd[pl.ds(per_q * kt + r, 1), :] + extra, (SEL_BLOCK, Q_TILE)) for r in range(per_q)]
        return lanes4(jnp.concatenate(rows, axis=0))

    reset_state()
    kt1 = jnp.maximum(c - 1, 0)
    s0 = _nt(ksb[c], qb) + bt_ref[0] + sel_mask(c, 0.0)
    s1 = _nt(ksb[kt1], qb) + bt_ref[1] + sel_mask(kt1, jnp.where(c >= 1, 0.0, NEG))
    update([s0, s1], [vst[c], vst[kt1]])

    n_far = jnp.maximum(c - 1, 0)

    def far_body(g, carry):
        ss, vts = [], []
        for j in range(SEL_GROUP):
            kt = g * SEL_GROUP + j
            ss.append(_nt(ksb[kt], qb) + sel_mask(kt, jnp.where(kt < n_far, 0.0, NEG)))
            vts.append(vst[kt])
        update(ss, vts)
        return carry
    lax.fori_loop(0, (n_far + SEL_GROUP - 1) // SEL_GROUP, far_body, 0)
    y_acc = y_acc + (gate[1] / l_s[...]) * acc_s[...]

    reset_state()
    ss, vts = [], []
    for i in range(n_win + 1):
        kt = jnp.maximum(c - i, 0)
        s_i = _nt(kwb[kt], qb)
        if i == 0:
            s_i = s_i + bt_ref[0]
        else:
            if i == 1:
                s_i = s_i + bt_ref[1]
            elif i == n_win:
                s_i = s_i + bt_ref[2]
            s_i = s_i + jnp.where(c >= i, 0.0, NEG)
        ss.append(s_i)
        vts.append(vwt[kt])
    update(ss, vts)
    y_acc = y_acc + (gate[2] / l_s[...]) * acc_s[...]

    y_t = jnp.concatenate([y_acc[:, h * Q_TILE:(h + 1) * Q_TILE] for h in range(nh)], axis=0)
    o_ref[...] = y_t.T * _silu(za_ref[...])


def _bucket_lookup(relc, idx, visible):
    r, cc = idx.shape
    onehot = (jnp.asarray(idx.reshape(1, -1)) == jnp.arange(relc.shape[0])[:, None]).astype(F32)
    vals = jnp.einsum("bh,bn->hn", relc, onehot, precision=lax.Precision.HIGHEST).reshape(-1, r, cc)
    vals = jnp.where(jnp.asarray(visible)[None], vals, NEG)
    return jnp.transpose(vals, (1, 0, 2)).reshape(r, -1).astype(F32)


def _nsa_tables(rel_bias, s_len):
    n_qt = s_len // Q_TILE
    n_c = s_len // CMP_STRIDE
    relc = rel_bias - rel_bias[REL_BUCKETS - 1][None, :]
    bk = _rel_buckets(s_len + Q_TILE)
    kl = np.arange(Q_TILE)[:, None]
    tq = np.arange(Q_TILE)[None, :]
    d0 = tq - kl
    diag = _bucket_lookup(relc, bk[np.clip(d0, 0, None)], d0 >= 0)
    prev = _bucket_lookup(relc, bk[Q_TILE + d0], np.ones_like(d0, bool))
    edge = jnp.asarray(np.tile(np.where(kl > tq, 0.0, NEG).astype(np.float32), (1, N_HEADS)))
    bt = jnp.stack([diag, prev, edge], axis=0)
    per_q = Q_TILE // CMP_STRIDE
    n_rows = n_c + per_q * (n_qt - 1)
    r = np.arange(n_rows)[:, None]
    dc = tq - CMP_STRIDE * (r - per_q * (n_qt - 1)) - (CMP_BLOCK - 1)
    pt = _bucket_lookup(relc, bk[np.clip(dc, 0, None)], dc >= 0)
    n_blk = s_len // SEL_BLOCK
    ratio = SEL_BLOCK // CMP_STRIDE
    jj = np.arange(n_blk)[:, None]
    ii = np.arange(n_c)[None, :]
    imat = ((ii >= ratio * jj - 1) & (ii <= ratio * jj + ratio - 1) & (ii < n_c - 1)).astype(np.float32)
    return bt, pt, jnp.asarray(imat)


def _nsa_attention(nsa3, kc, vct, bt, pt, imat):
    b, s_len, _ = nsa3.shape
    n_qt = s_len // Q_TILE
    n_c = s_len // CMP_STRIDE
    n_blk = s_len // SEL_BLOCK
    n_win = WINDOW // Q_TILE
    hd = HEAD_DIM
    hq = N_HEADS * Q_TILE
    kern = functools.partial(_nsa_kernel, n_top=min(N_SEL, n_blk), n_win=n_win)
    return pl.pallas_call(
        kern,
        grid=(b, n_qt),
        in_specs=[pl.BlockSpec((None, Q_TILE, 256), lambda i, c: (i, c, 0)),
                  pl.BlockSpec((None, Q_TILE, 256), lambda i, c: (i, c, 1)),
                  pl.BlockSpec((None, Q_TILE, LANES), lambda i, c: (i, c, 7)),
                  pl.BlockSpec((None, s_len, LANES), lambda i, c: (i, 0, 4)),
                  pl.BlockSpec((None, s_len, LANES), lambda i, c: (i, 0, 5)),
                  pl.BlockSpec((None, n_c, hd), lambda i, c: (i, 0, 0)),
                  pl.BlockSpec((None, hd, n_c), lambda i, c: (i, 0, 0)),
                  pl.BlockSpec(pt.shape, lambda i, c: (0, 0)),
                  pl.BlockSpec(bt.shape, lambda i, c: (0, 0, 0)),
                  pl.BlockSpec(imat.shape, lambda i, c: (0, 0))],
        out_specs=pl.BlockSpec((None, Q_TILE, 256), lambda i, c: (i, c, 0)),
        out_shape=jax.ShapeDtypeStruct((b, s_len, 256), F32),
        scratch_shapes=[pltpu.VMEM((n_qt, Q_TILE, hd), BF16),
                        pltpu.VMEM((n_qt, hd, Q_TILE), BF16),
                        pltpu.VMEM((n_qt, Q_TILE, hd), BF16),
                        pltpu.VMEM((n_qt, hd, Q_TILE), BF16),
                        pltpu.VMEM((n_blk, Q_TILE), F32),
                        pltpu.VMEM((1, hq), F32),
                        pltpu.VMEM((1, hq), F32),
                        pltpu.VMEM((hd, hq), F32)],
        compiler_params=_cparams(("arbitrary", "arbitrary")),
        name="nsa_attention",
    )(nsa3, nsa3, nsa3, nsa3, nsa3, kc, vct, pt, bt, imat)


def _gc_kernel(gm_ref, cv_ref, halo_ref, wall_ref, sb_ref, lng_ref, lnb_ref,
               cw_ref, cb_ref, clg_ref, clb_ref, cpw_ref, cpb_ref, ob_ref, oc_ref, xs, *, n_ct):
    c = pl.program_id(0) % n_ct
    t_len = gm_ref.shape[0]
    w = W_GRP
    gm = gm_ref[...]
    u = _gelu_tanh(gm[:, :w])
    v = _layer_norm(_gelu_tanh(gm[:, w:2 * w]), lng_ref[...], lnb_ref[...])
    wall = wall_ref[...]
    ti = lax.broadcasted_iota(jnp.int32, wall.shape, 0)
    si = lax.broadcasted_iota(jnp.int32, wall.shape, 1) % t_len
    wall = jnp.where(si <= ti, wall, 0.0)
    sv = _bdot(wall, _block_diag(v, N_HEADS)) + sb_ref[...]
    ob_ref[...] = u * sv * _silu(gm[:, 2 * w:])
    cv = cv_ref[...]
    hl = halo_ref[...]
    hx = hl[:, :w] * _sigmoid(hl[:, w:2 * w])
    hrows = hl.shape[0]
    xs[0:hrows, :] = jnp.where(c == 0, 0.0, hx)
    xs[hrows:hrows + t_len, :] = cv[:, :w] * _sigmoid(cv[:, w:2 * w])
    cw = cw_ref[...]
    acc = jnp.zeros((t_len, w), F32)
    base = hrows - (CONV_WIDTH - 1)
    for j in range(CONV_WIDTH):
        acc = acc + cw[j:j + 1, :] * xs[base + j:base + j + t_len, :]
    y = _layer_norm(acc + cb_ref[...], clg_ref[...], clb_ref[...])
    y = _bdot(_silu(y), cpw_ref[...]) + cpb_ref[...]
    oc_ref[...] = y * _silu(cv[:, 2 * w:])


def _gmlp_conv(gm, cv, s_len, wall, sb, lng, lnb, cw, cb, clg, clb, cpw, cpb, t_len=128, halo=32):
    n = gm.shape[0]
    n_ct = s_len // t_len
    w = W_GRP
    row = lambda a: a.reshape(1, w)
    full = lambda a: pl.BlockSpec(a.shape, lambda i: (0,) * a.ndim)
    cwp = jnp.concatenate([cw, jnp.zeros((32 - CONV_WIDTH, w), F32)], axis=0)
    args = (wall, sb, row(lng), row(lnb), cwp, row(cb), row(clg), row(clb), cpw.astype(BF16), row(cpb))
    per = t_len // halo
    return pl.pallas_call(
        functools.partial(_gc_kernel, n_ct=n_ct),
        grid=(n // t_len,),
        in_specs=[pl.BlockSpec((t_len, GM_W), lambda i: (i, 0)),
                  pl.BlockSpec((t_len, CV_W), lambda i: (i, 0)),
                  pl.BlockSpec((halo, CV_W), lambda i: (jnp.maximum(i * per - 1, 0), 0))]
                 + [full(a) for a in args],
        out_specs=[pl.BlockSpec((t_len, w), lambda i: (i, 0))] * 2,
        out_shape=[jax.ShapeDtypeStruct((n, w), F32)] * 2,
        scratch_shapes=[pltpu.VMEM((halo + t_len, w), F32)],
        compiler_params=_cparams(("parallel",)),
        name="gmlp_conv",
    )(gm, cv, cv, *args)


def _rwkv_kernel(rw_ref, prev_ref, mu_ref, wup_ref, aup_ref, vec_ref, o_ref, st):
    c = pl.program_id(0)
    n_b = rw_ref.shape[0]
    L = rw_ref.shape[1]
    w = W_GRP
    nh = N_HEADS
    hd = HEAD_DIM

    @pl.when(c == 0)
    def _init():
        st[...] = jnp.zeros(st.shape, F32)

    vec = vec_ref[...]
    w0, a0, k_k, k_a, r_k, gn_g, gn_b = [vec[i:i + 1, :] for i in range(7)]
    mu = mu_ref[...]
    rows = lax.broadcasted_iota(jnp.int32, (L, w), 0)
    lane = lax.broadcasted_iota(jnp.int32, (L, w), 1)
    s_of = lane % hd
    ones_bd = jnp.where((lax.broadcasted_iota(jnp.int32, (w, w), 0) // hd)
                        == (lax.broadcasted_iota(jnp.int32, (w, w), 1) // hd), 1.0, 0.0)
    tri = jnp.where(lax.broadcasted_iota(jnp.int32, (L, L), 1) <= lax.broadcasted_iota(jnp.int32, (L, L), 0), 1.0, 0.0)
    bd_mask = (lax.broadcasted_iota(jnp.int32, (w, w), 0) // hd) == (lax.broadcasted_iota(jnp.int32, (w, w), 1) // hd)
    strict = s_of < rows
    incl = s_of <= rows
    eye_all = jnp.where(s_of == rows, 1.0, 0.0)

    def bd(x):
        return jnp.where(bd_mask, jnp.concatenate([x] * nh, axis=0), 0.0)

    for b in range(n_b):
        z = rw_ref[b]
        zrow = lax.broadcasted_iota(jnp.int32, z.shape, 0)
        last = jnp.where(c == 0, 0.0, prev_ref[b, prev_ref.shape[1] - 1:prev_ref.shape[1], :])
        zprev = jnp.where(zrow == 0, last, pltpu.roll(z, 1, axis=0))
        xs = z + mu * (zprev - z)
        r = xs[:, 0:w]
        k = xs[:, w:2 * w]
        v = xs[:, 2 * w:3 * w]
        wa = xs[:, 3 * w:3 * w + LANES]
        zd = z[:, 3 * w + LANES:]

        zz = w0 + _dot3(jnp.tanh(wa), wup_ref[...])
        lw = -jnp.exp(-_softplus(-zz) - 0.5)
        a = _sigmoid(a0 + _dot3(wa, aup_ref[...]))
        kkr = k * k_k
        kk = kkr / jnp.maximum(jnp.sqrt(_dot_sel_rhs(kkr * kkr, ones_bd)), 1e-12)
        k2 = k * (1.0 + (a - 1.0) * k_a)
        bb = kk * a

        cs = _dot_sel_lhs(tri, lw)
        g_t = jnp.exp(cs)
        g_prev = jnp.exp(cs - lw)
        g_inv = jnp.exp(-cs)
        g_last = g_t[L - 1:L, :]
        kq = kk * g_prev
        rq = r * g_t
        kt = k2 * g_inv
        bt = bb * g_inv

        lhs = jnp.concatenate([kq, rq], axis=0)
        ab_b = _dot3_nt(lhs, bd(bt))
        ab_k = _dot3_nt(lhs, bd(kt))
        a_b = jnp.where(strict, ab_b[:L], 0.0)
        b_b = jnp.where(incl, ab_b[L:], 0.0)
        a_k = jnp.where(strict, ab_k[:L], 0.0)
        b_k = jnp.where(incl, ab_k[L:], 0.0)

        npow = a_b
        tinv = eye_all - npow
        steps = int(math.log2(L)) - 1
        for i in range(steps):
            npow = _dot3(npow, bd(npow))
            tinv = tinv + _dot3(tinv, bd(npow))

        s0 = st[b]
        kh = _dot3_nt(lhs, s0)
        akv = _dot3(jnp.concatenate([a_k, b_k], axis=0), bd(v))
        u = _dot3(tinv, bd(kh[:L] + akv[:L]))
        y = kh[L:] + akv[L:] - _dot3(b_b, bd(u))

        vu_t = jnp.concatenate([v, u, jnp.zeros((w - 2 * L, w), F32)], axis=0).T
        kb = jnp.concatenate([kt, -bt, jnp.zeros((w - 2 * L, w), F32)], axis=0)
        d = _dot3(vu_t, kb)
        st[b] = g_last * (s0 + jnp.where(bd_mask, d, 0.0))

        mean = _dot_sel_rhs(y, ones_bd) * (1.0 / hd)
        yc = y - mean
        var = _dot_sel_rhs(yc * yc, ones_bd) * (1.0 / hd)
        yn = yc * lax.rsqrt(var + RWKV_GN_EPS) * gn_g + gn_b
        bonus = _dot_sel_rhs(r * k2 * r_k, ones_bd) * v
        o_ref[b] = (yn + bonus) * _silu(zd)


def _rwkv(rw3, mu_p, wup_p, aup_p, vec):
    b, s_len, _ = rw3.shape
    L = RW_CHUNK
    w = W_GRP
    full = lambda a: pl.BlockSpec(a.shape, lambda c: (0,) * a.ndim)
    return pl.pallas_call(
        _rwkv_kernel,
        grid=(s_len // L,),
        in_specs=[pl.BlockSpec((b, L, RW_W), lambda c: (0, c, 0)),
                  pl.BlockSpec((b, SUBLANES, RW_W), lambda c: (0, jnp.maximum(c * (L // SUBLANES) - 1, 0), 0)),
                  full(mu_p), full(wup_p), full(aup_p), full(vec)],
        out_specs=pl.BlockSpec((b, L, w), lambda c: (0, c, 0)),
        out_shape=jax.ShapeDtypeStruct((b, s_len, w), F32),
        scratch_shapes=[pltpu.VMEM((b, w, w), F32)],
        compiler_params=_cparams(("arbitrary",)),
        name="rwkv7",
    )(rw3, rw3, mu_p, wup_p, aup_p, vec)


def _out_kernel(ya_ref, yb_ref, yc_ref, yd_ref, x_ref, p_ref, wo_ref, gp_ref, pp_ref, pg_ref, o_ref):
    w = W_GRP
    acc = None
    for i, y in enumerate((ya_ref, yb_ref, yc_ref, yd_ref)):
        t = jnp.dot(y[...].astype(BF16), wo_ref[i * w:(i + 1) * w, :], preferred_element_type=F32)
        acc = t if acc is None else acc + t
    ms = jnp.mean(acc * acc, axis=-1, keepdims=True)
    x1 = x_ref[...] + acc * lax.rsqrt(ms + NORM_EPS) * gp_ref[...]
    gate = _sigmoid(jnp.dot(x1.astype(BF16), pg_ref[...], preferred_element_type=F32))
    pe = jnp.dot(p_ref[...].astype(BF16), pp_ref[...], preferred_element_type=F32)
    o_ref[...] = x1 + gate * pe


def _out_proj(ys, x2, p2, w_out, g_post, ple_proj, ple_gate, tm=256):
    n, d = x2.shape
    w = W_GRP
    full = lambda a: pl.BlockSpec(a.shape, lambda i: (0,) * a.ndim)
    wo = w_out.astype(BF16)
    gp = g_post.reshape(1, d)
    pp = ple_proj.astype(BF16)
    pg = ple_gate.astype(BF16)
    return pl.pallas_call(
        _out_kernel,
        grid=(n // tm,),
        in_specs=[pl.BlockSpec((tm, w), lambda i: (i, 0))] * 4
                 + [pl.BlockSpec((tm, d), lambda i: (i, 0)),
                    pl.BlockSpec((tm, p2.shape[1]), lambda i: (i, 0)),
                    full(wo), full(gp), full(pp), full(pg)],
        out_specs=pl.BlockSpec((tm, d), lambda i: (i, 0)),
        out_shape=jax.ShapeDtypeStruct((n, d), F32),
        compiler_params=_cparams(("parallel",)),
        name="out_proj",
    )(*ys, x2, p2, wo, gp, pp, pg)


def _layer(x2, p2, b, s_len, bt, pt, imat, w_in, w_out, g_pre, g_post, nsa_pos, nsa_w1, nsa_w2,
           sgu_ln_g, sgu_ln_b, sgu_w, sgu_b, conv_w, conv_b, conv_ln_g, conv_ln_b, conv_pw, conv_pw_b,
           rwkv_mu, rwkv_w0, rwkv_w_up, rwkv_a0, rwkv_a_up, rwkv_k_k, rwkv_k_a, rwkv_r_k,
           rwkv_gn_g, rwkv_gn_b, ple_proj, ple_gate):
    n = x2.shape[0]
    w = W_GRP
    w_bf = w_in.astype(BF16)
    w_p = jnp.concatenate([jnp.zeros((w_in.shape[0], b_), BF16) if a_ is None else w_bf[:, a_:b_]
                           for a_, b_ in _SEGMENTS], axis=1)
    nsa, gm, cv, rw = _in_proj(x2, g_pre, w_p)

    n_c = s_len // CMP_STRIDE
    nsa3 = nsa.reshape(b, s_len, NSA_W)
    kvc = nsa3[:, :, 768:896].reshape(b, n_c, CMP_STRIDE, 2, NSA_DK)
    chf = jnp.transpose(kvc, (3, 0, 1, 2, 4)).reshape(2, b, n_c, CMP_STRIDE * NSA_DK)
    cmp_o, cmp_t = _nsa_compress(chf, nsa_pos, nsa_w1, nsa_w2)
    ya = _nsa_attention(nsa3, cmp_o[0], cmp_t[1], bt, pt, imat).reshape(n, w)

    t_len = sgu_w.shape[-1]
    wall = jnp.transpose(sgu_w, (1, 0, 2)).reshape(t_len, N_HEADS * t_len)
    sb = jnp.repeat(sgu_b.T, HEAD_DIM, axis=1)
    yb, yc = _gmlp_conv(gm, cv, s_len, wall, sb, sgu_ln_g, sgu_ln_b, conv_w, conv_b,
                        conv_ln_g, conv_ln_b, conv_pw, conv_pw_b, t_len=t_len)

    lora = RWKV_LORA
    mu_p = jnp.concatenate([rwkv_mu, jnp.zeros((RW_W - rwkv_mu.shape[0],), F32)]).reshape(1, RW_W)
    wup_p = jnp.concatenate([rwkv_w_up, jnp.zeros((LANES - lora, w), F32)], axis=0)
    aup_p = jnp.concatenate([jnp.zeros((lora, w), F32), rwkv_a_up, jnp.zeros((LANES - 2 * lora, w), F32)], axis=0)
    vec = jnp.stack([rwkv_w0, rwkv_a0, rwkv_k_k, rwkv_k_a, rwkv_r_k.reshape(w), rwkv_gn_g, rwkv_gn_b,
                     jnp.zeros((w,), F32)], axis=0)
    yd = _rwkv(rw.reshape(b, s_len, RW_W), mu_p, wup_p, aup_p, vec).reshape(n, w)

    return _out_proj((ya, yb, yc, yd), x2, p2, w_out, g_post, ple_proj, ple_gate)


def kernel(x, p, rel_bias, w_in, w_out, g_pre, g_post, nsa_pos, nsa_w1, nsa_w2, sgu_ln_g, sgu_ln_b, sgu_w, sgu_b, conv_w, conv_b, conv_ln_g, conv_ln_b, conv_pw, conv_pw_b, rwkv_mu, rwkv_w0, rwkv_w_up, rwkv_a0, rwkv_a_up, rwkv_k_k, rwkv_k_a, rwkv_r_k, rwkv_gn_g, rwkv_gn_b, ple_proj, ple_gate):
    b, s_len, d = x.shape
    depth = w_in.shape[0]
    bt, pt, imat = _nsa_tables(rel_bias, s_len)
    x2 = x.reshape(b * s_len, d)
    per_layer = (w_in, w_out, g_pre, g_post, nsa_pos, nsa_w1, nsa_w2, sgu_ln_g, sgu_ln_b, sgu_w, sgu_b,
                 conv_w, conv_b, conv_ln_g, conv_ln_b, conv_pw, conv_pw_b, rwkv_mu, rwkv_w0, rwkv_w_up,
                 rwkv_a0, rwkv_a_up, rwkv_k_k, rwkv_k_a, rwkv_r_k, rwkv_gn_g, rwkv_gn_b, ple_proj, ple_gate)
    for i in range(depth):
        x2 = _layer(x2, p[i].reshape(b * s_len, -1), b, s_len, bt, pt, imat, *[a[i] for a in per_layer])
    return x2.reshape(b, s_len, d)
```

```python
import functools
import math

import numpy as np
import jax
import jax.numpy as jnp
from jax import lax
from jax.experimental import pallas as pl
from jax.experimental.pallas import tpu as pltpu

F32 = jnp.float32
BF16 = jnp.bfloat16

W_GRP = 256
HEAD_DIM = 64
N_HEADS = 4
NSA_DK = 64
CMP_STRIDE = 16
CMP_BLOCK = 32
CMP_HIDDEN = 128
SEL_BLOCK = 64
N_SEL = 16
WINDOW = 512
REL_BUCKETS = 32
REL_MAX_EXACT = 16
REL_MAX_DIST = 128
CONV_WIDTH = 31
RWKV_LORA = 32
RWKV_GN_EPS = 64e-5
NORM_EPS = 1e-6
LN_EPS = 1e-5
NEG = -1e30
FORCE = 1e4
LOG2E = math.log2(math.e)

LANES = 128
SUBLANES = 8
Q_TILE = 128
SEL_GROUP = 4
NSA_STREAMS = 2
RW_CHUNK = 64
VMEM_LIMIT = 48 * 1024 * 1024

NSA_W, GM_W, CV_W, RW_W = 1024, 768, 768, 1152
N_PROJ = NSA_W + GM_W + CV_W + RW_W


def _proj_segments():
    names = ["q", "kc", "vc", "ks", "vs", "kw", "vw", "g", "za", "u", "v", "zb", "ga", "gb", "zc", "rw", "zd"]
    widths = [256, 64, 64, 64, 64, 64, 64, 12, 256, 256, 256, 256, 256, 256, 256, 832, 256]
    o, off = {}, 0
    for n, w in zip(names, widths):
        o[n] = (off, off + w)
        off += w
    return [o["q"], o["za"], (o["ks"][0], o["vw"][1]), (o["kc"][0], o["vc"][1]), o["g"], (None, 116),
            (o["u"][0], o["zc"][1]), o["rw"], (None, 64), o["zd"]]


_SEGMENTS = _proj_segments()


def _rel_buckets(n):
    d = np.arange(n)
    nf = np.maximum(d, REL_MAX_EXACT).astype(np.float32)
    large = REL_MAX_EXACT + (np.log(nf / np.float32(REL_MAX_EXACT)) / np.float32(math.log(REL_MAX_DIST / REL_MAX_EXACT))
                             * np.float32(REL_BUCKETS - REL_MAX_EXACT)).astype(np.int32)
    large = np.minimum(large, REL_BUCKETS - 1)
    return np.where(d < REL_MAX_EXACT, d, large)


def _bdot(a, b):
    return jnp.dot(a.astype(BF16), b.astype(BF16), preferred_element_type=F32)


def _nt(a, b):
    return lax.dot_general(a, b, (((1,), (1,)), ((), ())), preferred_element_type=F32)


def _split2(a):
    hi = a.astype(BF16)
    lo = (a - hi.astype(F32)).astype(BF16)
    return hi, lo


def _split3(a):
    hi = a.astype(BF16)
    r = a - hi.astype(F32)
    mid = r.astype(BF16)
    lo = (r - mid.astype(F32)).astype(BF16)
    return hi, mid, lo


def _dot3(a, b):
    ah, al = _split2(a)
    bh, bl = _split2(b)
    d = lambda x, y: jnp.dot(x, y, preferred_element_type=F32)
    return d(ah, bh) + (d(ah, bl) + d(al, bh))


def _dot3_nt(a, b):
    ah, al = _split2(a)
    bh, bl = _split2(b)
    return _nt(ah, bh) + (_nt(ah, bl) + _nt(al, bh))


def _dot_sel_lhs(a01, b):
    a = a01.astype(BF16)
    bh, bm, bl = _split3(b)
    d = lambda y: jnp.dot(a, y, preferred_element_type=F32)
    return d(bh) + (d(bm) + d(bl))


def _dot_sel_rhs(a, b01):
    b = b01.astype(BF16)
    ah, am, al = _split3(a)
    d = lambda x: jnp.dot(x, b, preferred_element_type=F32)
    return d(ah) + (d(am) + d(al))


def _sigmoid(x):
    return 1.0 / (1.0 + jnp.exp(-x))


def _silu(x):
    return x * _sigmoid(x)


def _gelu_tanh(x):
    c = math.sqrt(2.0 / math.pi)
    return 0.5 * x * (1.0 + jnp.tanh(c * (x + 0.044715 * (x * x * x))))


def _softplus(x):
    return jnp.maximum(x, 0.0) + jnp.log(1.0 + jnp.exp(-jnp.abs(x)))


def _layer_norm(x, g, b):
    mu = jnp.mean(x, axis=-1, keepdims=True)
    xc = x - mu
    var = jnp.mean(xc * xc, axis=-1, keepdims=True)
    return xc * lax.rsqrt(var + LN_EPS) * g + b


def _block_diag(x, n):
    r, c = x.shape
    t = jnp.concatenate([x] * n, axis=0)
    ri = lax.broadcasted_iota(jnp.int32, t.shape, 0) // r
    ci = lax.broadcasted_iota(jnp.int32, t.shape, 1) // (c // n)
    return jnp.where(ri == ci, t, 0.0)


def _cparams(sem):
    return pltpu.CompilerParams(dimension_semantics=sem, vmem_limit_bytes=VMEM_LIMIT)


def _proj_kernel(x_ref, g_ref, w_ref, o_nsa, o_gm, o_cv, o_rw):
    x = x_ref[...]
    ms = jnp.mean(x * x, axis=-1, keepdims=True)
    h = (x * lax.rsqrt(ms + NORM_EPS) * g_ref[...]).astype(BF16)
    off = 0
    for o in (o_nsa, o_gm, o_cv, o_rw):
        wd = o.shape[-1]
        o[...] = jnp.dot(h, w_ref[:, off:off + wd], preferred_element_type=F32)
        off += wd


def _in_proj(x2, g_pre, w_p, tm=256):
    n, d = x2.shape
    widths = (NSA_W, GM_W, CV_W, RW_W)
    return pl.pallas_call(
        _proj_kernel,
        grid=(n // tm,),
        in_specs=[pl.BlockSpec((tm, d), lambda i: (i, 0)),
                  pl.BlockSpec((1, d), lambda i: (0, 0)),
                  pl.BlockSpec((d, N_PROJ), lambda i: (0, 0))],
        out_specs=[pl.BlockSpec((tm, w), lambda i: (i, 0)) for w in widths],
        out_shape=[jax.ShapeDtypeStruct((n, w), F32) for w in widths],
        compiler_params=_cparams(("parallel",)),
        name="in_proj",
    )(x2, g_pre.reshape(1, d), w_p)


def _cmp_kernel(ch_ref, pos_ref, w1_ref, w2_ref, o_ref, ot_ref):
    ch = ch_ref[...]
    w1 = w1_ref[...]
    half = ch.shape[1]
    nc = ch.shape[0]
    a = _dot3(ch, w1[:half])
    b = _dot3(ch, w1[half:])
    b_next = pltpu.roll(b, nc - 1, axis=0)
    posc = _dot3(jnp.broadcast_to(pos_ref[...], (SUBLANES, 2 * half)), w1)[0:1]
    hid = a + b_next + posc
    out = _dot3(_silu(hid), w2_ref[...])
    o_ref[...] = out
    wide = jnp.concatenate([out, jnp.zeros_like(out)], axis=1)
    ot_ref[...] = wide.T[:out.shape[1]]


def _nsa_compress(chf, pos, w1, w2):
    _, b, nc, wdt = chf.shape
    dk = NSA_DK
    return pl.pallas_call(
        _cmp_kernel,
        grid=(2, b),
        in_specs=[pl.BlockSpec((None, None, nc, wdt), lambda s, i: (s, i, 0, 0)),
                  pl.BlockSpec((None, 1, 2 * wdt), lambda s, i: (s, 0, 0)),
                  pl.BlockSpec((None, 2 * wdt, CMP_HIDDEN), lambda s, i: (s, 0, 0)),
                  pl.BlockSpec((None, CMP_HIDDEN, dk), lambda s, i: (s, 0, 0))],
        out_specs=[pl.BlockSpec((None, None, nc, dk), lambda s, i: (s, i, 0, 0)),
                   pl.BlockSpec((None, None, dk, nc), lambda s, i: (s, i, 0, 0))],
        out_shape=[jax.ShapeDtypeStruct((2, b, nc, dk), F32),
                   jax.ShapeDtypeStruct((2, b, dk, nc), F32)],
        compiler_params=_cparams(("parallel", "parallel")),
        name="nsa_compress",
    )(chf, pos.reshape(2, 1, 2 * wdt), w1, w2)


def _nsa_kernel(q_ref, za_ref, g_ref, ks_ref, kw_ref, kc_ref, vct_ref, pt_ref, bt_ref, imat_ref,
                o_ref, ksb, vst, kwb, vwt, madd, m_s, acc_s, *, n_top, n_win):
    c = pl.program_id(1)
    n_qt = pl.num_programs(1)
    n_kt = ksb.shape[0]
    hd = HEAD_DIM
    nh = N_HEADS
    hq = nh * Q_TILE

    n_blk = imat_ref.shape[0]
    per_q = Q_TILE // SEL_BLOCK
    v_rows = vst.shape[1]

    @pl.when(c == 0)
    def _prep():
        key_blk = lax.broadcasted_iota(jnp.int32, (Q_TILE, n_blk), 0) // SEL_BLOCK
        col = lax.broadcasted_iota(jnp.int32, (Q_TILE, n_blk), 1)
        ones_row = jnp.where(lax.broadcasted_iota(jnp.int32, (v_rows - hd, Q_TILE), 0) == 0, 1.0, 0.0)

        def body(i, carry):
            r0 = pl.multiple_of(i * Q_TILE, Q_TILE)
            t = ks_ref[pl.ds(r0, Q_TILE), :]
            onehot = jnp.where(col == key_blk + per_q * i, 1.0, 0.0)
            ksb[i] = jnp.concatenate([t[:, :hd], onehot], axis=1).astype(BF16)
            vst[i] = jnp.concatenate([t.T[hd:, :], ones_row], axis=0).astype(BF16)
            t = kw_ref[pl.ds(r0, Q_TILE), :]
            kwb[i] = t[:, :hd].astype(BF16)
            vwt[i] = jnp.concatenate([t.T[hd:, :], ones_row], axis=0).astype(BF16)
            return carry
        lax.fori_loop(0, n_kt, body, 0)

    q = q_ref[...] * (hd ** -0.5 * LOG2E)
    q_all = jnp.concatenate([q[:, hd * h:hd * (h + 1)] for h in range(nh)], axis=0)
    qb = q_all.astype(BF16)
    g_t = _sigmoid(g_ref[...]).T
    gate = [jnp.concatenate([g_t[3 * h + br:3 * h + br + 1, :] for h in range(nh)], axis=1) for br in range(3)]

    kc = kc_ref[...]
    n_c = kc.shape[0]
    off = pl.multiple_of((n_qt - 1 - c) * (Q_TILE // CMP_STRIDE), SUBLANES)
    bias = pt_ref[pl.ds(off, n_c), :]
    s = _dot3_nt(kc, q_all) + bias
    valid = bias > 0.5 * NEG
    m = jnp.max(s, axis=0, keepdims=True)
    p = jnp.where(valid, jnp.exp2(s - m), 0.0)
    l = jnp.sum(p, axis=0, keepdims=True)
    pn = p * jnp.where(l > 0.0, 1.0 / l, 0.0)
    y_acc = gate[0] * _bdot(vct_ref[...], pn)
    pc = pn[:, 0:Q_TILE]
    for h in range(1, nh):
        pc = pc + pn[:, h * Q_TILE:(h + 1) * Q_TILE]

    tl = lax.broadcasted_iota(jnp.int32, (n_blk, Q_TILE), 1)
    jb = lax.broadcasted_iota(jnp.int32, (n_blk, Q_TILE), 0)
    cur = c * per_q + tl // SEL_BLOCK
    causal = jb <= cur
    need_rank = (c + 1) * per_q > n_top

    @pl.when(jnp.logical_not(need_rank))
    def _all_causal():
        madd[...] = jnp.where(causal, 0.0, NEG)

    @pl.when(need_rank)
    def _rank():
        imp = _dot_sel_lhs(imat_ref[...], pc)
        forced = (jb == 0) | (jb == cur) | (jb == cur - 1)
        val = jnp.where(causal, jnp.where(forced, FORCE, imp), NEG)
        n_r = n_blk // SUBLANES
        blocks = [val[SUBLANES * r:SUBLANES * (r + 1), :] for r in range(n_r)]
        cnts = [jnp.zeros((SUBLANES, Q_TILE), F32) for _ in range(n_r)]
        jrow = lax.broadcasted_iota(jnp.int32, (SUBLANES, Q_TILE), 0)
        for i in range(n_blk):
            vi = jnp.broadcast_to(val[i:i + 1, :], (SUBLANES, Q_TILE))
            for r in range(n_r):
                if SUBLANES * r > i:
                    beats = vi >= blocks[r]
                elif SUBLANES * r + SUBLANES - 1 < i:
                    beats = vi > blocks[r]
                else:
                    ge = jnp.where(vi >= blocks[r], 1.0, 0.0)
                    gt = jnp.where(vi > blocks[r], 1.0, 0.0)
                    cnts[r] = cnts[r] + jnp.where(jrow + SUBLANES * r > i, ge, gt)
                    continue
                cnts[r] = cnts[r] + jnp.where(beats, 1.0, 0.0)
        cnt = jnp.concatenate(cnts, axis=0)
        madd[...] = jnp.where(causal & (cnt < float(n_top)), 0.0, NEG)

    def reset_state():
        m_s[...] = jnp.full(m_s.shape, NEG, F32)
        acc_s[...] = jnp.zeros(acc_s.shape, F32)

    def update(ss, vts):
        m_old = m_s[...]
        m_new = m_old
        for s_i in ss:
            m_new = jnp.maximum(m_new, jnp.max(s_i, axis=0, keepdims=True))
        pv = None
        for s_i, v_i in zip(ss, vts):
            t = jnp.dot(v_i, jnp.exp2(s_i - m_new).astype(BF16), preferred_element_type=F32)
            pv = t if pv is None else pv + t
        acc_s[...] = jnp.exp2(m_old - m_new) * acc_s[...] + pv
        m_s[...] = m_new

    def branch_out(g):
        acc = acc_s[...]
        return (g / acc[hd:hd + 1, :]) * acc[:hd, :]

    pad = jnp.zeros((LANES - n_blk, Q_TILE), F32)
    m_t = jnp.concatenate([madd[...], pad], axis=0).T[:, :n_blk]
    blk_lane = lax.broadcasted_iota(jnp.int32, m_t.shape, 1)
    m_far = jnp.where(blk_lane < per_q * (c - 1), m_t, NEG)
    q_near = jnp.concatenate([q_all, jnp.concatenate([m_t] * nh, axis=0)], axis=1).astype(BF16)
    q_far = jnp.concatenate([q_all, jnp.concatenate([m_far] * nh, axis=0)], axis=1).astype(BF16)

    reset_state()
    kt1 = jnp.maximum(c - 1, 0)
    s0 = _nt(ksb[c], q_near) + bt_ref[0]
    s1 = _nt(ksb[kt1], q_near) + bt_ref[1] + jnp.where(c >= 1, 0.0, NEG)
    update([s0, s1], [vst[c], vst[kt1]])

    n_far = jnp.maximum(c - 1, 0)

    def far_body(g, carry):
        kts = [g * SEL_GROUP + j for j in range(SEL_GROUP)]
        update([_nt(ksb[kt], q_far) for kt in kts], [vst[kt] for kt in kts])
        return carry
    lax.fori_loop(0, (n_far + SEL_GROUP - 1) // SEL_GROUP, far_body, 0)
    y_acc = y_acc + branch_out(gate[1])

    reset_state()
    ss, vts = [], []
    for i in range(n_win + 1):
        kt = jnp.maximum(c - i, 0)
        s_i = _nt(kwb[kt], qb)
        if i == 0:
            s_i = s_i + bt_ref[0]
        else:
            if i == 1:
                s_i = s_i + bt_ref[1]
            elif i == n_win:
                s_i = s_i + bt_ref[2]
            s_i = s_i + jnp.where(c >= i, 0.0, NEG)
        ss.append(s_i)
        vts.append(vwt[kt])
    update(ss, vts)
    y_acc = y_acc + branch_out(gate[2])

    y_t = jnp.concatenate([y_acc[:, h * Q_TILE:(h + 1) * Q_TILE] for h in range(nh)], axis=0)
    o_ref[...] = y_t.T * _silu(za_ref[...])


def _nsa_kernel_multi(q_ref, za_ref, g_ref, ks_ref, kw_ref, kc_ref, vct_ref, pt_ref, bt_ref, imat_ref,
                      o_ref, ksb, vst, kwb, vwt, madd, m_s, acc_s, *, n_top, n_win):
    c = pl.program_id(1)
    n_qt = pl.num_programs(1)
    n_st = q_ref.shape[0]
    n_kt = ksb.shape[1]
    hd = HEAD_DIM
    nh = N_HEADS
    n_blk = imat_ref.shape[0]
    per_q = Q_TILE // SEL_BLOCK
    v_rows = vst.shape[2]
    streams = range(n_st)

    @pl.when(c == 0)
    def _prep():
        key_blk = lax.broadcasted_iota(jnp.int32, (Q_TILE, n_blk), 0) // SEL_BLOCK
        col = lax.broadcasted_iota(jnp.int32, (Q_TILE, n_blk), 1)
        ones_row = jnp.where(lax.broadcasted_iota(jnp.int32, (v_rows - hd, Q_TILE), 0) == 0, 1.0, 0.0)

        def body(i, carry):
            r0 = pl.multiple_of(i * Q_TILE, Q_TILE)
            onehot = jnp.where(col == key_blk + per_q * i, 1.0, 0.0)
            for s in streams:
                t = ks_ref[s, pl.ds(r0, Q_TILE), :]
                ksb[s, i] = jnp.concatenate([t[:, :hd], onehot], axis=1).astype(BF16)
                vst[s, i] = jnp.concatenate([t.T[hd:, :], ones_row], axis=0).astype(BF16)
                t = kw_ref[s, pl.ds(r0, Q_TILE), :]
                kwb[s, i] = t[:, :hd].astype(BF16)
                vwt[s, i] = jnp.concatenate([t.T[hd:, :], ones_row], axis=0).astype(BF16)
            return carry
        lax.fori_loop(0, n_kt, body, 0)

    def reset_state():
        m_s[...] = jnp.full(m_s.shape, NEG, F32)
        acc_s[...] = jnp.zeros(acc_s.shape, F32)

    def update(groups):
        m_old = [m_s[s] for s in streams]
        m_new = []
        for s in streams:
            m = m_old[s]
            for s_i in groups[s][0]:
                m = jnp.maximum(m, jnp.max(s_i, axis=0, keepdims=True))
            m_new.append(m)
        for s in streams:
            pv = None
            for s_i, v_i in zip(*groups[s]):
                t = jnp.dot(v_i, jnp.exp2(s_i - m_new[s]).astype(BF16), preferred_element_type=F32)
                pv = t if pv is None else pv + t
            acc_s[s] = jnp.exp2(m_old[s] - m_new[s]) * acc_s[s] + pv
            m_s[s] = m_new[s]

    def branch_out(s, g):
        acc = acc_s[s]
        return (g / acc[hd:hd + 1, :]) * acc[:hd, :]

    q_all, qb, gate = [], [], []
    for s in streams:
        q = q_ref[s] * (hd ** -0.5 * LOG2E)
        qa = jnp.concatenate([q[:, hd * h:hd * (h + 1)] for h in range(nh)], axis=0)
        q_all.append(qa)
        qb.append(qa.astype(BF16))
        g_t = _sigmoid(g_ref[s]).T
        gate.append([jnp.concatenate([g_t[3 * h + br:3 * h + br + 1, :] for h in range(nh)], axis=1)
                     for br in range(3)])

    reset_state()
    groups = []
    for s in streams:
        ss, vts = [], []
        for i in range(n_win + 1):
            kt = jnp.maximum(c - i, 0)
            s_i = _nt(kwb[s, kt], qb[s])
            if i == 0:
                s_i = s_i + bt_ref[0]
            else:
                if i == 1:
                    s_i = s_i + bt_ref[1]
                elif i == n_win:
                    s_i = s_i + bt_ref[2]
                s_i = s_i + jnp.where(c >= i, 0.0, NEG)
            ss.append(s_i)
            vts.append(vwt[s, kt])
        groups.append((ss, vts))
    update(groups)
    y_acc = [branch_out(s, gate[s][2]) for s in streams]

    n_c = kc_ref.shape[1]
    off = pl.multiple_of((n_qt - 1 - c) * (Q_TILE // CMP_STRIDE), SUBLANES)
    bias = pt_ref[pl.ds(off, n_c), :]
    valid = bias > 0.5 * NEG
    pcs = []
    for s in streams:
        sc = _dot3_nt(kc_ref[s], q_all[s]) + bias
        m = jnp.max(sc, axis=0, keepdims=True)
        p = jnp.where(valid, jnp.exp2(sc - m), 0.0)
        l = jnp.sum(p, axis=0, keepdims=True)
        pn = p * jnp.where(l > 0.0, 1.0 / l, 0.0)
        y_acc[s] = y_acc[s] + gate[s][0] * _bdot(vct_ref[s], pn)
        pc = pn[:, 0:Q_TILE]
        for h in range(1, nh):
            pc = pc + pn[:, h * Q_TILE:(h + 1) * Q_TILE]
        pcs.append(pc)

    tl = lax.broadcasted_iota(jnp.int32, (n_blk, Q_TILE), 1)
    jb = lax.broadcasted_iota(jnp.int32, (n_blk, Q_TILE), 0)
    cur = c * per_q + tl // SEL_BLOCK
    causal = jb <= cur
    need_rank = (c + 1) * per_q > n_top

    @pl.when(jnp.logical_not(need_rank))
    def _all_causal():
        for s in streams:
            madd[s] = jnp.where(causal, 0.0, NEG)

    @pl.when(need_rank)
    def _rank():
        forced = (jb == 0) | (jb == cur) | (jb == cur - 1)
        n_r = n_blk // SUBLANES
        jrow = lax.broadcasted_iota(jnp.int32, (SUBLANES, Q_TILE), 0)
        for s in streams:
            imp = _dot_sel_lhs(imat_ref[...], pcs[s])
            val = jnp.where(causal, jnp.where(forced, FORCE, imp), NEG)
            blocks = [val[SUBLANES * r:SUBLANES * (r + 1), :] for r in range(n_r)]
            cnts = [jnp.zeros((SUBLANES, Q_TILE), F32) for _ in range(n_r)]
            for i in range(n_blk):
                vi = jnp.broadcast_to(val[i:i + 1, :], (SUBLANES, Q_TILE))
                for r in range(n_r):
                    if SUBLANES * r > i:
                        cnts[r] = cnts[r] + jnp.where(vi >= blocks[r], 1.0, 0.0)
                    elif SUBLANES * r + SUBLANES - 1 < i:
                        cnts[r] = cnts[r] + jnp.where(vi > blocks[r], 1.0, 0.0)
                    else:
                        ge = jnp.where(vi >= blocks[r], 1.0, 0.0)
                        gt = jnp.where(vi > blocks[r], 1.0, 0.0)
                        cnts[r] = cnts[r] + jnp.where(jrow + SUBLANES * r > i, ge, gt)
            cnt = jnp.concatenate(cnts, axis=0)
            madd[s] = jnp.where(causal & (cnt < float(n_top)), 0.0, NEG)

    reset_state()
    pad = jnp.zeros((LANES - n_blk, Q_TILE), F32)
    kt1 = jnp.maximum(c - 1, 0)
    q_far, groups = [], []
    for s in streams:
        m_t = jnp.concatenate([madd[s], pad], axis=0).T[:, :n_blk]
        blk_lane = lax.broadcasted_iota(jnp.int32, m_t.shape, 1)
        m_far = jnp.where(blk_lane < per_q * (c - 1), m_t, NEG)
        q_near = jnp.concatenate([q_all[s], jnp.concatenate([m_t] * nh, axis=0)], axis=1).astype(BF16)
        q_far.append(jnp.concatenate([q_all[s], jnp.concatenate([m_far] * nh, axis=0)], axis=1).astype(BF16))
        s0 = _nt(ksb[s, c], q_near) + bt_ref[0]
        s1 = _nt(ksb[s, kt1], q_near) + bt_ref[1] + jnp.where(c >= 1, 0.0, NEG)
        groups.append(([s0, s1], [vst[s, c], vst[s, kt1]]))
    update(groups)

    n_far = jnp.maximum(c - 1, 0)

    def far_body(g, carry):
        kts = [g * SEL_GROUP + j for j in range(SEL_GROUP)]
        update([([_nt(ksb[s, kt], q_far[s]) for kt in kts], [vst[s, kt] for kt in kts]) for s in streams])
        return carry
    lax.fori_loop(0, (n_far + SEL_GROUP - 1) // SEL_GROUP, far_body, 0)

    for s in streams:
        y = y_acc[s] + branch_out(s, gate[s][1])
        y_t = jnp.concatenate([y[:, h * Q_TILE:(h + 1) * Q_TILE] for h in range(nh)], axis=0)
        o_ref[s] = y_t.T * _silu(za_ref[s])


def _bucket_lookup(relc, idx, visible):
    r, cc = idx.shape
    onehot = (jnp.asarray(idx.reshape(1, -1)) == jnp.arange(relc.shape[0])[:, None]).astype(F32)
    vals = jnp.einsum("bh,bn->hn", relc, onehot, precision=lax.Precision.HIGHEST).reshape(-1, r, cc)
    vals = jnp.where(jnp.asarray(visible)[None], vals, NEG)
    return jnp.transpose(vals, (1, 0, 2)).reshape(r, -1).astype(F32)


def _nsa_tables(rel_bias, s_len):
    n_qt = s_len // Q_TILE
    n_c = s_len // CMP_STRIDE
    relc = rel_bias - rel_bias[REL_BUCKETS - 1][None, :]
    bk = _rel_buckets(s_len + Q_TILE)
    kl = np.arange(Q_TILE)[:, None]
    tq = np.arange(Q_TILE)[None, :]
    d0 = tq - kl
    diag = _bucket_lookup(relc, bk[np.clip(d0, 0, None)], d0 >= 0)
    prev = _bucket_lookup(relc, bk[Q_TILE + d0], np.ones_like(d0, bool))
    edge = jnp.asarray(np.tile(np.where(kl > tq, 0.0, NEG).astype(np.float32), (1, N_HEADS)))
    bt = jnp.stack([diag, prev, edge], axis=0) * LOG2E
    per_q = Q_TILE // CMP_STRIDE
    n_rows = n_c + per_q * (n_qt - 1)
    r = np.arange(n_rows)[:, None]
    dc = tq - CMP_STRIDE * (r - per_q * (n_qt - 1)) - (CMP_BLOCK - 1)
    pt = _bucket_lookup(relc, bk[np.clip(dc, 0, None)], dc >= 0) * LOG2E
    n_blk = s_len // SEL_BLOCK
    ratio = SEL_BLOCK // CMP_STRIDE
    jj = np.arange(n_blk)[:, None]
    ii = np.arange(n_c)[None, :]
    imat = ((ii >= ratio * jj - 1) & (ii <= ratio * jj + ratio - 1) & (ii < n_c - 1)).astype(np.float32)
    return bt, pt, jnp.asarray(imat)


def _nsa_attention(nsa3, kc, vct, bt, pt, imat):
    b, s_len, _ = nsa3.shape
    n_qt = s_len // Q_TILE
    n_c = s_len // CMP_STRIDE
    n_blk = s_len // SEL_BLOCK
    n_win = WINDOW // Q_TILE
    hd = HEAD_DIM
    hq = N_HEADS * Q_TILE
    v_rows = hd + 2 * SUBLANES
    n_st = NSA_STREAMS if b % NSA_STREAMS == 0 else 1
    kern = functools.partial(_nsa_kernel_multi, n_top=min(N_SEL, n_blk), n_win=n_win)
    return pl.pallas_call(
        kern,
        grid=(b // n_st, n_qt),
        in_specs=[pl.BlockSpec((n_st, Q_TILE, 256), lambda i, c: (i, c, 0)),
                  pl.BlockSpec((n_st, Q_TILE, 256), lambda i, c: (i, c, 1)),
                  pl.BlockSpec((n_st, Q_TILE, LANES), lambda i, c: (i, c, 7)),
                  pl.BlockSpec((n_st, s_len, LANES), lambda i, c: (i, 0, 4)),
                  pl.BlockSpec((n_st, s_len, LANES), lambda i, c: (i, 0, 5)),
                  pl.BlockSpec((n_st, n_c, hd), lambda i, c: (i, 0, 0)),
                  pl.BlockSpec((n_st, hd, n_c), lambda i, c: (i, 0, 0)),
                  pl.BlockSpec(pt.shape, lambda i, c: (0, 0)),
                  pl.BlockSpec(bt.shape, lambda i, c: (0, 0, 0)),
                  pl.BlockSpec(imat.shape, lambda i, c: (0, 0))],
        out_specs=pl.BlockSpec((n_st, Q_TILE, 256), lambda i, c: (i, c, 0)),
        out_shape=jax.ShapeDtypeStruct((b, s_len, 256), F32),
        scratch_shapes=[pltpu.VMEM((n_st, n_qt, Q_TILE, hd + n_blk), BF16),
                        pltpu.VMEM((n_st, n_qt, v_rows, Q_TILE), BF16),
                        pltpu.VMEM((n_st, n_qt, Q_TILE, hd), BF16),
                        pltpu.VMEM((n_st, n_qt, v_rows, Q_TILE), BF16),
                        pltpu.VMEM((n_st, n_blk, Q_TILE), F32),
                        pltpu.VMEM((n_st, 1, hq), F32),
                        pltpu.VMEM((n_st, v_rows, hq), F32)],
        compiler_params=_cparams(("arbitrary", "arbitrary")),
        name="nsa_attention",
    )(nsa3, nsa3, nsa3, nsa3, nsa3, kc, vct, pt, bt, imat)


def _gc_kernel(gm_ref, cv_ref, halo_ref, wall_ref, sb_ref, lng_ref, lnb_ref,
               cw_ref, cb_ref, clg_ref, clb_ref, cpw_ref, cpb_ref, ob_ref, oc_ref, xs, *, n_ct):
    c = pl.program_id(0) % n_ct
    t_len = gm_ref.shape[0]
    w = W_GRP
    gm = gm_ref[...]
    u = _gelu_tanh(gm[:, :w])
    v = _layer_norm(_gelu_tanh(gm[:, w:2 * w]), lng_ref[...], lnb_ref[...])
    wall = wall_ref[...]
    ti = lax.broadcasted_iota(jnp.int32, wall.shape, 0)
    si = lax.broadcasted_iota(jnp.int32, wall.shape, 1) % t_len
    wall = jnp.where(si <= ti, wall, 0.0)
    sv = _bdot(wall, _block_diag(v, N_HEADS)) + sb_ref[...]
    ob_ref[...] = u * sv * _silu(gm[:, 2 * w:])
    cv = cv_ref[...]
    hl = halo_ref[...]
    hx = hl[:, :w] * _sigmoid(hl[:, w:2 * w])
    hrows = hl.shape[0]
    xs[0:hrows, :] = jnp.where(c == 0, 0.0, hx)
    xs[hrows:hrows + t_len, :] = cv[:, :w] * _sigmoid(cv[:, w:2 * w])
    cw = cw_ref[...]
    acc = jnp.zeros((t_len, w), F32)
    base = hrows - (CONV_WIDTH - 1)
    for j in range(CONV_WIDTH):
        acc = acc + cw[j:j + 1, :] * xs[base + j:base + j + t_len, :]
    y = _layer_norm(acc + cb_ref[...], clg_ref[...], clb_ref[...])
    y = _bdot(_silu(y), cpw_ref[...]) + cpb_ref[...]
    oc_ref[...] = y * _silu(cv[:, 2 * w:])


def _gmlp_conv(gm, cv, s_len, wall, sb, lng, lnb, cw, cb, clg, clb, cpw, cpb, t_len=128, halo=32):
    n = gm.shape[0]
    n_ct = s_len // t_len
    w = W_GRP
    row = lambda a: a.reshape(1, w)
    full = lambda a: pl.BlockSpec(a.shape, lambda i: (0,) * a.ndim)
    cwp = jnp.concatenate([cw, jnp.zeros((32 - CONV_WIDTH, w), F32)], axis=0)
    args = (wall, sb, row(lng), row(lnb), cwp, row(cb), row(clg), row(clb), cpw.astype(BF16), row(cpb))
    per = t_len // halo
    return pl.pallas_call(
        functools.partial(_gc_kernel, n_ct=n_ct),
        grid=(n // t_len,),
        in_specs=[pl.BlockSpec((t_len, GM_W), lambda i: (i, 0)),
                  pl.BlockSpec((t_len, CV_W), lambda i: (i, 0)),
                  pl.BlockSpec((halo, CV_W), lambda i: (jnp.maximum(i * per - 1, 0), 0))]
                 + [full(a) for a in args],
        out_specs=[pl.BlockSpec((t_len, w), lambda i: (i, 0))] * 2,
        out_shape=[jax.ShapeDtypeStruct((n, w), F32)] * 2,
        scratch_shapes=[pltpu.VMEM((halo + t_len, w), F32)],
        compiler_params=_cparams(("parallel",)),
        name="gmlp_conv",
    )(gm, cv, cv, *args)


def _rwkv_kernel(rw_ref, prev_ref, mu_ref, wup_ref, aup_ref, vec_ref, o_ref, st):
    c = pl.program_id(0)
    n_b = rw_ref.shape[0]
    L = rw_ref.shape[1]
    w = W_GRP
    nh = N_HEADS
    hd = HEAD_DIM

    @pl.when(c == 0)
    def _init():
        st[...] = jnp.zeros(st.shape, F32)

    vec = vec_ref[...]
    w0, a0, k_k, k_a, r_k, gn_g, gn_b = [vec[i:i + 1, :] for i in range(7)]
    mu = mu_ref[...]
    rows = lax.broadcasted_iota(jnp.int32, (L, w), 0)
    lane = lax.broadcasted_iota(jnp.int32, (L, w), 1)
    s_of = lane % hd
    ones_bd = jnp.where((lax.broadcasted_iota(jnp.int32, (w, w), 0) // hd)
                        == (lax.broadcasted_iota(jnp.int32, (w, w), 1) // hd), 1.0, 0.0)
    tri = jnp.where(lax.broadcasted_iota(jnp.int32, (L, L), 1) <= lax.broadcasted_iota(jnp.int32, (L, L), 0), 1.0, 0.0)
    bd_mask = (lax.broadcasted_iota(jnp.int32, (w, w), 0) // hd) == (lax.broadcasted_iota(jnp.int32, (w, w), 1) // hd)
    strict = s_of < rows
    incl = s_of <= rows
    eye_all = jnp.where(s_of == rows, 1.0, 0.0)

    def bd(x):
        xb = x.astype(BF16)
        return jnp.where(bd_mask, jnp.concatenate([xb] * nh, axis=0), jnp.zeros((), BF16))

    def mm(a, b):
        return jnp.dot(a.astype(BF16), b.astype(BF16), preferred_element_type=F32)

    def mm_nt(a, b):
        return _nt(a.astype(BF16), b.astype(BF16))

    nb = range(n_b)
    stack = lambda parts: jnp.concatenate(parts, axis=0)
    part = lambda x, b: x[b * L:(b + 1) * L]
    zs = [rw_ref[b] for b in nb]
    zrow = lax.broadcasted_iota(jnp.int32, zs[0].shape, 0)
    n_prev = prev_ref.shape[1]
    xs = []
    for b in nb:
        last = jnp.where(c == 0, 0.0, prev_ref[b, n_prev - 1:n_prev, :])
        zprev = jnp.where(zrow == 0, last, pltpu.roll(zs[b], 1, axis=0))
        xs.append(zs[b] + mu * (zprev - zs[b]))
    xs = stack(xs)
    r = xs[:, 0:w]
    k = xs[:, w:2 * w]
    v = xs[:, 2 * w:3 * w]
    wa = xs[:, 3 * w:3 * w + LANES]
    zd = stack([z[:, 3 * w + LANES:] for z in zs])

    zz = w0 + _dot3(jnp.tanh(wa), wup_ref[...])
    lw = -jnp.exp(-_softplus(-zz) - 0.5)
    a = _sigmoid(a0 + _dot3(wa, aup_ref[...]))
    kkr = k * k_k
    kk = kkr / jnp.maximum(jnp.sqrt(mm(kkr * kkr, ones_bd)), 1e-12)
    k2 = k * (1.0 + (a - 1.0) * k_a)
    bb = kk * a

    lw_wide = jnp.concatenate([part(lw, b) for b in nb], axis=1)
    cs_wide = _dot_sel_lhs(tri, lw_wide)
    cs = stack([cs_wide[:, b * w:(b + 1) * w] for b in nb])
    g_t = jnp.exp(cs)
    g_inv = jnp.exp(-cs)
    kq = (kk * jnp.exp(cs - lw)).astype(BF16)
    rq = (r * g_t).astype(BF16)
    kt = k2 * g_inv
    bt = bb * g_inv

    lhs = [stack([part(kq, b), part(rq, b)]) for b in nb]
    ab_b = [_nt(lhs[b], bd(part(bt, b))) for b in nb]
    ab_k = [_nt(lhs[b], bd(part(kt, b))) for b in nb]
    a_b = [jnp.where(strict, x[:L], 0.0) for x in ab_b]
    b_b = [jnp.where(incl, x[L:], 0.0) for x in ab_b]
    ak_bk = [stack([jnp.where(strict, x[:L], 0.0), jnp.where(incl, x[L:], 0.0)]) for x in ab_k]

    npow = a_b
    tinv = [eye_all - x for x in a_b]
    for i in range(int(math.log2(L)) - 1):
        npow = [mm(x, bd(x)) for x in npow]
        tinv = [t + mm(t, bd(x)) for t, x in zip(tinv, npow)]

    s0 = [st[b] for b in nb]
    kh = [_nt(lhs[b], s0[b].astype(BF16)) for b in nb]
    akv = [mm(ak_bk[b], bd(part(v, b))) for b in nb]
    u = [mm(tinv[b], bd(kh[b][:L] + akv[b][:L])) for b in nb]
    y = [kh[b][L:] + akv[b][L:] - mm(b_b[b], bd(u[b])) for b in nb]

    zpad = jnp.zeros((w - 2 * L, w), F32)
    for b in nb:
        vu_t = stack([part(v, b), u[b], zpad]).T
        kb = stack([part(kt, b), -part(bt, b), zpad])
        d = mm(vu_t, kb)
        st[b] = g_t[(b + 1) * L - 1:(b + 1) * L, :] * (s0[b] + jnp.where(bd_mask, d, 0.0))

    y = stack(y)
    y_hi, y_lo = _split2(y)
    mean = (mm(y_hi, ones_bd) + mm(y_lo, ones_bd)) * (1.0 / hd)
    yc = y - mean
    var = mm(yc * yc, ones_bd) * (1.0 / hd)
    yn = yc * lax.rsqrt(var + RWKV_GN_EPS) * gn_g + gn_b
    bonus = mm(r * k2 * r_k, ones_bd) * v
    out = (yn + bonus) * _silu(zd)
    for b in nb:
        o_ref[b] = part(out, b)


def _rwkv(rw3, mu_p, wup_p, aup_p, vec):
    b, s_len, _ = rw3.shape
    L = RW_CHUNK
    w = W_GRP
    full = lambda a: pl.BlockSpec(a.shape, lambda c: (0,) * a.ndim)
    return pl.pallas_call(
        _rwkv_kernel,
        grid=(s_len // L,),
        in_specs=[pl.BlockSpec((b, L, RW_W), lambda c: (0, c, 0)),
                  pl.BlockSpec((b, SUBLANES, RW_W), lambda c: (0, jnp.maximum(c * (L // SUBLANES) - 1, 0), 0)),
                  full(mu_p), full(wup_p), full(aup_p), full(vec)],
        out_specs=pl.BlockSpec((b, L, w), lambda c: (0, c, 0)),
        out_shape=jax.ShapeDtypeStruct((b, s_len, w), F32),
        scratch_shapes=[pltpu.VMEM((b, w, w), F32)],
        compiler_params=_cparams(("arbitrary",)),
        name="rwkv7",
    )(rw3, rw3, mu_p, wup_p, aup_p, vec)


def _out_kernel(ya_ref, yb_ref, yc_ref, yd_ref, x_ref, p_ref, wo_ref, gp_ref, pp_ref, pg_ref, o_ref):
    w = W_GRP
    acc = None
    for i, y in enumerate((ya_ref, yb_ref, yc_ref, yd_ref)):
        t = jnp.dot(y[...].astype(BF16), wo_ref[i * w:(i + 1) * w, :], preferred_element_type=F32)
        acc = t if acc is None else acc + t
    ms = jnp.mean(acc * acc, axis=-1, keepdims=True)
    x1 = x_ref[...] + acc * lax.rsqrt(ms + NORM_EPS) * gp_ref[...]
    gate = _sigmoid(jnp.dot(x1.astype(BF16), pg_ref[...], preferred_element_type=F32))
    pe = jnp.dot(p_ref[...].astype(BF16), pp_ref[...], preferred_element_type=F32)
    o_ref[...] = x1 + gate * pe


def _out_proj(ys, x2, p2, w_out, g_post, ple_proj, ple_gate, tm=256):
    n, d = x2.shape
    w = W_GRP
    full = lambda a: pl.BlockSpec(a.shape, lambda i: (0,) * a.ndim)
    wo = w_out.astype(BF16)
    gp = g_post.reshape(1, d)
    pp = ple_proj.astype(BF16)
    pg = ple_gate.astype(BF16)
    return pl.pallas_call(
        _out_kernel,
        grid=(n // tm,),
        in_specs=[pl.BlockSpec((tm, w), lambda i: (i, 0))] * 4
                 + [pl.BlockSpec((tm, d), lambda i: (i, 0)),
                    pl.BlockSpec((tm, p2.shape[1]), lambda i: (i, 0)),
                    full(wo), full(gp), full(pp), full(pg)],
        out_specs=pl.BlockSpec((tm, d), lambda i: (i, 0)),
        out_shape=jax.ShapeDtypeStruct((n, d), F32),
        compiler_params=_cparams(("parallel",)),
        name="out_proj",
    )(*ys, x2, p2, wo, gp, pp, pg)


def _layer(x2, p2, b, s_len, bt, pt, imat, w_in, w_out, g_pre, g_post, nsa_pos, nsa_w1, nsa_w2,
           sgu_ln_g, sgu_ln_b, sgu_w, sgu_b, conv_w, conv_b, conv_ln_g, conv_ln_b, conv_pw, conv_pw_b,
           rwkv_mu, rwkv_w0, rwkv_w_up, rwkv_a0, rwkv_a_up, rwkv_k_k, rwkv_k_a, rwkv_r_k,
           rwkv_gn_g, rwkv_gn_b, ple_proj, ple_gate):
    n = x2.shape[0]
    w = W_GRP
    w_bf = w_in.astype(BF16)
    w_p = jnp.concatenate([jnp.zeros((w_in.shape[0], b_), BF16) if a_ is None else w_bf[:, a_:b_]
                           for a_, b_ in _SEGMENTS], axis=1)
    nsa, gm, cv, rw = _in_proj(x2, g_pre, w_p)

    n_c = s_len // CMP_STRIDE
    nsa3 = nsa.reshape(b, s_len, NSA_W)
    kvc = nsa3[:, :, 768:896].reshape(b, n_c, CMP_STRIDE, 2, NSA_DK)
    chf = jnp.transpose(kvc, (3, 0, 1, 2, 4)).reshape(2, b, n_c, CMP_STRIDE * NSA_DK)
    cmp_o, cmp_t = _nsa_compress(chf, nsa_pos, nsa_w1, nsa_w2)
    ya = _nsa_attention(nsa3, cmp_o[0], cmp_t[1], bt, pt, imat).reshape(n, w)

    t_len = sgu_w.shape[-1]
    wall = jnp.transpose(sgu_w, (1, 0, 2)).reshape(t_len, N_HEADS * t_len)
    sb = jnp.repeat(sgu_b.T, HEAD_DIM, axis=1)
    yb, yc = _gmlp_conv(gm, cv, s_len, wall, sb, sgu_ln_g, sgu_ln_b, conv_w, conv_b,
                        conv_ln_g, conv_ln_b, conv_pw, conv_pw_b, t_len=t_len)

    lora = RWKV_LORA
    mu_p = jnp.concatenate([rwkv_mu, jnp.zeros((RW_W - rwkv_mu.shape[0],), F32)]).reshape(1, RW_W)
    wup_p = jnp.concatenate([rwkv_w_up, jnp.zeros((LANES - lora, w), F32)], axis=0)
    aup_p = jnp.concatenate([jnp.zeros((lora, w), F32), rwkv_a_up, jnp.zeros((LANES - 2 * lora, w), F32)], axis=0)
    vec = jnp.stack([rwkv_w0, rwkv_a0, rwkv_k_k, rwkv_k_a, rwkv_r_k.reshape(w), rwkv_gn_g, rwkv_gn_b,
                     jnp.zeros((w,), F32)], axis=0)
    yd = _rwkv(rw.reshape(b, s_len, RW_W), mu_p, wup_p, aup_p, vec).reshape(n, w)

    return _out_proj((ya, yb, yc, yd), x2, p2, w_out, g_post, ple_proj, ple_gate)


def kernel(x, p, rel_bias, w_in, w_out, g_pre, g_post, nsa_pos, nsa_w1, nsa_w2, sgu_ln_g, sgu_ln_b, sgu_w, sgu_b, conv_w, conv_b, conv_ln_g, conv_ln_b, conv_pw, conv_pw_b, rwkv_mu, rwkv_w0, rwkv_w_up, rwkv_a0, rwkv_a_up, rwkv_k_k, rwkv_k_a, rwkv_r_k, rwkv_gn_g, rwkv_gn_b, ple_proj, ple_gate):
    b, s_len, d = x.shape
    depth = w_in.shape[0]
    bt, pt, imat = _nsa_tables(rel_bias, s_len)
    x2 = x.reshape(b * s_len, d)
    per_layer = (w_in, w_out, g_pre, g_post, nsa_pos, nsa_w1, nsa_w2, sgu_ln_g, sgu_ln_b, sgu_w, sgu_b,
                 conv_w, conv_b, conv_ln_g, conv_ln_b, conv_pw, conv_pw_b, rwkv_mu, rwkv_w0, rwkv_w_up,
                 rwkv_a0, rwkv_a_up, rwkv_k_k, rwkv_k_a, rwkv_r_k, rwkv_gn_g, rwkv_gn_b, ple_proj, ple_gate)
    for i in range(depth):
        x2 = _layer(x2, p[i].reshape(b * s_len, -1), b, s_len, bt, pt, imat, *[a[i] for a in per_layer])
    return x2.reshape(b, s_len, d)
```

```python
import functools
import math

import numpy as np
import jax
import jax.numpy as jnp
from jax import lax
from jax.experimental import pallas as pl
from jax.experimental.pallas import tpu as pltpu

F32 = jnp.float32
BF16 = jnp.bfloat16

W_GRP = 256
HEAD_DIM = 64
N_HEADS = 4
NSA_DK = 64
CMP_STRIDE = 16
CMP_BLOCK = 32
CMP_HIDDEN = 128
SEL_BLOCK = 64
N_SEL = 16
WINDOW = 512
REL_BUCKETS = 32
REL_MAX_EXACT = 16
REL_MAX_DIST = 128
CONV_WIDTH = 31
RWKV_LORA = 32
RWKV_GN_EPS = 64e-5
NORM_EPS = 1e-6
LN_EPS = 1e-5
NEG = -1e30
FORCE = 1e4
LOG2E = math.log2(math.e)

LANES = 128
SUBLANES = 8
Q_TILE = 128
SEL_GROUP = 4
NSA_STREAMS = 4
RW_CHUNK = 64
VMEM_LIMIT = 48 * 1024 * 1024

NSA_W, GM_W, CV_W, RW_W = 1024, 768, 768, 1152
N_PROJ = NSA_W + GM_W + CV_W + RW_W


def _proj_segments():
    names = ["q", "kc", "vc", "ks", "vs", "kw", "vw", "g", "za", "u", "v", "zb", "ga", "gb", "zc", "rw", "zd"]
    widths = [256, 64, 64, 64, 64, 64, 64, 12, 256, 256, 256, 256, 256, 256, 256, 832, 256]
    o, off = {}, 0
    for n, w in zip(names, widths):
        o[n] = (off, off + w)
        off += w
    return [o["q"], o["za"], (o["ks"][0], o["vw"][1]), (o["kc"][0], o["vc"][1]), o["g"], (None, 116),
            (o["u"][0], o["zc"][1]), o["rw"], (None, 64), o["zd"]]


_SEGMENTS = _proj_segments()


def _rel_buckets(n):
    d = np.arange(n)
    nf = np.maximum(d, REL_MAX_EXACT).astype(np.float32)
    large = REL_MAX_EXACT + (np.log(nf / np.float32(REL_MAX_EXACT)) / np.float32(math.log(REL_MAX_DIST / REL_MAX_EXACT))
                             * np.float32(REL_BUCKETS - REL_MAX_EXACT)).astype(np.int32)
    large = np.minimum(large, REL_BUCKETS - 1)
    return np.where(d < REL_MAX_EXACT, d, large)


def _bdot(a, b):
    return jnp.dot(a.astype(BF16), b.astype(BF16), preferred_element_type=F32)


def _nt(a, b):
    return lax.dot_general(a, b, (((1,), (1,)), ((), ())), preferred_element_type=F32)


def _split2(a):
    hi = a.astype(BF16)
    lo = (a - hi.astype(F32)).astype(BF16)
    return hi, lo


def _split3(a):
    hi = a.astype(BF16)
    r = a - hi.astype(F32)
    mid = r.astype(BF16)
    lo = (r - mid.astype(F32)).astype(BF16)
    return hi, mid, lo


def _dot3(a, b):
    ah, al = _split2(a)
    bh, bl = _split2(b)
    d = lambda x, y: jnp.dot(x, y, preferred_element_type=F32)
    return d(ah, bh) + (d(ah, bl) + d(al, bh))


def _dot3_nt(a, b):
    ah, al = _split2(a)
    bh, bl = _split2(b)
    return _nt(ah, bh) + (_nt(ah, bl) + _nt(al, bh))


def _dot_sel_lhs(a01, b):
    a = a01.astype(BF16)
    bh, bm, bl = _split3(b)
    d = lambda y: jnp.dot(a, y, preferred_element_type=F32)
    return d(bh) + (d(bm) + d(bl))


def _dot_sel_rhs(a, b01):
    b = b01.astype(BF16)
    ah, am, al = _split3(a)
    d = lambda x: jnp.dot(x, b, preferred_element_type=F32)
    return d(ah) + (d(am) + d(al))


def _sigmoid(x):
    return 1.0 / (1.0 + jnp.exp(-x))


def _silu(x):
    return x * _sigmoid(x)


def _gelu_tanh(x):
    c = math.sqrt(2.0 / math.pi)
    return 0.5 * x * (1.0 + jnp.tanh(c * (x + 0.044715 * (x * x * x))))


def _softplus(x):
    return jnp.maximum(x, 0.0) + jnp.log(1.0 + jnp.exp(-jnp.abs(x)))


def _layer_norm(x, g, b):
    mu = jnp.mean(x, axis=-1, keepdims=True)
    xc = x - mu
    var = jnp.mean(xc * xc, axis=-1, keepdims=True)
    return xc * lax.rsqrt(var + LN_EPS) * g + b


def _block_diag(x, n):
    r, c = x.shape
    t = jnp.concatenate([x] * n, axis=0)
    ri = lax.broadcasted_iota(jnp.int32, t.shape, 0) // r
    ci = lax.broadcasted_iota(jnp.int32, t.shape, 1) // (c // n)
    return jnp.where(ri == ci, t, 0.0)


def _cparams(sem):
    return pltpu.CompilerParams(dimension_semantics=sem, vmem_limit_bytes=VMEM_LIMIT)


def _proj_kernel(x_ref, g_ref, w_ref, o_nsa, o_gm, o_cv, o_rw):
    x = x_ref[...]
    ms = jnp.mean(x * x, axis=-1, keepdims=True)
    h = (x * lax.rsqrt(ms + NORM_EPS) * g_ref[...]).astype(BF16)
    off = 0
    for o in (o_nsa, o_gm, o_cv, o_rw):
        wd = o.shape[-1]
        o[...] = jnp.dot(h, w_ref[:, off:off + wd], preferred_element_type=F32)
        off += wd


def _in_proj(x2, g_pre, w_p, tm=256):
    n, d = x2.shape
    widths = (NSA_W, GM_W, CV_W, RW_W)
    return pl.pallas_call(
        _proj_kernel,
        grid=(n // tm,),
        in_specs=[pl.BlockSpec((tm, d), lambda i: (i, 0)),
                  pl.BlockSpec((1, d), lambda i: (0, 0)),
                  pl.BlockSpec((d, N_PROJ), lambda i: (0, 0))],
        out_specs=[pl.BlockSpec((tm, w), lambda i: (i, 0)) for w in widths],
        out_shape=[jax.ShapeDtypeStruct((n, w), F32) for w in widths],
        compiler_params=_cparams(("parallel",)),
        name="in_proj",
    )(x2, g_pre.reshape(1, d), w_p)


def _cmp_kernel(kv_ref, pos_ref, w1_ref, w2_ref, o_ref, ot_ref):
    nc = o_ref.shape[0]
    dk = o_ref.shape[1]
    first = None
    second = None
    for r in range(CMP_STRIDE):
        x = kv_ref[pl.ds(r, nc, stride=CMP_STRIDE), :]
        t = _dot3(x + pos_ref[r], w1_ref[r])
        first = t if first is None else first + t
        t = _dot3(x + pos_ref[CMP_STRIDE + r], w1_ref[CMP_STRIDE + r])
        second = t if second is None else second + t
    hid = first + pltpu.roll(second, nc - 1, axis=0)
    out = _dot3(_silu(hid), w2_ref[...])
    o_ref[...] = out[:, :dk]
    ot_ref[...] = out.T[dk:, :]


def _nsa_compress(nsa3, pos, w1, w2):
    b, s_len, _ = nsa3.shape
    nc = s_len // CMP_STRIDE
    dk = NSA_DK
    zw = jnp.zeros((CMP_BLOCK, dk, CMP_HIDDEN), F32)
    w1r = w1.reshape(2, CMP_BLOCK, dk, CMP_HIDDEN)
    w1bd = jnp.concatenate([jnp.concatenate([w1r[0], zw], axis=2),
                            jnp.concatenate([zw, w1r[1]], axis=2)], axis=1)
    z2 = jnp.zeros((CMP_HIDDEN, dk), F32)
    w2bd = jnp.concatenate([jnp.concatenate([w2[0], z2], axis=1),
                            jnp.concatenate([z2, w2[1]], axis=1)], axis=0)
    posr = jnp.concatenate([pos[0], pos[1]], axis=1).reshape(CMP_BLOCK, 1, 2 * dk)
    full = lambda a: pl.BlockSpec(a.shape, lambda i: (0,) * a.ndim)
    return pl.pallas_call(
        _cmp_kernel,
        grid=(b,),
        in_specs=[pl.BlockSpec((None, s_len, LANES), lambda i: (i, 0, 6)),
                  full(posr), full(w1bd), full(w2bd)],
        out_specs=[pl.BlockSpec((None, nc, dk), lambda i: (i, 0, 0)),
                   pl.BlockSpec((None, dk, nc), lambda i: (i, 0, 0))],
        out_shape=[jax.ShapeDtypeStruct((b, nc, dk), F32),
                   jax.ShapeDtypeStruct((b, dk, nc), F32)],
        compiler_params=_cparams(("parallel",)),
        name="nsa_compress",
    )(nsa3, posr, w1bd, w2bd)


def _nsa_kernel(q_ref, za_ref, g_ref, ks_ref, kw_ref, kc_ref, vct_ref, pt_ref, bt_ref, imat_ref,
                o_ref, ksb, vst, kwb, vwt, madd, m_s, acc_s, *, n_top, n_win):
    c = pl.program_id(1)
    n_qt = pl.num_programs(1)
    n_kt = ksb.shape[0]
    hd = HEAD_DIM
    nh = N_HEADS
    hq = nh * Q_TILE

    n_blk = imat_ref.shape[0]
    per_q = Q_TILE // SEL_BLOCK
    v_rows = vst.shape[1]

    @pl.when(c == 0)
    def _prep():
        key_blk = lax.broadcasted_iota(jnp.int32, (Q_TILE, n_blk), 0) // SEL_BLOCK
        col = lax.broadcasted_iota(jnp.int32, (Q_TILE, n_blk), 1)
        ones_row = jnp.where(lax.broadcasted_iota(jnp.int32, (v_rows - hd, Q_TILE), 0) == 0, 1.0, 0.0)

        def body(i, carry):
            r0 = pl.multiple_of(i * Q_TILE, Q_TILE)
            t = ks_ref[pl.ds(r0, Q_TILE), :]
            onehot = jnp.where(col == key_blk + per_q * i, 1.0, 0.0)
            ksb[i] = jnp.concatenate([t[:, :hd], onehot], axis=1).astype(BF16)
            vst[i] = jnp.concatenate([t.T[hd:, :], ones_row], axis=0).astype(BF16)
            t = kw_ref[pl.ds(r0, Q_TILE), :]
            kwb[i] = t[:, :hd].astype(BF16)
            vwt[i] = jnp.concatenate([t.T[hd:, :], ones_row], axis=0).astype(BF16)
            return carry
        lax.fori_loop(0, n_kt, body, 0)

    q = q_ref[...] * (hd ** -0.5 * LOG2E)
    q_all = jnp.concatenate([q[:, hd * h:hd * (h + 1)] for h in range(nh)], axis=0)
    qb = q_all.astype(BF16)
    g_t = _sigmoid(g_ref[...]).T
    gate = [jnp.concatenate([g_t[3 * h + br:3 * h + br + 1, :] for h in range(nh)], axis=1) for br in range(3)]

    kc = kc_ref[...]
    n_c = kc.shape[0]
    off = pl.multiple_of((n_qt - 1 - c) * (Q_TILE // CMP_STRIDE), SUBLANES)
    bias = pt_ref[pl.ds(off, n_c), :]
    s = _dot3_nt(kc, q_all) + bias
    valid = bias > 0.5 * NEG
    m = jnp.max(s, axis=0, keepdims=True)
    p = jnp.where(valid, jnp.exp2(s - m), 0.0)
    l = jnp.sum(p, axis=0, keepdims=True)
    pn = p * jnp.where(l > 0.0, 1.0 / l, 0.0)
    y_acc = gate[0] * _bdot(vct_ref[...], pn)
    pc = pn[:, 0:Q_TILE]
    for h in range(1, nh):
        pc = pc + pn[:, h * Q_TILE:(h + 1) * Q_TILE]

    tl = lax.broadcasted_iota(jnp.int32, (n_blk, Q_TILE), 1)
    jb = lax.broadcasted_iota(jnp.int32, (n_blk, Q_TILE), 0)
    cur = c * per_q + tl // SEL_BLOCK
    causal = jb <= cur
    need_rank = (c + 1) * per_q > n_top

    @pl.when(jnp.logical_not(need_rank))
    def _all_causal():
        madd[...] = jnp.where(causal, 0.0, NEG)

    @pl.when(need_rank)
    def _rank():
        imp = _dot_sel_lhs(imat_ref[...], pc)
        forced = (jb == 0) | (jb == cur) | (jb == cur - 1)
        val = jnp.where(causal, jnp.where(forced, FORCE, imp), NEG)
        n_r = n_blk // SUBLANES
        blocks = [val[SUBLANES * r:SUBLANES * (r + 1), :] for r in range(n_r)]
        cnts = [jnp.zeros((SUBLANES, Q_TILE), F32) for _ in range(n_r)]
        jrow = lax.broadcasted_iota(jnp.int32, (SUBLANES, Q_TILE), 0)
        for i in range(n_blk):
            vi = jnp.broadcast_to(val[i:i + 1, :], (SUBLANES, Q_TILE))
            for r in range(n_r):
                if SUBLANES * r > i:
                    beats = vi >= blocks[r]
                elif SUBLANES * r + SUBLANES - 1 < i:
                    beats = vi > blocks[r]
                else:
                    ge = jnp.where(vi >= blocks[r], 1.0, 0.0)
                    gt = jnp.where(vi > blocks[r], 1.0, 0.0)
                    cnts[r] = cnts[r] + jnp.where(jrow + SUBLANES * r > i, ge, gt)
                    continue
                cnts[r] = cnts[r] + jnp.where(beats, 1.0, 0.0)
        cnt = jnp.concatenate(cnts, axis=0)
        madd[...] = jnp.where(causal & (cnt < float(n_top)), 0.0, NEG)

    def reset_state():
        m_s[...] = jnp.full(m_s.shape, NEG, F32)
        acc_s[...] = jnp.zeros(acc_s.shape, F32)

    def update(ss, vts):
        m_old = m_s[...]
        m_new = m_old
        for s_i in ss:
            m_new = jnp.maximum(m_new, jnp.max(s_i, axis=0, keepdims=True))
        pv = None
        for s_i, v_i in zip(ss, vts):
            t = jnp.dot(v_i, jnp.exp2(s_i - m_new).astype(BF16), preferred_element_type=F32)
            pv = t if pv is None else pv + t
        acc_s[...] = jnp.exp2(m_old - m_new) * acc_s[...] + pv
        m_s[...] = m_new

    def branch_out(g):
        acc = acc_s[...]
        return (g / acc[hd:hd + 1, :]) * acc[:hd, :]

    pad = jnp.zeros((LANES - n_blk, Q_TILE), F32)
    m_t = jnp.concatenate([madd[...], pad], axis=0).T[:, :n_blk]
    blk_lane = lax.broadcasted_iota(jnp.int32, m_t.shape, 1)
    m_far = jnp.where(blk_lane < per_q * (c - 1), m_t, NEG)
    q_near = jnp.concatenate([q_all, jnp.concatenate([m_t] * nh, axis=0)], axis=1).astype(BF16)
    q_far = jnp.concatenate([q_all, jnp.concatenate([m_far] * nh, axis=0)], axis=1).astype(BF16)

    reset_state()
    kt1 = jnp.maximum(c - 1, 0)
    s0 = _nt(ksb[c], q_near) + bt_ref[0]
    s1 = _nt(ksb[kt1], q_near) + bt_ref[1] + jnp.where(c >= 1, 0.0, NEG)
    update([s0, s1], [vst[c], vst[kt1]])

    n_far = jnp.maximum(c - 1, 0)

    def far_body(g, carry):
        kts = [g * SEL_GROUP + j for j in range(SEL_GROUP)]
        update([_nt(ksb[kt], q_far) for kt in kts], [vst[kt] for kt in kts])
        return carry
    lax.fori_loop(0, (n_far + SEL_GROUP - 1) // SEL_GROUP, far_body, 0)
    y_acc = y_acc + branch_out(gate[1])

    reset_state()
    ss, vts = [], []
    for i in range(n_win + 1):
        kt = jnp.maximum(c - i, 0)
        s_i = _nt(kwb[kt], qb)
        if i == 0:
            s_i = s_i + bt_ref[0]
        else:
            if i == 1:
                s_i = s_i + bt_ref[1]
            elif i == n_win:
                s_i = s_i + bt_ref[2]
            s_i = s_i + jnp.where(c >= i, 0.0, NEG)
        ss.append(s_i)
        vts.append(vwt[kt])
    update(ss, vts)
    y_acc = y_acc + branch_out(gate[2])

    y_t = jnp.concatenate([y_acc[:, h * Q_TILE:(h + 1) * Q_TILE] for h in range(nh)], axis=0)
    o_ref[...] = y_t.T * _silu(za_ref[...])


def _nsa_kernel_multi(q_ref, za_ref, g_ref, ks_ref, kw_ref, kc_ref, vct_ref, pt_ref, bt_ref, imat_ref,
                      o_ref, ksb, vst, kwb, vwt, madd, val_s, cnt_s, m_s, acc_s, *, n_top, n_win):
    c = pl.program_id(1)
    n_qt = pl.num_programs(1)
    n_st = q_ref.shape[0]
    n_kt = ksb.shape[1]
    hd = HEAD_DIM
    nh = N_HEADS
    n_blk = imat_ref.shape[0]
    per_q = Q_TILE // SEL_BLOCK
    v_rows = vst.shape[2]
    streams = range(n_st)

    @pl.when(c == 0)
    def _prep():
        key_blk = lax.broadcasted_iota(jnp.int32, (Q_TILE, n_blk), 0) // SEL_BLOCK
        col = lax.broadcasted_iota(jnp.int32, (Q_TILE, n_blk), 1)
        ones_row = jnp.where(lax.broadcasted_iota(jnp.int32, (v_rows - hd, Q_TILE), 0) == 0, 1.0, 0.0)

        def body(i, carry):
            r0 = pl.multiple_of(i * Q_TILE, Q_TILE)
            onehot = jnp.where(col == key_blk + per_q * i, 1.0, 0.0)
            for s in streams:
                t = ks_ref[s, pl.ds(r0, Q_TILE), :]
                ksb[s, i] = jnp.concatenate([t[:, :hd], onehot], axis=1).astype(BF16)
                vst[s, i] = jnp.concatenate([t.T[hd:, :], ones_row], axis=0).astype(BF16)
                t = kw_ref[s, pl.ds(r0, Q_TILE), :]
                kwb[s, i] = t[:, :hd].astype(BF16)
                vwt[s, i] = jnp.concatenate([t.T[hd:, :], ones_row], axis=0).astype(BF16)
            return carry
        lax.fori_loop(0, n_kt, body, 0)

    def reset_state():
        m_s[...] = jnp.full(m_s.shape, NEG, F32)
        acc_s[...] = jnp.zeros(acc_s.shape, F32)

    def update(groups):
        m_old = [m_s[s] for s in streams]
        m_new = []
        for s in streams:
            m = m_old[s]
            for s_i in groups[s][0]:
                m = jnp.maximum(m, jnp.max(s_i, axis=0, keepdims=True))
            m_new.append(m)
        for s in streams:
            ps = [jnp.exp2(s_i - m_new[s]).astype(BF16) for s_i in groups[s][0]]
            vs = groups[s][1]
            pv = None
            for j in range(0, len(ps), 2):
                p_j = jnp.concatenate(ps[j:j + 2], axis=0)
                v_j = jnp.concatenate(vs[j:j + 2], axis=1)
                t = jnp.dot(v_j, p_j, preferred_element_type=F32)
                pv = t if pv is None else pv + t
            acc_s[s] = jnp.exp2(m_old[s] - m_new[s]) * acc_s[s] + pv
            m_s[s] = m_new[s]

    def branch_out(s, g):
        acc = acc_s[s]
        return (g / acc[hd:hd + 1, :]) * acc[:hd, :]

    q_all, qb, gate = [], [], []
    for s in streams:
        q = q_ref[s] * (hd ** -0.5 * LOG2E)
        qa = jnp.concatenate([q[:, hd * h:hd * (h + 1)] for h in range(nh)], axis=0)
        q_all.append(qa)
        qb.append(qa.astype(BF16))
        g_t = _sigmoid(g_ref[s]).T
        gate.append([jnp.concatenate([g_t[3 * h + br:3 * h + br + 1, :] for h in range(nh)], axis=1)
                     for br in range(3)])

    reset_state()
    groups = []
    for s in streams:
        ss, vts = [], []
        for i in range(n_win + 1):
            kt = jnp.maximum(c - i, 0)
            s_i = _nt(kwb[s, kt], qb[s])
            if i == 0:
                s_i = s_i + bt_ref[0]
            else:
                if i == 1:
                    s_i = s_i + bt_ref[1]
                elif i == n_win:
                    s_i = s_i + bt_ref[2]
                s_i = s_i + jnp.where(c >= i, 0.0, NEG)
            ss.append(s_i)
            vts.append(vwt[s, kt])
        groups.append((ss, vts))
    update(groups)
    y_acc = [branch_out(s, gate[s][2]) for s in streams]

    n_c = kc_ref.shape[1]
    off = pl.multiple_of((n_qt - 1 - c) * (Q_TILE // CMP_STRIDE), SUBLANES)
    bias = pt_ref[pl.ds(off, n_c), :]
    valid = bias > 0.5 * NEG
    pcs = []
    for s in streams:
        sc = _dot3_nt(kc_ref[s], q_all[s]) + bias
        m = jnp.max(sc, axis=0, keepdims=True)
        p = jnp.where(valid, jnp.exp2(sc - m), 0.0)
        l = jnp.sum(p, axis=0, keepdims=True)
        pn = p * jnp.where(l > 0.0, 1.0 / l, 0.0)
        y_acc[s] = y_acc[s] + gate[s][0] * _bdot(vct_ref[s], pn)
        pc = pn[:, 0:Q_TILE]
        for h in range(1, nh):
            pc = pc + pn[:, h * Q_TILE:(h + 1) * Q_TILE]
        pcs.append(pc)

    tl = lax.broadcasted_iota(jnp.int32, (n_blk, Q_TILE), 1)
    jb = lax.broadcasted_iota(jnp.int32, (n_blk, Q_TILE), 0)
    cur = c * per_q + tl // SEL_BLOCK
    causal = jb <= cur
    need_rank = (c + 1) * per_q > n_top

    @pl.when(jnp.logical_not(need_rank))
    def _all_causal():
        for s in streams:
            madd[s] = jnp.where(causal, 0.0, NEG)

    @pl.when(need_rank)
    def _rank():
        forced = (jb == 0) | (jb == cur) | (jb == cur - 1)
        n_r = n_blk // SUBLANES
        jrow = lax.broadcasted_iota(jnp.int32, (SUBLANES, Q_TILE), 0)
        for s in streams:
            imp = _dot_sel_lhs(imat_ref[...], pcs[s])
            val_s[s] = jnp.where(causal, jnp.where(forced, FORCE, imp), NEG)
            cnt_s[s] = jnp.zeros((n_blk, Q_TILE), F32)
        for ci in range(n_r):
            @pl.when(SUBLANES * ci < (c + 1) * per_q)
            def _chunk():
                for s in streams:
                    val = val_s[s]
                    blocks = [val[SUBLANES * r:SUBLANES * (r + 1), :] for r in range(n_r)]
                    cnts = [None] * n_r
                    for i in range(SUBLANES * ci, SUBLANES * (ci + 1)):
                        vi = jnp.broadcast_to(val[i:i + 1, :], (SUBLANES, Q_TILE))
                        for r in range(n_r):
                            if r > ci:
                                t = jnp.where(vi >= blocks[r], 1.0, 0.0)
                            elif r < ci:
                                t = jnp.where(vi > blocks[r], 1.0, 0.0)
                            else:
                                ge = jnp.where(vi >= blocks[r], 1.0, 0.0)
                                gt = jnp.where(vi > blocks[r], 1.0, 0.0)
                                t = jnp.where(jrow + SUBLANES * r > i, ge, gt)
                            cnts[r] = t if cnts[r] is None else cnts[r] + t
                    cnt_s[s] = cnt_s[s] + jnp.concatenate(cnts, axis=0)
        for s in streams:
            madd[s] = jnp.where(causal & (cnt_s[s] < float(n_top)), 0.0, NEG)

    reset_state()
    pad = jnp.zeros((LANES - n_blk, Q_TILE), F32)
    kt1 = jnp.maximum(c - 1, 0)
    q_far, groups = [], []
    for s in streams:
        m_t = jnp.concatenate([madd[s], pad], axis=0).T[:, :n_blk]
        blk_lane = lax.broadcasted_iota(jnp.int32, m_t.shape, 1)
        m_far = jnp.where(blk_lane < per_q * (c - 1), m_t, NEG)
        q_near = jnp.concatenate([q_all[s], jnp.concatenate([m_t] * nh, axis=0)], axis=1).astype(BF16)
        q_far.append(jnp.concatenate([q_all[s], jnp.concatenate([m_far] * nh, axis=0)], axis=1).astype(BF16))
        s0 = _nt(ksb[s, c], q_near) + bt_ref[0]
        s1 = _nt(ksb[s, kt1], q_near) + bt_ref[1] + jnp.where(c >= 1, 0.0, NEG)
        groups.append(([s0, s1], [vst[s, c], vst[s, kt1]]))
    update(groups)

    n_far = jnp.maximum(c - 1, 0)

    def far_body(g, carry):
        kts = [g * SEL_GROUP + j for j in range(SEL_GROUP)]
        update([([_nt(ksb[s, kt], q_far[s]) for kt in kts], [vst[s, kt] for kt in kts]) for s in streams])
        return carry
    lax.fori_loop(0, (n_far + SEL_GROUP - 1) // SEL_GROUP, far_body, 0)

    for s in streams:
        y = y_acc[s] + branch_out(s, gate[s][1])
        y_t = jnp.concatenate([y[:, h * Q_TILE:(h + 1) * Q_TILE] for h in range(nh)], axis=0)
        o_ref[s] = y_t.T * _silu(za_ref[s])


def _bucket_lookup(relc, idx, visible):
    r, cc = idx.shape
    onehot = (jnp.asarray(idx.reshape(1, -1)) == jnp.arange(relc.shape[0])[:, None]).astype(F32)
    vals = jnp.einsum("bh,bn->hn", relc, onehot, precision=lax.Precision.HIGHEST).reshape(-1, r, cc)
    vals = jnp.where(jnp.asarray(visible)[None], vals, NEG)
    return jnp.transpose(vals, (1, 0, 2)).reshape(r, -1).astype(F32)


def _nsa_tables(rel_bias, s_len):
    n_qt = s_len // Q_TILE
    n_c = s_len // CMP_STRIDE
    relc = rel_bias - rel_bias[REL_BUCKETS - 1][None, :]
    bk = _rel_buckets(s_len + Q_TILE)
    kl = np.arange(Q_TILE)[:, None]
    tq = np.arange(Q_TILE)[None, :]
    d0 = tq - kl
    diag = _bucket_lookup(relc, bk[np.clip(d0, 0, None)], d0 >= 0)
    prev = _bucket_lookup(relc, bk[Q_TILE + d0], np.ones_like(d0, bool))
    edge = jnp.asarray(np.tile(np.where(kl > tq, 0.0, NEG).astype(np.float32), (1, N_HEADS)))
    bt = jnp.stack([diag, prev, edge], axis=0) * LOG2E
    per_q = Q_TILE // CMP_STRIDE
    n_rows = n_c + per_q * (n_qt - 1)
    r = np.arange(n_rows)[:, None]
    dc = tq - CMP_STRIDE * (r - per_q * (n_qt - 1)) - (CMP_BLOCK - 1)
    pt = _bucket_lookup(relc, bk[np.clip(dc, 0, None)], dc >= 0) * LOG2E
    n_blk = s_len // SEL_BLOCK
    ratio = SEL_BLOCK // CMP_STRIDE
    jj = np.arange(n_blk)[:, None]
    ii = np.arange(n_c)[None, :]
    imat = ((ii >= ratio * jj - 1) & (ii <= ratio * jj + ratio - 1) & (ii < n_c - 1)).astype(np.float32)
    return bt, pt, jnp.asarray(imat)


def _nsa_attention(nsa3, kc, vct, bt, pt, imat):
    b, s_len, _ = nsa3.shape
    n_qt = s_len // Q_TILE
    n_c = s_len // CMP_STRIDE
    n_blk = s_len // SEL_BLOCK
    n_win = WINDOW // Q_TILE
    hd = HEAD_DIM
    hq = N_HEADS * Q_TILE
    v_rows = hd + 2 * SUBLANES
    n_st = NSA_STREAMS if b % NSA_STREAMS == 0 else 1
    kern = functools.partial(_nsa_kernel_multi, n_top=min(N_SEL, n_blk), n_win=n_win)
    return pl.pallas_call(
        kern,
        grid=(b // n_st, n_qt),
        in_specs=[pl.BlockSpec((n_st, Q_TILE, 256), lambda i, c: (i, c, 0)),
                  pl.BlockSpec((n_st, Q_TILE, 256), lambda i, c: (i, c, 1)),
                  pl.BlockSpec((n_st, Q_TILE, LANES), lambda i, c: (i, c, 7)),
                  pl.BlockSpec((n_st, s_len, LANES), lambda i, c: (i, 0, 4),
                               pipeline_mode=pl.Buffered(1)),
                  pl.BlockSpec((n_st, s_len, LANES), lambda i, c: (i, 0, 5),
                               pipeline_mode=pl.Buffered(1)),
                  pl.BlockSpec((n_st, n_c, hd), lambda i, c: (i, 0, 0)),
                  pl.BlockSpec((n_st, hd, n_c), lambda i, c: (i, 0, 0)),
                  pl.BlockSpec(pt.shape, lambda i, c: (0, 0)),
                  pl.BlockSpec(bt.shape, lambda i, c: (0, 0, 0)),
                  pl.BlockSpec(imat.shape, lambda i, c: (0, 0))],
        out_specs=pl.BlockSpec((n_st, Q_TILE, 256), lambda i, c: (i, c, 0)),
        out_shape=jax.ShapeDtypeStruct((b, s_len, 256), F32),
        scratch_shapes=[pltpu.VMEM((n_st, n_qt, Q_TILE, hd + n_blk), BF16),
                        pltpu.VMEM((n_st, n_qt, v_rows, Q_TILE), BF16),
                        pltpu.VMEM((n_st, n_qt, Q_TILE, hd), BF16),
                        pltpu.VMEM((n_st, n_qt, v_rows, Q_TILE), BF16),
                        pltpu.VMEM((n_st, n_blk, Q_TILE), F32),
                        pltpu.VMEM((n_st, n_blk, Q_TILE), F32),
                        pltpu.VMEM((n_st, n_blk, Q_TILE), F32),
                        pltpu.VMEM((n_st, 1, hq), F32),
                        pltpu.VMEM((n_st, v_rows, hq), F32)],
        compiler_params=_cparams(("arbitrary", "arbitrary")),
        name="nsa_attention",
    )(nsa3, nsa3, nsa3, nsa3, nsa3, kc, vct, pt, bt, imat)


def _gc_kernel(gm_ref, cv_ref, halo_ref, wall_ref, sb_ref, lng_ref, lnb_ref,
               cw_ref, cb_ref, clg_ref, clb_ref, cpw_ref, cpb_ref, ob_ref, oc_ref, xs, *, n_ct):
    c = pl.program_id(0) % n_ct
    t_len = gm_ref.shape[0]
    w = W_GRP
    gm = gm_ref[...]
    u = _gelu_tanh(gm[:, :w])
    v = _layer_norm(_gelu_tanh(gm[:, w:2 * w]), lng_ref[...], lnb_ref[...])
    wall = wall_ref[...]
    ti = lax.broadcasted_iota(jnp.int32, wall.shape, 0)
    si = lax.broadcasted_iota(jnp.int32, wall.shape, 1) % t_len
    wall = jnp.where(si <= ti, wall, 0.0)
    sv = _bdot(wall, _block_diag(v, N_HEADS)) + sb_ref[...]
    ob_ref[...] = u * sv * _silu(gm[:, 2 * w:])
    cv = cv_ref[...]
    hl = halo_ref[...]
    hx = hl[:, :w] * _sigmoid(hl[:, w:2 * w])
    hrows = hl.shape[0]
    xs[0:hrows, :] = jnp.where(c == 0, 0.0, hx)
    xs[hrows:hrows + t_len, :] = cv[:, :w] * _sigmoid(cv[:, w:2 * w])
    cw = cw_ref[...]
    acc = jnp.zeros((t_len, w), F32)
    base = hrows - (CONV_WIDTH - 1)
    for j in range(CONV_WIDTH):
        acc = acc + cw[j:j + 1, :] * xs[base + j:base + j + t_len, :]
    y = _layer_norm(acc + cb_ref[...], clg_ref[...], clb_ref[...])
    y = _bdot(_silu(y), cpw_ref[...]) + cpb_ref[...]
    oc_ref[...] = y * _silu(cv[:, 2 * w:])


def _gmlp_conv(gm, cv, s_len, wall, sb, lng, lnb, cw, cb, clg, clb, cpw, cpb, t_len=128, halo=32):
    n = gm.shape[0]
    n_ct = s_len // t_len
    w = W_GRP
    row = lambda a: a.reshape(1, w)
    full = lambda a: pl.BlockSpec(a.shape, lambda i: (0,) * a.ndim)
    cwp = jnp.concatenate([cw, jnp.zeros((32 - CONV_WIDTH, w), F32)], axis=0)
    args = (wall, sb, row(lng), row(lnb), cwp, row(cb), row(clg), row(clb), cpw.astype(BF16), row(cpb))
    per = t_len // halo
    return pl.pallas_call(
        functools.partial(_gc_kernel, n_ct=n_ct),
        grid=(n // t_len,),
        in_specs=[pl.BlockSpec((t_len, GM_W), lambda i: (i, 0)),
                  pl.BlockSpec((t_len, CV_W), lambda i: (i, 0)),
                  pl.BlockSpec((halo, CV_W), lambda i: (jnp.maximum(i * per - 1, 0), 0))]
                 + [full(a) for a in args],
        out_specs=[pl.BlockSpec((t_len, w), lambda i: (i, 0))] * 2,
        out_shape=[jax.ShapeDtypeStruct((n, w), F32)] * 2,
        scratch_shapes=[pltpu.VMEM((halo + t_len, w), F32)],
        compiler_params=_cparams(("parallel",)),
        name="gmlp_conv",
    )(gm, cv, cv, *args)


def _rwkv_kernel(rw_ref, prev_ref, mu_ref, wup_ref, aup_ref, vec_ref, o_ref, st):
    c = pl.program_id(0)
    n_b = rw_ref.shape[0]
    L = rw_ref.shape[1]
    w = W_GRP
    nh = N_HEADS
    hd = HEAD_DIM

    @pl.when(c == 0)
    def _init():
        st[...] = jnp.zeros(st.shape, F32)

    vec = vec_ref[...]
    w0, a0, k_k, k_a, r_k, gn_g, gn_b = [vec[i:i + 1, :] for i in range(7)]
    mu = mu_ref[...]
    rows = lax.broadcasted_iota(jnp.int32, (L, w), 0)
    lane = lax.broadcasted_iota(jnp.int32, (L, w), 1)
    s_of = lane % hd
    ones_bd = jnp.where((lax.broadcasted_iota(jnp.int32, (w, w), 0) // hd)
                        == (lax.broadcasted_iota(jnp.int32, (w, w), 1) // hd), 1.0, 0.0)
    tri = jnp.where(lax.broadcasted_iota(jnp.int32, (L, L), 1) <= lax.broadcasted_iota(jnp.int32, (L, L), 0), 1.0, 0.0)
    bd_mask = (lax.broadcasted_iota(jnp.int32, (w, w), 0) // hd) == (lax.broadcasted_iota(jnp.int32, (w, w), 1) // hd)
    strict = s_of < rows
    incl = s_of <= rows
    eye_all = jnp.where(s_of == rows, 1.0, 0.0)

    def bd(x):
        xb = x.astype(BF16)
        return jnp.where(bd_mask, jnp.concatenate([xb] * nh, axis=0), jnp.zeros((), BF16))

    def mm(a, b):
        return jnp.dot(a.astype(BF16), b.astype(BF16), preferred_element_type=F32)

    def mm_nt(a, b):
        return _nt(a.astype(BF16), b.astype(BF16))

    nb = range(n_b)
    stack = lambda parts: jnp.concatenate(parts, axis=0)
    part = lambda x, b: x[b * L:(b + 1) * L]
    zs = [rw_ref[b] for b in nb]
    zrow = lax.broadcasted_iota(jnp.int32, zs[0].shape, 0)
    n_prev = prev_ref.shape[1]
    xs = []
    for b in nb:
        last = jnp.where(c == 0, 0.0, prev_ref[b, n_prev - 1:n_prev, :])
        zprev = jnp.where(zrow == 0, last, pltpu.roll(zs[b], 1, axis=0))
        xs.append(zs[b] + mu * (zprev - zs[b]))
    xs = stack(xs)
    r = xs[:, 0:w]
    k = xs[:, w:2 * w]
    v = xs[:, 2 * w:3 * w]
    wa = xs[:, 3 * w:3 * w + LANES]
    zd = stack([z[:, 3 * w + LANES:] for z in zs])

    zz = w0 + _dot3(jnp.tanh(wa), wup_ref[...])
    lw = -jnp.exp(-_softplus(-zz) - 0.5)
    a = _sigmoid(a0 + _dot3(wa, aup_ref[...]))
    kkr = k * k_k
    kk = kkr / jnp.maximum(jnp.sqrt(mm(kkr * kkr, ones_bd)), 1e-12)
    k2 = k * (1.0 + (a - 1.0) * k_a)
    bb = kk * a

    lw_wide = jnp.concatenate([part(lw, b) for b in nb], axis=1)
    cs_wide = _dot_sel_lhs(tri, lw_wide)
    cs = stack([cs_wide[:, b * w:(b + 1) * w] for b in nb])
    g_t = jnp.exp(cs)
    g_inv = jnp.exp(-cs)
    kq = (kk * jnp.exp(cs - lw)).astype(BF16)
    rq = (r * g_t).astype(BF16)
    kt = k2 * g_inv
    bt = bb * g_inv

    lhs = [stack([part(kq, b), part(rq, b)]) for b in nb]
    ab_b = [_nt(lhs[b], bd(part(bt, b))) for b in nb]
    ab_k = [_nt(lhs[b], bd(part(kt, b))) for b in nb]
    a_b = [jnp.where(strict, x[:L], 0.0) for x in ab_b]
    b_b = [jnp.where(incl, x[L:], 0.0) for x in ab_b]
    ak_bk = [stack([jnp.where(strict, x[:L], 0.0), jnp.where(incl, x[L:], 0.0)]) for x in ab_k]

    npow = a_b
    tinv = [eye_all - x for x in a_b]
    for i in range(int(math.log2(L)) - 1):
        npow = [mm(x, bd(x)) for x in npow]
        tinv = [t + mm(t, bd(x)) for t, x in zip(tinv, npow)]

    s0 = [st[b] for b in nb]
    kh = [_nt(lhs[b], s0[b].astype(BF16)) for b in nb]
    akv = [mm(ak_bk[b], bd(part(v, b))) for b in nb]
    u = [mm(tinv[b], bd(kh[b][:L] + akv[b][:L])) for b in nb]
    y = [kh[b][L:] + akv[b][L:] - mm(b_b[b], bd(u[b])) for b in nb]

    zpad = jnp.zeros((w - 2 * L, w), F32)
    for b in nb:
        vu_t = stack([part(v, b), u[b], zpad]).T
        kb = stack([part(kt, b), -part(bt, b), zpad])
        d = mm(vu_t, kb)
        st[b] = g_t[(b + 1) * L - 1:(b + 1) * L, :] * (s0[b] + jnp.where(bd_mask, d, 0.0))

    y = stack(y)
    y_hi, y_lo = _split2(y)
    mean = (mm(y_hi, ones_bd) + mm(y_lo, ones_bd)) * (1.0 / hd)
    yc = y - mean
    var = mm(yc * yc, ones_bd) * (1.0 / hd)
    yn = yc * lax.rsqrt(var + RWKV_GN_EPS) * gn_g + gn_b
    bonus = mm(r * k2 * r_k, ones_bd) * v
    out = (yn + bonus) * _silu(zd)
    for b in nb:
        o_ref[b] = part(out, b)


def _rwkv(rw3, mu_p, wup_p, aup_p, vec):
    b, s_len, _ = rw3.shape
    L = RW_CHUNK
    w = W_GRP
    full = lambda a: pl.BlockSpec(a.shape, lambda c: (0,) * a.ndim)
    return pl.pallas_call(
        _rwkv_kernel,
        grid=(s_len // L,),
        in_specs=[pl.BlockSpec((b, L, RW_W), lambda c: (0, c, 0)),
                  pl.BlockSpec((b, SUBLANES, RW_W), lambda c: (0, jnp.maximum(c * (L // SUBLANES) - 1, 0), 0)),
                  full(mu_p), full(wup_p), full(aup_p), full(vec)],
        out_specs=pl.BlockSpec((b, L, w), lambda c: (0, c, 0)),
        out_shape=jax.ShapeDtypeStruct((b, s_len, w), F32),
        scratch_shapes=[pltpu.VMEM((b, w, w), F32)],
        compiler_params=_cparams(("arbitrary",)),
        name="rwkv7",
    )(rw3, rw3, mu_p, wup_p, aup_p, vec)


def _out_kernel(ya_ref, yb_ref, yc_ref, yd_ref, x_ref, p_ref, wo_ref, gp_ref, pp_ref, pg_ref, o_ref):
    w = W_GRP
    acc = None
    for i, y in enumerate((ya_ref, yb_ref, yc_ref, yd_ref)):
        t = jnp.dot(y[...].astype(BF16), wo_ref[i * w:(i + 1) * w, :], preferred_element_type=F32)
        acc = t if acc is None else acc + t
    ms = jnp.mean(acc * acc, axis=-1, keepdims=True)
    x1 = x_ref[...] + acc * lax.rsqrt(ms + NORM_EPS) * gp_ref[...]
    gate = _sigmoid(jnp.dot(x1.astype(BF16), pg_ref[...], preferred_element_type=F32))
    pe = jnp.dot(p_ref[...].astype(BF16), pp_ref[...], preferred_element_type=F32)
    o_ref[...] = x1 + gate * pe


def _out_proj(ys, x2, p2, w_out, g_post, ple_proj, ple_gate, tm=256):
    n, d = x2.shape
    w = W_GRP
    full = lambda a: pl.BlockSpec(a.shape, lambda i: (0,) * a.ndim)
    wo = w_out.astype(BF16)
    gp = g_post.reshape(1, d)
    pp = ple_proj.astype(BF16)
    pg = ple_gate.astype(BF16)
    return pl.pallas_call(
        _out_kernel,
        grid=(n // tm,),
        in_specs=[pl.BlockSpec((tm, w), lambda i: (i, 0))] * 4
                 + [pl.BlockSpec((tm, d), lambda i: (i, 0)),
                    pl.BlockSpec((tm, p2.shape[1]), lambda i: (i, 0)),
                    full(wo), full(gp), full(pp), full(pg)],
        out_specs=pl.BlockSpec((tm, d), lambda i: (i, 0)),
        out_shape=jax.ShapeDtypeStruct((n, d), F32),
        compiler_params=_cparams(("parallel",)),
        name="out_proj",
    )(*ys, x2, p2, wo, gp, pp, pg)


def _layer(x2, p2, b, s_len, bt, pt, imat, w_in, w_out, g_pre, g_post, nsa_pos, nsa_w1, nsa_w2,
           sgu_ln_g, sgu_ln_b, sgu_w, sgu_b, conv_w, conv_b, conv_ln_g, conv_ln_b, conv_pw, conv_pw_b,
           rwkv_mu, rwkv_w0, rwkv_w_up, rwkv_a0, rwkv_a_up, rwkv_k_k, rwkv_k_a, rwkv_r_k,
           rwkv_gn_g, rwkv_gn_b, ple_proj, ple_gate):
    n = x2.shape[0]
    w = W_GRP
    w_bf = w_in.astype(BF16)
    w_p = jnp.concatenate([jnp.zeros((w_in.shape[0], b_), BF16) if a_ is None else w_bf[:, a_:b_]
                           for a_, b_ in _SEGMENTS], axis=1)
    nsa, gm, cv, rw = _in_proj(x2, g_pre, w_p)

    nsa3 = nsa.reshape(b, s_len, NSA_W)
    kc, vct = _nsa_compress(nsa3, nsa_pos, nsa_w1, nsa_w2)
    ya = _nsa_attention(nsa3, kc, vct, bt, pt, imat).reshape(n, w)

    t_len = sgu_w.shape[-1]
    wall = jnp.transpose(sgu_w, (1, 0, 2)).reshape(t_len, N_HEADS * t_len)
    sb = jnp.repeat(sgu_b.T, HEAD_DIM, axis=1)
    yb, yc = _gmlp_conv(gm, cv, s_len, wall, sb, sgu_ln_g, sgu_ln_b, conv_w, conv_b,
                        conv_ln_g, conv_ln_b, conv_pw, conv_pw_b, t_len=t_len)

    lora = RWKV_LORA
    mu_p = jnp.concatenate([rwkv_mu, jnp.zeros((RW_W - rwkv_mu.shape[0],), F32)]).reshape(1, RW_W)
    wup_p = jnp.concatenate([rwkv_w_up, jnp.zeros((LANES - lora, w), F32)], axis=0)
    aup_p = jnp.concatenate([jnp.zeros((lora, w), F32), rwkv_a_up, jnp.zeros((LANES - 2 * lora, w), F32)], axis=0)
    vec = jnp.stack([rwkv_w0, rwkv_a0, rwkv_k_k, rwkv_k_a, rwkv_r_k.reshape(w), rwkv_gn_g, rwkv_gn_b,
                     jnp.zeros((w,), F32)], axis=0)
    yd = _rwkv(rw.reshape(b, s_len, RW_W), mu_p, wup_p, aup_p, vec).reshape(n, w)

    return _out_proj((ya, yb, yc, yd), x2, p2, w_out, g_post, ple_proj, ple_gate)


def kernel(x, p, rel_bias, w_in, w_out, g_pre, g_post, nsa_pos, nsa_w1, nsa_w2, sgu_ln_g, sgu_ln_b, sgu_w, sgu_b, conv_w, conv_b, conv_ln_g, conv_ln_b, conv_pw, conv_pw_b, rwkv_mu, rwkv_w0, rwkv_w_up, rwkv_a0, rwkv_a_up, rwkv_k_k, rwkv_k_a, rwkv_r_k, rwkv_gn_g, rwkv_gn_b, ple_proj, ple_gate):
    b, s_len, d = x.shape
    depth = w_in.shape[0]
    bt, pt, imat = _nsa_tables(rel_bias, s_len)
    x2 = x.reshape(b * s_len, d)
    per_layer = (w_in, w_out, g_pre, g_post, nsa_pos, nsa_w1, nsa_w2, sgu_ln_g, sgu_ln_b, sgu_w, sgu_b,
                 conv_w, conv_b, conv_ln_g, conv_ln_b, conv_pw, conv_pw_b, rwkv_mu, rwkv_w0, rwkv_w_up,
                 rwkv_a0, rwkv_a_up, rwkv_k_k, rwkv_k_a, rwkv_r_k, rwkv_gn_g, rwkv_gn_b, ple_proj, ple_gate)
    for i in range(depth):
        x2 = _layer(x2, p[i].reshape(b * s_len, -1), b, s_len, bt, pt, imat, *[a[i] for a in per_layer])
    return x2.reshape(b, s_len, d)
```

```python
import functools
import math

import numpy as np
import jax
import jax.numpy as jnp
from jax import lax
from jax.experimental import pallas as pl
from jax.experimental.pallas import tpu as pltpu

F32 = jnp.float32
BF16 = jnp.bfloat16

W_GRP = 256
HEAD_DIM = 64
N_HEADS = 4
NSA_DK = 64
CMP_STRIDE = 16
CMP_BLOCK = 32
CMP_HIDDEN = 128
SEL_BLOCK = 64
N_SEL = 16
WINDOW = 512
REL_BUCKETS = 32
REL_MAX_EXACT = 16
REL_MAX_DIST = 128
CONV_WIDTH = 31
RWKV_LORA = 32
RWKV_GN_EPS = 64e-5
NORM_EPS = 1e-6
LN_EPS = 1e-5
NEG = -1e30
FORCE = 1e4
LOG2E = math.log2(math.e)

LANES = 128
SUBLANES = 8
Q_TILE = 128
SEL_GROUP = 4
NSA_STREAMS = 4
RW_CHUNK = 64
VMEM_LIMIT = 48 * 1024 * 1024
NSA_VMEM_LIMIT = 56 * 1024 * 1024

NSA_W, GM_W, CV_W, RW_W = 1024, 768, 768, 1152
N_PROJ = NSA_W + GM_W + CV_W + RW_W


def _proj_segments():
    names = ["q", "kc", "vc", "ks", "vs", "kw", "vw", "g", "za", "u", "v", "zb", "ga", "gb", "zc", "rw", "zd"]
    widths = [256, 64, 64, 64, 64, 64, 64, 12, 256, 256, 256, 256, 256, 256, 256, 832, 256]
    o, off = {}, 0
    for n, w in zip(names, widths):
        o[n] = (off, off + w)
        off += w
    return [o["q"], o["za"], (o["ks"][0], o["vw"][1]), (o["kc"][0], o["vc"][1]), o["g"], (None, 116),
            (o["u"][0], o["zc"][1]), o["rw"], (None, 64), o["zd"]]


_SEGMENTS = _proj_segments()


def _rel_buckets(n):
    d = np.arange(n)
    nf = np.maximum(d, REL_MAX_EXACT).astype(np.float32)
    large = REL_MAX_EXACT + (np.log(nf / np.float32(REL_MAX_EXACT)) / np.float32(math.log(REL_MAX_DIST / REL_MAX_EXACT))
                             * np.float32(REL_BUCKETS - REL_MAX_EXACT)).astype(np.int32)
    large = np.minimum(large, REL_BUCKETS - 1)
    return np.where(d < REL_MAX_EXACT, d, large)


def _bdot(a, b):
    return jnp.dot(a.astype(BF16), b.astype(BF16), preferred_element_type=F32)


def _nt(a, b):
    return lax.dot_general(a, b, (((1,), (1,)), ((), ())), preferred_element_type=F32)


def _split2(a):
    hi = a.astype(BF16)
    lo = (a - hi.astype(F32)).astype(BF16)
    return hi, lo


def _split3(a):
    hi = a.astype(BF16)
    r = a - hi.astype(F32)
    mid = r.astype(BF16)
    lo = (r - mid.astype(F32)).astype(BF16)
    return hi, mid, lo


def _dot3(a, b):
    ah, al = _split2(a)
    bh, bl = _split2(b)
    d = lambda x, y: jnp.dot(x, y, preferred_element_type=F32)
    return d(ah, bh) + (d(ah, bl) + d(al, bh))


def _dot3_nt(a, b):
    ah, al = _split2(a)
    bh, bl = _split2(b)
    return _nt(ah, bh) + (_nt(ah, bl) + _nt(al, bh))


def _dot_sel_lhs(a01, b):
    a = a01.astype(BF16)
    bh, bm, bl = _split3(b)
    d = lambda y: jnp.dot(a, y, preferred_element_type=F32)
    return d(bh) + (d(bm) + d(bl))


def _dot_sel_rhs(a, b01):
    b = b01.astype(BF16)
    ah, am, al = _split3(a)
    d = lambda x: jnp.dot(x, b, preferred_element_type=F32)
    return d(ah) + (d(am) + d(al))


def _sigmoid(x):
    return 1.0 / (1.0 + jnp.exp(-x))


def _silu(x):
    return x * _sigmoid(x)


def _gelu_tanh(x):
    c = math.sqrt(2.0 / math.pi)
    return 0.5 * x * (1.0 + jnp.tanh(c * (x + 0.044715 * (x * x * x))))


def _softplus(x):
    return jnp.maximum(x, 0.0) + jnp.log(1.0 + jnp.exp(-jnp.abs(x)))


def _layer_norm(x, g, b):
    mu = jnp.mean(x, axis=-1, keepdims=True)
    xc = x - mu
    var = jnp.mean(xc * xc, axis=-1, keepdims=True)
    return xc * lax.rsqrt(var + LN_EPS) * g + b


def _block_diag(x, n):
    r, c = x.shape
    t = jnp.concatenate([x] * n, axis=0)
    ri = lax.broadcasted_iota(jnp.int32, t.shape, 0) // r
    ci = lax.broadcasted_iota(jnp.int32, t.shape, 1) // (c // n)
    return jnp.where(ri == ci, t, 0.0)


def _cparams(sem, vmem=VMEM_LIMIT):
    return pltpu.CompilerParams(dimension_semantics=sem, vmem_limit_bytes=vmem)


def _proj_kernel(x_ref, g_ref, w_ref, o_nsa, o_gm, o_cv, o_rw):
    x = x_ref[...]
    ms = jnp.mean(x * x, axis=-1, keepdims=True)
    h = (x * lax.rsqrt(ms + NORM_EPS) * g_ref[...]).astype(BF16)
    off = 0
    for o in (o_nsa, o_gm, o_cv, o_rw):
        wd = o.shape[-1]
        o[...] = jnp.dot(h, w_ref[:, off:off + wd], preferred_element_type=F32)
        off += wd


def _in_proj(x2, g_pre, w_p, tm=256):
    n, d = x2.shape
    widths = (NSA_W, GM_W, CV_W, RW_W)
    return pl.pallas_call(
        _proj_kernel,
        grid=(n // tm,),
        in_specs=[pl.BlockSpec((tm, d), lambda i: (i, 0)),
                  pl.BlockSpec((1, d), lambda i: (0, 0)),
                  pl.BlockSpec((d, N_PROJ), lambda i: (0, 0))],
        out_specs=[pl.BlockSpec((tm, w), lambda i: (i, 0)) for w in widths],
        out_shape=[jax.ShapeDtypeStruct((n, w), F32) for w in widths],
        compiler_params=_cparams(("parallel",)),
        name="in_proj",
    )(x2, g_pre.reshape(1, d), w_p)


def _cmp_kernel(kv_ref, pos_ref, w1_ref, w2_ref, o_ref, ot_ref):
    nc = o_ref.shape[0]
    dk = o_ref.shape[1]
    first = None
    second = None
    for r in range(CMP_STRIDE):
        x = kv_ref[pl.ds(r, nc, stride=CMP_STRIDE), :]
        t = _dot3(x + pos_ref[r], w1_ref[r])
        first = t if first is None else first + t
        t = _dot3(x + pos_ref[CMP_STRIDE + r], w1_ref[CMP_STRIDE + r])
        second = t if second is None else second + t
    hid = first + pltpu.roll(second, nc - 1, axis=0)
    out = _dot3(_silu(hid), w2_ref[...])
    o_ref[...] = out[:, :dk]
    ot_ref[...] = out.T[dk:, :]


def _nsa_compress(nsa3, pos, w1, w2):
    b, s_len, _ = nsa3.shape
    nc = s_len // CMP_STRIDE
    dk = NSA_DK
    zw = jnp.zeros((CMP_BLOCK, dk, CMP_HIDDEN), F32)
    w1r = w1.reshape(2, CMP_BLOCK, dk, CMP_HIDDEN)
    w1bd = jnp.concatenate([jnp.concatenate([w1r[0], zw], axis=2),
                            jnp.concatenate([zw, w1r[1]], axis=2)], axis=1)
    z2 = jnp.zeros((CMP_HIDDEN, dk), F32)
    w2bd = jnp.concatenate([jnp.concatenate([w2[0], z2], axis=1),
                            jnp.concatenate([z2, w2[1]], axis=1)], axis=0)
    posr = jnp.concatenate([pos[0], pos[1]], axis=1).reshape(CMP_BLOCK, 1, 2 * dk)
    full = lambda a: pl.BlockSpec(a.shape, lambda i: (0,) * a.ndim)
    return pl.pallas_call(
        _cmp_kernel,
        grid=(b,),
        in_specs=[pl.BlockSpec((None, s_len, LANES), lambda i: (i, 0, 6)),
                  full(posr), full(w1bd), full(w2bd)],
        out_specs=[pl.BlockSpec((None, nc, dk), lambda i: (i, 0, 0)),
                   pl.BlockSpec((None, dk, nc), lambda i: (i, 0, 0))],
        out_shape=[jax.ShapeDtypeStruct((b, nc, dk), F32),
                   jax.ShapeDtypeStruct((b, dk, nc), F32)],
        compiler_params=_cparams(("parallel",)),
        name="nsa_compress",
    )(nsa3, posr, w1bd, w2bd)


def _nsa_kernel(q_ref, za_ref, g_ref, ks_ref, kw_ref, kc_ref, vct_ref, pt_ref, bt_ref, imat_ref,
                o_ref, ksb, vst, kwb, vwt, madd, m_s, acc_s, *, n_top, n_win):
    c = pl.program_id(1)
    n_qt = pl.num_programs(1)
    n_kt = ksb.shape[0]
    hd = HEAD_DIM
    nh = N_HEADS
    hq = nh * Q_TILE

    n_blk = imat_ref.shape[0]
    per_q = Q_TILE // SEL_BLOCK
    v_rows = vst.shape[1]

    @pl.when(c == 0)
    def _prep():
        key_blk = lax.broadcasted_iota(jnp.int32, (Q_TILE, n_blk), 0) // SEL_BLOCK
        col = lax.broadcasted_iota(jnp.int32, (Q_TILE, n_blk), 1)
        ones_row = jnp.where(lax.broadcasted_iota(jnp.int32, (v_rows - hd, Q_TILE), 0) == 0, 1.0, 0.0)

        def body(i, carry):
            r0 = pl.multiple_of(i * Q_TILE, Q_TILE)
            t = ks_ref[pl.ds(r0, Q_TILE), :]
            onehot = jnp.where(col == key_blk + per_q * i, 1.0, 0.0)
            ksb[i] = jnp.concatenate([t[:, :hd], onehot], axis=1).astype(BF16)
            vst[i] = jnp.concatenate([t.T[hd:, :], ones_row], axis=0).astype(BF16)
            t = kw_ref[pl.ds(r0, Q_TILE), :]
            kwb[i] = t[:, :hd].astype(BF16)
            vwt[i] = jnp.concatenate([t.T[hd:, :], ones_row], axis=0).astype(BF16)
            return carry
        lax.fori_loop(0, n_kt, body, 0)

    q = q_ref[...] * (hd ** -0.5 * LOG2E)
    q_all = jnp.concatenate([q[:, hd * h:hd * (h + 1)] for h in range(nh)], axis=0)
    qb = q_all.astype(BF16)
    g_t = _sigmoid(g_ref[...]).T
    gate = [jnp.concatenate([g_t[3 * h + br:3 * h + br + 1, :] for h in range(nh)], axis=1) for br in range(3)]

    kc = kc_ref[...]
    n_c = kc.shape[0]
    off = pl.multiple_of((n_qt - 1 - c) * (Q_TILE // CMP_STRIDE), SUBLANES)
    bias = pt_ref[pl.ds(off, n_c), :]
    s = _dot3_nt(kc, q_all) + bias
    valid = bias > 0.5 * NEG
    m = jnp.max(s, axis=0, keepdims=True)
    p = jnp.where(valid, jnp.exp2(s - m), 0.0)
    l = jnp.sum(p, axis=0, keepdims=True)
    pn = p * jnp.where(l > 0.0, 1.0 / l, 0.0)
    y_acc = gate[0] * _bdot(vct_ref[...], pn)
    pc = pn[:, 0:Q_TILE]
    for h in range(1, nh):
        pc = pc + pn[:, h * Q_TILE:(h + 1) * Q_TILE]

    tl = lax.broadcasted_iota(jnp.int32, (n_blk, Q_TILE), 1)
    jb = lax.broadcasted_iota(jnp.int32, (n_blk, Q_TILE), 0)
    cur = c * per_q + tl // SEL_BLOCK
    causal = jb <= cur
    need_rank = (c + 1) * per_q > n_top

    @pl.when(jnp.logical_not(need_rank))
    def _all_causal():
        madd[...] = jnp.where(causal, 0.0, NEG)

    @pl.when(need_rank)
    def _rank():
        imp = _dot_sel_lhs(imat_ref[...], pc)
        forced = (jb == 0) | (jb == cur) | (jb == cur - 1)
        val = jnp.where(causal, jnp.where(forced, FORCE, imp), NEG)
        n_r = n_blk // SUBLANES
        blocks = [val[SUBLANES * r:SUBLANES * (r + 1), :] for r in range(n_r)]
        cnts = [jnp.zeros((SUBLANES, Q_TILE), F32) for _ in range(n_r)]
        jrow = lax.broadcasted_iota(jnp.int32, (SUBLANES, Q_TILE), 0)
        for i in range(n_blk):
            vi = jnp.broadcast_to(val[i:i + 1, :], (SUBLANES, Q_TILE))
            for r in range(n_r):
                if SUBLANES * r > i:
                    beats = vi >= blocks[r]
                elif SUBLANES * r + SUBLANES - 1 < i:
                    beats = vi > blocks[r]
                else:
                    ge = jnp.where(vi >= blocks[r], 1.0, 0.0)
                    gt = jnp.where(vi > blocks[r], 1.0, 0.0)
                    cnts[r] = cnts[r] + jnp.where(jrow + SUBLANES * r > i, ge, gt)
                    continue
                cnts[r] = cnts[r] + jnp.where(beats, 1.0, 0.0)
        cnt = jnp.concatenate(cnts, axis=0)
        madd[...] = jnp.where(causal & (cnt < float(n_top)), 0.0, NEG)

    def reset_state():
        m_s[...] = jnp.full(m_s.shape, NEG, F32)
        acc_s[...] = jnp.zeros(acc_s.shape, F32)

    def update(ss, vts):
        m_old = m_s[...]
        m_new = m_old
        for s_i in ss:
            m_new = jnp.maximum(m_new, jnp.max(s_i, axis=0, keepdims=True))
        pv = None
        for s_i, v_i in zip(ss, vts):
            t = jnp.dot(v_i, jnp.exp2(s_i - m_new).astype(BF16), preferred_element_type=F32)
            pv = t if pv is None else pv + t
        acc_s[...] = jnp.exp2(m_old - m_new) * acc_s[...] + pv
        m_s[...] = m_new

    def branch_out(g):
        acc = acc_s[...]
        return (g / acc[hd:hd + 1, :]) * acc[:hd, :]

    pad = jnp.zeros((LANES - n_blk, Q_TILE), F32)
    m_t = jnp.concatenate([madd[...], pad], axis=0).T[:, :n_blk]
    blk_lane = lax.broadcasted_iota(jnp.int32, m_t.shape, 1)
    m_far = jnp.where(blk_lane < per_q * (c - 1), m_t, NEG)
    q_near = jnp.concatenate([q_all, jnp.concatenate([m_t] * nh, axis=0)], axis=1).astype(BF16)
    q_far = jnp.concatenate([q_all, jnp.concatenate([m_far] * nh, axis=0)], axis=1).astype(BF16)

    reset_state()
    kt1 = jnp.maximum(c - 1, 0)
    s0 = _nt(ksb[c], q_near) + bt_ref[0]
    s1 = _nt(ksb[kt1], q_near) + bt_ref[1] + jnp.where(c >= 1, 0.0, NEG)
    update([s0, s1], [vst[c], vst[kt1]])

    n_far = jnp.maximum(c - 1, 0)

    def far_body(g, carry):
        kts = [g * SEL_GROUP + j for j in range(SEL_GROUP)]
        update([_nt(ksb[kt], q_far) for kt in kts], [vst[kt] for kt in kts])
        return carry
    lax.fori_loop(0, (n_far + SEL_GROUP - 1) // SEL_GROUP, far_body, 0)
    y_acc = y_acc + branch_out(gate[1])

    reset_state()
    ss, vts = [], []
    for i in range(n_win + 1):
        kt = jnp.maximum(c - i, 0)
        s_i = _nt(kwb[kt], qb)
        if i == 0:
            s_i = s_i + bt_ref[0]
        else:
            if i == 1:
                s_i = s_i + bt_ref[1]
            elif i == n_win:
                s_i = s_i + bt_ref[2]
            s_i = s_i + jnp.where(c >= i, 0.0, NEG)
        ss.append(s_i)
        vts.append(vwt[kt])
    update(ss, vts)
    y_acc = y_acc + branch_out(gate[2])

    y_t = jnp.concatenate([y_acc[:, h * Q_TILE:(h + 1) * Q_TILE] for h in range(nh)], axis=0)
    o_ref[...] = y_t.T * _silu(za_ref[...])


def _nsa_kernel_multi(q_ref, za_ref, g_ref, ks_ref, kw_ref, kc_ref, vct_ref, pt_ref, bt_ref, imat_ref,
                      o_ref, ksb, vst, kwb, vwt, madd, val_s, cnt_s, m_s, acc_s, *, n_top, n_win):
    c = pl.program_id(1)
    n_qt = pl.num_programs(1)
    n_st = q_ref.shape[0]
    n_kt = ksb.shape[1]
    hd = HEAD_DIM
    nh = N_HEADS
    n_blk = imat_ref.shape[0]
    per_q = Q_TILE // SEL_BLOCK
    v_rows = vst.shape[2]
    streams = range(n_st)

    @pl.when(c == 0)
    def _prep():
        key_blk = lax.broadcasted_iota(jnp.int32, (Q_TILE, n_blk), 0) // SEL_BLOCK
        col = lax.broadcasted_iota(jnp.int32, (Q_TILE, n_blk), 1)
        ones_row = jnp.where(lax.broadcasted_iota(jnp.int32, (v_rows - hd, Q_TILE), 0) == 0, 1.0, 0.0)

        def body(i, carry):
            r0 = pl.multiple_of(i * Q_TILE, Q_TILE)
            onehot = jnp.where(col == key_blk + per_q * i, 1.0, 0.0)
            for s in streams:
                t = ks_ref[s, pl.ds(r0, Q_TILE), :]
                ksb[s, i] = jnp.concatenate([t[:, :hd], onehot], axis=1).astype(BF16)
                vst[s, i] = jnp.concatenate([t.T[hd:, :], ones_row], axis=0).astype(BF16)
                t = kw_ref[s, pl.ds(r0, Q_TILE), :]
                kwb[s, i] = t[:, :hd].astype(BF16)
                vwt[s, i] = jnp.concatenate([t.T[hd:, :], ones_row], axis=0).astype(BF16)
            return carry
        lax.fori_loop(0, n_kt, body, 0)

    def reset_state():
        m_s[...] = jnp.full(m_s.shape, NEG, F32)
        acc_s[...] = jnp.zeros(acc_s.shape, F32)

    def update(groups, between=None):
        m_old = [m_s[s] for s in streams]
        m_new = []
        for s in streams:
            m = m_old[s]
            for s_i in groups[s][0]:
                m = jnp.maximum(m, jnp.max(s_i, axis=0, keepdims=True))
            m_new.append(m)
        for s in streams:
            ss, vs = groups[s]
            pv = None
            for j in range(0, len(ss), 2):
                if between is not None:
                    between(s, j)
                p_j = jnp.concatenate([jnp.exp2(s_i - m_new[s]).astype(BF16) for s_i in ss[j:j + 2]], axis=0)
                v_j = jnp.concatenate(vs[j:j + 2], axis=1)
                t = jnp.dot(v_j, p_j, preferred_element_type=F32)
                pv = t if pv is None else pv + t
            acc_s[s] = jnp.exp2(m_old[s] - m_new[s]) * acc_s[s] + pv
            m_s[s] = m_new[s]

    def branch_out(s, g):
        acc = acc_s[s]
        return (g / acc[hd:hd + 1, :]) * acc[:hd, :]

    q_all, qb, gate = [], [], []
    for s in streams:
        q = q_ref[s] * (hd ** -0.5 * LOG2E)
        qa = jnp.concatenate([q[:, hd * h:hd * (h + 1)] for h in range(nh)], axis=0)
        q_all.append(qa)
        qb.append(qa.astype(BF16))
        g_t = _sigmoid(g_ref[s]).T
        gate.append([jnp.concatenate([g_t[3 * h + br:3 * h + br + 1, :] for h in range(nh)], axis=1)
                     for br in range(3)])

    reset_state()
    groups = []
    for s in streams:
        ss, vts = [], []
        for i in range(n_win + 1):
            kt = jnp.maximum(c - i, 0)
            s_i = _nt(kwb[s, kt], qb[s])
            if i == 0:
                s_i = s_i + bt_ref[0]
            else:
                if i == 1:
                    s_i = s_i + bt_ref[1]
                elif i == n_win:
                    s_i = s_i + bt_ref[2]
                s_i = s_i + jnp.where(c >= i, 0.0, NEG)
            ss.append(s_i)
            vts.append(vwt[s, kt])
        groups.append((ss, vts))
    update(groups)
    y_acc = [branch_out(s, gate[s][2]) for s in streams]

    n_c = kc_ref.shape[1]
    off = pl.multiple_of((n_qt - 1 - c) * (Q_TILE // CMP_STRIDE), SUBLANES)
    bias = pt_ref[pl.ds(off, n_c), :]
    valid = bias > 0.5 * NEG
    scs = [_dot3_nt(kc_ref[s], q_all[s]) + bias for s in streams]
    ps = [jnp.where(valid, jnp.exp2(sc - jnp.max(sc, axis=0, keepdims=True)), 0.0) for sc in scs]
    ls = [jnp.sum(p, axis=0, keepdims=True) for p in ps]
    pns = [p * jnp.where(l > 0.0, 1.0 / l, 0.0) for p, l in zip(ps, ls)]
    pcs = []
    for s in streams:
        y_acc[s] = y_acc[s] + gate[s][0] * _bdot(vct_ref[s], pns[s])
        pc = pns[s][:, 0:Q_TILE]
        for h in range(1, nh):
            pc = pc + pns[s][:, h * Q_TILE:(h + 1) * Q_TILE]
        pcs.append(pc)

    tl = lax.broadcasted_iota(jnp.int32, (n_blk, Q_TILE), 1)
    jb = lax.broadcasted_iota(jnp.int32, (n_blk, Q_TILE), 0)
    cur = c * per_q + tl // SEL_BLOCK
    causal = jb <= cur
    need_rank = (c + 1) * per_q > n_top

    @pl.when(jnp.logical_not(need_rank))
    def _all_causal():
        for s in streams:
            madd[s] = jnp.where(causal, 0.0, NEG)

    @pl.when(need_rank)
    def _rank():
        forced = (jb == 0) | (jb == cur) | (jb == cur - 1)
        n_r = n_blk // SUBLANES
        jrow = lax.broadcasted_iota(jnp.int32, (SUBLANES, Q_TILE), 0)
        for s in streams:
            imp = _dot_sel_lhs(imat_ref[...], pcs[s])
            val_s[s] = jnp.where(causal, jnp.where(forced, FORCE, imp), NEG)
            cnt_s[s] = jnp.zeros((n_blk, Q_TILE), F32)
        for ci in range(n_r):
            @pl.when(SUBLANES * ci < (c + 1) * per_q)
            def _chunk():
                for s in streams:
                    val = val_s[s]
                    blocks = [val[SUBLANES * r:SUBLANES * (r + 1), :] for r in range(n_r)]
                    cnts = [None] * n_r
                    for i in range(SUBLANES * ci, SUBLANES * (ci + 1)):
                        vi = jnp.broadcast_to(val[i:i + 1, :], (SUBLANES, Q_TILE))
                        for r in range(n_r):
                            if r > ci:
                                t = jnp.where(vi >= blocks[r], 1.0, 0.0)
                            elif r < ci:
                                t = jnp.where(vi > blocks[r], 1.0, 0.0)
                            else:
                                ge = jnp.where(vi >= blocks[r], 1.0, 0.0)
                                gt = jnp.where(vi > blocks[r], 1.0, 0.0)
                                t = jnp.where(jrow + SUBLANES * r > i, ge, gt)
                            cnts[r] = t if cnts[r] is None else cnts[r] + t
                    cnt_s[s] = cnt_s[s] + jnp.concatenate(cnts, axis=0)
        for s in streams:
            madd[s] = jnp.where(causal & (cnt_s[s] < float(n_top)), 0.0, NEG)

    reset_state()
    pad = jnp.zeros((LANES - n_blk, Q_TILE), F32)
    kt1 = jnp.maximum(c - 1, 0)
    q_far, groups = [], []
    for s in streams:
        m_t = jnp.concatenate([madd[s], pad], axis=0).T[:, :n_blk]
        blk_lane = lax.broadcasted_iota(jnp.int32, m_t.shape, 1)
        m_far = jnp.where(blk_lane < per_q * (c - 1), m_t, NEG)
        q_near = jnp.concatenate([q_all[s], jnp.concatenate([m_t] * nh, axis=0)], axis=1).astype(BF16)
        q_far.append(jnp.concatenate([q_all[s], jnp.concatenate([m_far] * nh, axis=0)], axis=1).astype(BF16))
        s0 = _nt(ksb[s, c], q_near) + bt_ref[0]
        s1 = _nt(ksb[s, kt1], q_near) + bt_ref[1] + jnp.where(c >= 1, 0.0, NEG)
        groups.append(([s0, s1], [vst[s, c], vst[s, kt1]]))

    update(groups)

    n_far = jnp.maximum(c - 1, 0)

    def far_body(g, carry):
        kts = [g * SEL_GROUP + j for j in range(SEL_GROUP)]
        update([([_nt(ksb[s, kt], q_far[s]) for kt in kts], [vst[s, kt] for kt in kts]) for s in streams])
        return carry
    lax.fori_loop(0, (n_far + SEL_GROUP - 1) // SEL_GROUP, far_body, 0)

    for s in streams:
        y = y_acc[s] + branch_out(s, gate[s][1])
        y_t = jnp.concatenate([y[:, h * Q_TILE:(h + 1) * Q_TILE] for h in range(nh)], axis=0)
        o_ref[s] = y_t.T * _silu(za_ref[s])


def _bucket_lookup(relc, idx, visible):
    r, cc = idx.shape
    onehot = (jnp.asarray(idx.reshape(1, -1)) == jnp.arange(relc.shape[0])[:, None]).astype(F32)
    vals = jnp.einsum("bh,bn->hn", relc, onehot, precision=lax.Precision.HIGHEST).reshape(-1, r, cc)
    vals = jnp.where(jnp.asarray(visible)[None], vals, NEG)
    return jnp.transpose(vals, (1, 0, 2)).reshape(r, -1).astype(F32)


def _nsa_tables(rel_bias, s_len):
    n_qt = s_len // Q_TILE
    n_c = s_len // CMP_STRIDE
    relc = rel_bias - rel_bias[REL_BUCKETS - 1][None, :]
    bk = _rel_buckets(s_len + Q_TILE)
    kl = np.arange(Q_TILE)[:, None]
    tq = np.arange(Q_TILE)[None, :]
    d0 = tq - kl
    diag = _bucket_lookup(relc, bk[np.clip(d0, 0, None)], d0 >= 0)
    prev = _bucket_lookup(relc, bk[Q_TILE + d0], np.ones_like(d0, bool))
    edge = jnp.asarray(np.tile(np.where(kl > tq, 0.0, NEG).astype(np.float32), (1, N_HEADS)))
    bt = jnp.stack([diag, prev, edge], axis=0) * LOG2E
    per_q = Q_TILE // CMP_STRIDE
    n_rows = n_c + per_q * (n_qt - 1)
    r = np.arange(n_rows)[:, None]
    dc = tq - CMP_STRIDE * (r - per_q * (n_qt - 1)) - (CMP_BLOCK - 1)
    pt = _bucket_lookup(relc, bk[np.clip(dc, 0, None)], dc >= 0) * LOG2E
    n_blk = s_len // SEL_BLOCK
    ratio = SEL_BLOCK // CMP_STRIDE
    jj = np.arange(n_blk)[:, None]
    ii = np.arange(n_c)[None, :]
    imat = ((ii >= ratio * jj - 1) & (ii <= ratio * jj + ratio - 1) & (ii < n_c - 1)).astype(np.float32)
    return bt, pt, jnp.asarray(imat)


def _nsa_attention(nsa3, kc, vct, bt, pt, imat):
    b, s_len, _ = nsa3.shape
    n_qt = s_len // Q_TILE
    n_c = s_len // CMP_STRIDE
    n_blk = s_len // SEL_BLOCK
    n_win = WINDOW // Q_TILE
    hd = HEAD_DIM
    hq = N_HEADS * Q_TILE
    v_rows = hd + 2 * SUBLANES
    n_st = NSA_STREAMS if b % NSA_STREAMS == 0 else 1
    kern = functools.partial(_nsa_kernel_multi, n_top=min(N_SEL, n_blk), n_win=n_win)
    return pl.pallas_call(
        kern,
        grid=(b // n_st, n_qt),
        in_specs=[pl.BlockSpec((n_st, Q_TILE, 256), lambda i, c: (i, c, 0)),
                  pl.BlockSpec((n_st, Q_TILE, 256), lambda i, c: (i, c, 1)),
                  pl.BlockSpec((n_st, Q_TILE, LANES), lambda i, c: (i, c, 7)),
                  pl.BlockSpec((n_st, s_len, LANES), lambda i, c: (i, 0, 4),
                               pipeline_mode=pl.Buffered(1)),
                  pl.BlockSpec((n_st, s_len, LANES), lambda i, c: (i, 0, 5),
                               pipeline_mode=pl.Buffered(1)),
                  pl.BlockSpec((n_st, n_c, hd), lambda i, c: (i, 0, 0)),
                  pl.BlockSpec((n_st, hd, n_c), lambda i, c: (i, 0, 0)),
                  pl.BlockSpec(pt.shape, lambda i, c: (0, 0)),
                  pl.BlockSpec(bt.shape, lambda i, c: (0, 0, 0)),
                  pl.BlockSpec(imat.shape, lambda i, c: (0, 0))],
        out_specs=pl.BlockSpec((n_st, Q_TILE, 256), lambda i, c: (i, c, 0)),
        out_shape=jax.ShapeDtypeStruct((b, s_len, 256), F32),
        scratch_shapes=[pltpu.VMEM((n_st, n_qt, Q_TILE, hd + n_blk), BF16),
                        pltpu.VMEM((n_st, n_qt, v_rows, Q_TILE), BF16),
                        pltpu.VMEM((n_st, n_qt, Q_TILE, hd), BF16),
                        pltpu.VMEM((n_st, n_qt, v_rows, Q_TILE), BF16),
                        pltpu.VMEM((n_st, n_blk, Q_TILE), F32),
                        pltpu.VMEM((n_st, n_blk, Q_TILE), F32),
                        pltpu.VMEM((n_st, n_blk, Q_TILE), F32),
                        pltpu.VMEM((n_st, 1, hq), F32),
                        pltpu.VMEM((n_st, v_rows, hq), F32)],
        compiler_params=_cparams(("arbitrary", "arbitrary"), NSA_VMEM_LIMIT),
        name="nsa_attention",
    )(nsa3, nsa3, nsa3, nsa3, nsa3, kc, vct, pt, bt, imat)


def _gc_kernel(gm_ref, cv_ref, halo_ref, wall_ref, sb_ref, lng_ref, lnb_ref,
               cw_ref, cb_ref, clg_ref, clb_ref, cpw_ref, cpb_ref, ob_ref, oc_ref, xs, *, n_ct):
    c = pl.program_id(0) % n_ct
    t_len = gm_ref.shape[0]
    w = W_GRP
    gm = gm_ref[...]
    u = _gelu_tanh(gm[:, :w])
    v = _layer_norm(_gelu_tanh(gm[:, w:2 * w]), lng_ref[...], lnb_ref[...])
    wall = wall_ref[...]
    ti = lax.broadcasted_iota(jnp.int32, wall.shape, 0)
    si = lax.broadcasted_iota(jnp.int32, wall.shape, 1) % t_len
    wall = jnp.where(si <= ti, wall, 0.0)
    sv = _bdot(wall, _block_diag(v, N_HEADS)) + sb_ref[...]
    ob_ref[...] = u * sv * _silu(gm[:, 2 * w:])
    cv = cv_ref[...]
    hl = halo_ref[...]
    hx = hl[:, :w] * _sigmoid(hl[:, w:2 * w])
    hrows = hl.shape[0]
    xs[0:hrows, :] = jnp.where(c == 0, 0.0, hx)
    xs[hrows:hrows + t_len, :] = cv[:, :w] * _sigmoid(cv[:, w:2 * w])
    cw = cw_ref[...]
    acc = jnp.zeros((t_len, w), F32)
    base = hrows - (CONV_WIDTH - 1)
    for j in range(CONV_WIDTH):
        acc = acc + cw[j:j + 1, :] * xs[base + j:base + j + t_len, :]
    y = _layer_norm(acc + cb_ref[...], clg_ref[...], clb_ref[...])
    y = _bdot(_silu(y), cpw_ref[...]) + cpb_ref[...]
    oc_ref[...] = y * _silu(cv[:, 2 * w:])


def _gmlp_conv(gm, cv, s_len, wall, sb, lng, lnb, cw, cb, clg, clb, cpw, cpb, t_len=128, halo=32):
    n = gm.shape[0]
    n_ct = s_len // t_len
    w = W_GRP
    row = lambda a: a.reshape(1, w)
    full = lambda a: pl.BlockSpec(a.shape, lambda i: (0,) * a.ndim)
    cwp = jnp.concatenate([cw, jnp.zeros((32 - CONV_WIDTH, w), F32)], axis=0)
    args = (wall, sb, row(lng), row(lnb), cwp, row(cb), row(clg), row(clb), cpw.astype(BF16), row(cpb))
    per = t_len // halo
    return pl.pallas_call(
        functools.partial(_gc_kernel, n_ct=n_ct),
        grid=(n // t_len,),
        in_specs=[pl.BlockSpec((t_len, GM_W), lambda i: (i, 0)),
                  pl.BlockSpec((t_len, CV_W), lambda i: (i, 0)),
                  pl.BlockSpec((halo, CV_W), lambda i: (jnp.maximum(i * per - 1, 0), 0))]
                 + [full(a) for a in args],
        out_specs=[pl.BlockSpec((t_len, w), lambda i: (i, 0))] * 2,
        out_shape=[jax.ShapeDtypeStruct((n, w), F32)] * 2,
        scratch_shapes=[pltpu.VMEM((halo + t_len, w), F32)],
        compiler_params=_cparams(("parallel",)),
        name="gmlp_conv",
    )(gm, cv, cv, *args)


def _proj_gc_kernel(x_ref, g_ref, w_ref, wall_ref, sb_ref, lng_ref, lnb_ref, cw_ref, cb_ref, clg_ref, clb_ref,
                    cpw_ref, cpb_ref, o_nsa, o_rw, ob_ref, oc_ref, xs, *, tiles_per_seq, t_len):
    first = (pl.program_id(0) % tiles_per_seq) == 0
    tm = x_ref.shape[0]
    halo = xs.shape[0] - tm
    w = W_GRP
    x = x_ref[...]
    ms = jnp.mean(x * x, axis=-1, keepdims=True)
    h = (x * lax.rsqrt(ms + NORM_EPS) * g_ref[...]).astype(BF16)
    c_gm, c_cv, c_rw = NSA_W, NSA_W + GM_W, NSA_W + GM_W + CV_W
    gm = jnp.dot(h, w_ref[:, c_gm:c_cv], preferred_element_type=F32)
    cv = jnp.dot(h, w_ref[:, c_cv:c_rw], preferred_element_type=F32)

    u = _gelu_tanh(gm[:, :w])
    v = _layer_norm(_gelu_tanh(gm[:, w:2 * w]), lng_ref[...], lnb_ref[...])
    zb = _silu(gm[:, 2 * w:])
    wall = wall_ref[...]
    ti = lax.broadcasted_iota(jnp.int32, wall.shape, 0)
    si = lax.broadcasted_iota(jnp.int32, wall.shape, 1) % t_len
    wall = jnp.where(si <= ti, wall, 0.0).astype(BF16)
    for ch in range(tm // t_len):
        r0, r1 = ch * t_len, (ch + 1) * t_len
        sv = jnp.dot(wall, _block_diag(v[r0:r1], N_HEADS).astype(BF16), preferred_element_type=F32) + sb_ref[...]
        ob_ref[r0:r1, :] = u[r0:r1] * sv * zb[r0:r1]

    o_nsa[...] = jnp.dot(h, w_ref[:, :c_gm], preferred_element_type=F32)

    @pl.when(first)
    def _zero_halo():
        xs[0:halo, :] = jnp.zeros((halo, w), F32)
    xs[halo:halo + tm, :] = cv[:, :w] * _sigmoid(cv[:, w:2 * w])
    cw = cw_ref[...]
    acc = jnp.zeros((tm, w), F32)
    base = halo - (CONV_WIDTH - 1)
    for j in range(CONV_WIDTH):
        acc = acc + cw[j:j + 1, :] * xs[base + j:base + j + tm, :]
    xs[0:halo, :] = xs[tm:tm + halo, :]
    y = _layer_norm(acc + cb_ref[...], clg_ref[...], clb_ref[...])
    y = _bdot(_silu(y), cpw_ref[...]) + cpb_ref[...]
    oc_ref[...] = y * _silu(cv[:, 2 * w:])

    o_rw[...] = jnp.dot(h, w_ref[:, c_rw:], preferred_element_type=F32)


def _in_proj_gmlp_conv(x2, g_pre, w_p, s_len, wall, sb, lng, lnb, cw, cb, clg, clb, cpw, cpb, t_len, tm=256, halo=32):
    n, d = x2.shape
    w = W_GRP
    row = lambda a: a.reshape(1, w)
    full = lambda a: pl.BlockSpec(a.shape, lambda i: (0,) * a.ndim)
    cwp = jnp.concatenate([cw, jnp.zeros((32 - CONV_WIDTH, w), F32)], axis=0)
    args = (wall, sb, row(lng), row(lnb), cwp, row(cb), row(clg), row(clb), cpw.astype(BF16), row(cpb))
    widths = (NSA_W, RW_W, w, w)
    return pl.pallas_call(
        functools.partial(_proj_gc_kernel, tiles_per_seq=s_len // tm, t_len=t_len),
        grid=(n // tm,),
        in_specs=[pl.BlockSpec((tm, d), lambda i: (i, 0)),
                  pl.BlockSpec((1, d), lambda i: (0, 0)),
                  pl.BlockSpec((d, N_PROJ), lambda i: (0, 0))] + [full(a) for a in args],
        out_specs=[pl.BlockSpec((tm, wd), lambda i: (i, 0)) for wd in widths],
        out_shape=[jax.ShapeDtypeStruct((n, wd), F32) for wd in widths],
        scratch_shapes=[pltpu.VMEM((halo + tm, w), F32)],
        compiler_params=_cparams(("arbitrary",)),
        name="in_proj_gmlp_conv",
    )(x2, g_pre.reshape(1, d), w_p, *args)


def _rwkv_kernel(rw_ref, prev_ref, mu_ref, wup_ref, aup_ref, vec_ref, o_ref, st):
    c = pl.program_id(0)
    n_b = rw_ref.shape[0]
    L = rw_ref.shape[1]
    w = W_GRP
    nh = N_HEADS
    hd = HEAD_DIM

    @pl.when(c == 0)
    def _init():
        st[...] = jnp.zeros(st.shape, F32)

    vec = vec_ref[...]
    w0, a0, k_k, k_a, r_k, gn_g, gn_b = [vec[i:i + 1, :] for i in range(7)]
    mu = mu_ref[...]
    rows = lax.broadcasted_iota(jnp.int32, (L, w), 0)
    lane = lax.broadcasted_iota(jnp.int32, (L, w), 1)
    s_of = lane % hd
    ones_bd = jnp.where((lax.broadcasted_iota(jnp.int32, (w, w), 0) // hd)
                        == (lax.broadcasted_iota(jnp.int32, (w, w), 1) // hd), 1.0, 0.0)
    tri = jnp.where(lax.broadcasted_iota(jnp.int32, (L, L), 1) <= lax.broadcasted_iota(jnp.int32, (L, L), 0), 1.0, 0.0)
    bd_mask = (lax.broadcasted_iota(jnp.int32, (w, w), 0) // hd) == (lax.broadcasted_iota(jnp.int32, (w, w), 1) // hd)
    strict = s_of < rows
    incl = s_of <= rows
    eye_all = jnp.where(s_of == rows, 1.0, 0.0)

    def bd(x):
        xb = x.astype(BF16)
        return jnp.where(bd_mask, jnp.concatenate([xb] * nh, axis=0), jnp.zeros((), BF16))

    def mm(a, b):
        return jnp.dot(a.astype(BF16), b.astype(BF16), preferred_element_type=F32)

    def mm_nt(a, b):
        return _nt(a.astype(BF16), b.astype(BF16))

    nb = range(n_b)
    stack = lambda parts: jnp.concatenate(parts, axis=0)
    part = lambda x, b: x[b * L:(b + 1) * L]
    zs = [rw_ref[b] for b in nb]
    zrow = lax.broadcasted_iota(jnp.int32, zs[0].shape, 0)
    n_prev = prev_ref.shape[1]
    xs = []
    for b in nb:
        last = jnp.where(c == 0, 0.0, prev_ref[b, n_prev - 1:n_prev, :])
        zprev = jnp.where(zrow == 0, last, pltpu.roll(zs[b], 1, axis=0))
        xs.append(zs[b] + mu * (zprev - zs[b]))
    xs = stack(xs)
    r = xs[:, 0:w]
    k = xs[:, w:2 * w]
    v = xs[:, 2 * w:3 * w]
    wa = xs[:, 3 * w:3 * w + LANES]
    zd = stack([z[:, 3 * w + LANES:] for z in zs])

    zz = w0 + _dot3(jnp.tanh(wa), wup_ref[...])
    lw = (-math.exp(-0.5)) * _sigmoid(zz)
    a = _sigmoid(a0 + _dot3(wa, aup_ref[...]))
    kkr = k * k_k
    kk = kkr * lax.rsqrt(jnp.maximum(mm(kkr * kkr, ones_bd), 1e-24))
    k2 = k * (1.0 + (a - 1.0) * k_a)
    bb = kk * a

    lw_wide = jnp.concatenate([part(lw, b) for b in nb], axis=1)
    cs_wide = _dot_sel_lhs(tri, lw_wide)
    cs = stack([cs_wide[:, b * w:(b + 1) * w] for b in nb])
    g_t = jnp.exp(cs)
    g_inv = jnp.exp(-cs)
    kq = (kk * jnp.exp(cs - lw)).astype(BF16)
    rq = (r * g_t).astype(BF16)
    kt = k2 * g_inv
    bt = bb * g_inv

    lhs = [stack([part(kq, b), part(rq, b)]) for b in nb]
    ab_b = [_nt(lhs[b], bd(part(bt, b))) for b in nb]
    ab_k = [_nt(lhs[b], bd(part(kt, b))) for b in nb]
    a_b = [jnp.where(strict, x[:L], 0.0) for x in ab_b]
    b_b = [jnp.where(incl, x[L:], 0.0) for x in ab_b]
    ak_bk = [stack([jnp.where(strict, x[:L], 0.0), jnp.where(incl, x[L:], 0.0)]) for x in ab_k]

    npow = a_b
    tinv = [eye_all - x for x in a_b]
    for i in range(int(math.log2(L)) - 1):
        npow = [mm(x, bd(x)) for x in npow]
        tinv = [t + mm(t, bd(x)) for t, x in zip(tinv, npow)]

    s0 = [st[b] for b in nb]
    kh = [_nt(lhs[b], s0[b].astype(BF16)) for b in nb]
    akv = [mm(ak_bk[b], bd(part(v, b))) for b in nb]
    u = [mm(tinv[b], bd(kh[b][:L] + akv[b][:L])) for b in nb]
    y = [kh[b][L:] + akv[b][L:] - mm(b_b[b], bd(u[b])) for b in nb]

    zpad = jnp.zeros((w - 2 * L, w), F32)
    for b in nb:
        vu_t = stack([part(v, b), u[b], zpad]).T
        kb = stack([part(kt, b), -part(bt, b), zpad])
        d = mm(vu_t, kb)
        st[b] = g_t[(b + 1) * L - 1:(b + 1) * L, :] * (s0[b] + jnp.where(bd_mask, d, 0.0))

    y = stack(y)
    y_hi, y_lo = _split2(y)
    mean = (mm(y_hi, ones_bd) + mm(y_lo, ones_bd)) * (1.0 / hd)
    yc = y - mean
    var = mm(yc * yc, ones_bd) * (1.0 / hd)
    yn = yc * lax.rsqrt(var + RWKV_GN_EPS) * gn_g + gn_b
    bonus = mm(r * k2 * r_k, ones_bd) * v
    out = (yn + bonus) * _silu(zd)
    for b in nb:
        o_ref[b] = part(out, b)


def _rwkv(rw3, mu_p, wup_p, aup_p, vec):
    b, s_len, _ = rw3.shape
    L = RW_CHUNK
    w = W_GRP
    full = lambda a: pl.BlockSpec(a.shape, lambda c: (0,) * a.ndim)
    return pl.pallas_call(
        _rwkv_kernel,
        grid=(s_len // L,),
        in_specs=[pl.BlockSpec((b, L, RW_W), lambda c: (0, c, 0)),
                  pl.BlockSpec((b, SUBLANES, RW_W), lambda c: (0, jnp.maximum(c * (L // SUBLANES) - 1, 0), 0)),
                  full(mu_p), full(wup_p), full(aup_p), full(vec)],
        out_specs=pl.BlockSpec((b, L, w), lambda c: (0, c, 0)),
        out_shape=jax.ShapeDtypeStruct((b, s_len, w), F32),
        scratch_shapes=[pltpu.VMEM((b, w, w), F32)],
        compiler_params=_cparams(("arbitrary",)),
        name="rwkv7",
    )(rw3, rw3, mu_p, wup_p, aup_p, vec)


def _out_kernel(ya_ref, yb_ref, yc_ref, yd_ref, x_ref, p_ref, wo_ref, gp_ref, pp_ref, pg_ref, o_ref):
    w = W_GRP
    acc = None
    for i, y in enumerate((ya_ref, yb_ref, yc_ref, yd_ref)):
        t = jnp.dot(y[...].astype(BF16), wo_ref[i * w:(i + 1) * w, :], preferred_element_type=F32)
        acc = t if acc is None else acc + t
    ms = jnp.mean(acc * acc, axis=-1, keepdims=True)
    x1 = x_ref[...] + acc * lax.rsqrt(ms + NORM_EPS) * gp_ref[...]
    gate = _sigmoid(jnp.dot(x1.astype(BF16), pg_ref[...], preferred_element_type=F32))
    pe = jnp.dot(p_ref[...].astype(BF16), pp_ref[...], preferred_element_type=F32)
    o_ref[...] = x1 + gate * pe


def _out_proj(ys, x2, p2, w_out, g_post, ple_proj, ple_gate, tm=256):
    n, d = x2.shape
    w = W_GRP
    full = lambda a: pl.BlockSpec(a.shape, lambda i: (0,) * a.ndim)
    wo = w_out.astype(BF16)
    gp = g_post.reshape(1, d)
    pp = ple_proj.astype(BF16)
    pg = ple_gate.astype(BF16)
    return pl.pallas_call(
        _out_kernel,
        grid=(n // tm,),
        in_specs=[pl.BlockSpec((tm, w), lambda i: (i, 0))] * 4
                 + [pl.BlockSpec((tm, d), lambda i: (i, 0)),
                    pl.BlockSpec((tm, p2.shape[1]), lambda i: (i, 0)),
                    full(wo), full(gp), full(pp), full(pg)],
        out_specs=pl.BlockSpec((tm, d), lambda i: (i, 0)),
        out_shape=jax.ShapeDtypeStruct((n, d), F32),
        compiler_params=_cparams(("parallel",)),
        name="out_proj",
    )(*ys, x2, p2, wo, gp, pp, pg)


def _layer(x2, p2, b, s_len, bt, pt, imat, w_in, w_out, g_pre, g_post, nsa_pos, nsa_w1, nsa_w2,
           sgu_ln_g, sgu_ln_b, sgu_w, sgu_b, conv_w, conv_b, conv_ln_g, conv_ln_b, conv_pw, conv_pw_b,
           rwkv_mu, rwkv_w0, rwkv_w_up, rwkv_a0, rwkv_a_up, rwkv_k_k, rwkv_k_a, rwkv_r_k,
           rwkv_gn_g, rwkv_gn_b, ple_proj, ple_gate):
    n = x2.shape[0]
    w = W_GRP
    w_bf = w_in.astype(BF16)
    w_p = jnp.concatenate([jnp.zeros((w_in.shape[0], b_), BF16) if a_ is None else w_bf[:, a_:b_]
                           for a_, b_ in _SEGMENTS], axis=1)
    t_len = sgu_w.shape[-1]
    wall = jnp.transpose(sgu_w, (1, 0, 2)).reshape(t_len, N_HEADS * t_len)
    sb = jnp.repeat(sgu_b.T, HEAD_DIM, axis=1)
    nsa, rw, yb, yc = _in_proj_gmlp_conv(x2, g_pre, w_p, s_len, wall, sb, sgu_ln_g, sgu_ln_b, conv_w, conv_b,
                                         conv_ln_g, conv_ln_b, conv_pw, conv_pw_b, t_len)

    nsa3 = nsa.reshape(b, s_len, NSA_W)
    kc, vct = _nsa_compress(nsa3, nsa_pos, nsa_w1, nsa_w2)
    ya = _nsa_attention(nsa3, kc, vct, bt, pt, imat).reshape(n, w)

    lora = RWKV_LORA
    mu_p = jnp.concatenate([rwkv_mu, jnp.zeros((RW_W - rwkv_mu.shape[0],), F32)]).reshape(1, RW_W)
    wup_p = jnp.concatenate([rwkv_w_up, jnp.zeros((LANES - lora, w), F32)], axis=0)
    aup_p = jnp.concatenate([jnp.zeros((lora, w), F32), rwkv_a_up, jnp.zeros((LANES - 2 * lora, w), F32)], axis=0)
    vec = jnp.stack([rwkv_w0, rwkv_a0, rwkv_k_k, rwkv_k_a, rwkv_r_k.reshape(w), rwkv_gn_g, rwkv_gn_b,
                     jnp.zeros((w,), F32)], axis=0)
    yd = _rwkv(rw.reshape(b, s_len, RW_W), mu_p, wup_p, aup_p, vec).reshape(n, w)

    return _out_proj((ya, yb, yc, yd), x2, p2, w_out, g_post, ple_proj, ple_gate)


def kernel(x, p, rel_bias, w_in, w_out, g_pre, g_post, nsa_pos, nsa_w1, nsa_w2, sgu_ln_g, sgu_ln_b, sgu_w, sgu_b, conv_w, conv_b, conv_ln_g, conv_ln_b, conv_pw, conv_pw_b, rwkv_mu, rwkv_w0, rwkv_w_up, rwkv_a0, rwkv_a_up, rwkv_k_k, rwkv_k_a, rwkv_r_k, rwkv_gn_g, rwkv_gn_b, ple_proj, ple_gate):
    b, s_len, d = x.shape
    depth = w_in.shape[0]
    bt, pt, imat = _nsa_tables(rel_bias, s_len)
    x2 = x.reshape(b * s_len, d)
    per_layer = (w_in, w_out, g_pre, g_post, nsa_pos, nsa_w1, nsa_w2, sgu_ln_g, sgu_ln_b, sgu_w, sgu_b,
                 conv_w, conv_b, conv_ln_g, conv_ln_b, conv_pw, conv_pw_b, rwkv_mu, rwkv_w0, rwkv_w_up,
                 rwkv_a0, rwkv_a_up, rwkv_k_k, rwkv_k_a, rwkv_r_k, rwkv_gn_g, rwkv_gn_b, ple_proj, ple_gate)
    for i in range(depth):
        x2 = _layer(x2, p[i].reshape(b * s_len, -1), b, s_len, bt, pt, imat, *[a[i] for a in per_layer])
    return x2.reshape(b, s_len, d)
```

```python
import functools
import math

import numpy as np
import jax
import jax.numpy as jnp
from jax import lax
from jax.experimental import pallas as pl
from jax.experimental.pallas import tpu as pltpu

F32 = jnp.float32
BF16 = jnp.bfloat16

W_GRP = 256
HEAD_DIM = 64
N_HEADS = 4
NSA_DK = 64
CMP_STRIDE = 16
CMP_BLOCK = 32
CMP_HIDDEN = 128
SEL_BLOCK = 64
N_SEL = 16
WINDOW = 512
REL_BUCKETS = 32
REL_MAX_EXACT = 16
REL_MAX_DIST = 128
CONV_WIDTH = 31
RWKV_LORA = 32
RWKV_GN_EPS = 64e-5
NORM_EPS = 1e-6
LN_EPS = 1e-5
NEG = -1e30
FORCE = 1e4
LOG2E = math.log2(math.e)

LANES = 128
SUBLANES = 8
Q_TILE = 128
SEL_GROUP = 4
NSA_STREAMS = 4
RW_CHUNK = 64
RW_SUB = 4
VMEM_LIMIT = 48 * 1024 * 1024
NSA_VMEM_LIMIT = 56 * 1024 * 1024

NSA_W, GM_W, CV_W, RW_W = 1024, 768, 768, 1152
N_PROJ = NSA_W + GM_W + CV_W + RW_W


def _proj_segments():
    names = ["q", "kc", "vc", "ks", "vs", "kw", "vw", "g", "za", "u", "v", "zb", "ga", "gb", "zc", "rw", "zd"]
    widths = [256, 64, 64, 64, 64, 64, 64, 12, 256, 256, 256, 256, 256, 256, 256, 832, 256]
    o, off = {}, 0
    for n, w in zip(names, widths):
        o[n] = (off, off + w)
        off += w
    return [o["q"], o["za"], (o["ks"][0], o["vw"][1]), (o["kc"][0], o["vc"][1]), o["g"], (None, 116),
            (o["u"][0], o["zc"][1]), o["rw"], (None, 64), o["zd"]]


_SEGMENTS = _proj_segments()


def _rel_buckets(n):
    d = np.arange(n)
    nf = np.maximum(d, REL_MAX_EXACT).astype(np.float32)
    large = REL_MAX_EXACT + (np.log(nf / np.float32(REL_MAX_EXACT)) / np.float32(math.log(REL_MAX_DIST / REL_MAX_EXACT))
                             * np.float32(REL_BUCKETS - REL_MAX_EXACT)).astype(np.int32)
    large = np.minimum(large, REL_BUCKETS - 1)
    return np.where(d < REL_MAX_EXACT, d, large)


def _bdot(a, b):
    return jnp.dot(a.astype(BF16), b.astype(BF16), preferred_element_type=F32)


def _nt(a, b):
    return lax.dot_general(a, b, (((1,), (1,)), ((), ())), preferred_element_type=F32)


def _split2(a):
    hi = a.astype(BF16)
    lo = (a - hi.astype(F32)).astype(BF16)
    return hi, lo


def _split3(a):
    hi = a.astype(BF16)
    r = a - hi.astype(F32)
    mid = r.astype(BF16)
    lo = (r - mid.astype(F32)).astype(BF16)
    return hi, mid, lo


def _dot3(a, b):
    ah, al = _split2(a)
    bh, bl = _split2(b)
    d = lambda x, y: jnp.dot(x, y, preferred_element_type=F32)
    return d(ah, bh) + (d(ah, bl) + d(al, bh))


def _dot3_nt(a, b):
    ah, al = _split2(a)
    bh, bl = _split2(b)
    return _nt(ah, bh) + (_nt(ah, bl) + _nt(al, bh))


def _dot_sel_lhs(a01, b):
    a = a01.astype(BF16)
    bh, bm, bl = _split3(b)
    d = lambda y: jnp.dot(a, y, preferred_element_type=F32)
    return d(bh) + (d(bm) + d(bl))


def _dot_sel_rhs(a, b01):
    b = b01.astype(BF16)
    ah, am, al = _split3(a)
    d = lambda x: jnp.dot(x, b, preferred_element_type=F32)
    return d(ah) + (d(am) + d(al))


def _sigmoid(x):
    return 1.0 / (1.0 + jnp.exp(-x))


def _silu(x):
    return x * _sigmoid(x)


def _gelu_tanh(x):
    c = math.sqrt(2.0 / math.pi)
    return 0.5 * x * (1.0 + jnp.tanh(c * (x + 0.044715 * (x * x * x))))


def _softplus(x):
    return jnp.maximum(x, 0.0) + jnp.log(1.0 + jnp.exp(-jnp.abs(x)))


def _layer_norm(x, g, b):
    mu = jnp.mean(x, axis=-1, keepdims=True)
    xc = x - mu
    var = jnp.mean(xc * xc, axis=-1, keepdims=True)
    return xc * lax.rsqrt(var + LN_EPS) * g + b


def _block_diag(x, n):
    r, c = x.shape
    t = jnp.concatenate([x] * n, axis=0)
    ri = lax.broadcasted_iota(jnp.int32, t.shape, 0) // r
    ci = lax.broadcasted_iota(jnp.int32, t.shape, 1) // (c // n)
    return jnp.where(ri == ci, t, 0.0)


def _cparams(sem, vmem=VMEM_LIMIT):
    return pltpu.CompilerParams(dimension_semantics=sem, vmem_limit_bytes=vmem)


def _proj_kernel(x_ref, g_ref, w_ref, o_nsa, o_gm, o_cv, o_rw):
    x = x_ref[...]
    ms = jnp.mean(x * x, axis=-1, keepdims=True)
    h = (x * lax.rsqrt(ms + NORM_EPS) * g_ref[...]).astype(BF16)
    off = 0
    for o in (o_nsa, o_gm, o_cv, o_rw):
        wd = o.shape[-1]
        o[...] = jnp.dot(h, w_ref[:, off:off + wd], preferred_element_type=F32)
        off += wd


def _in_proj(x2, g_pre, w_p, tm=256):
    n, d = x2.shape
    widths = (NSA_W, GM_W, CV_W, RW_W)
    return pl.pallas_call(
        _proj_kernel,
        grid=(n // tm,),
        in_specs=[pl.BlockSpec((tm, d), lambda i: (i, 0)),
                  pl.BlockSpec((1, d), lambda i: (0, 0)),
                  pl.BlockSpec((d, N_PROJ), lambda i: (0, 0))],
        out_specs=[pl.BlockSpec((tm, w), lambda i: (i, 0)) for w in widths],
        out_shape=[jax.ShapeDtypeStruct((n, w), F32) for w in widths],
        compiler_params=_cparams(("parallel",)),
        name="in_proj",
    )(x2, g_pre.reshape(1, d), w_p)


def _cmp_kernel(kv_ref, pos_ref, w1_ref, w2_ref, o_ref, ot_ref):
    nc = o_ref.shape[0]
    dk = o_ref.shape[1]
    first = None
    second = None
    for r in range(CMP_STRIDE):
        x = kv_ref[pl.ds(r, nc, stride=CMP_STRIDE), :]
        t = _dot3(x + pos_ref[r], w1_ref[r])
        first = t if first is None else first + t
        t = _dot3(x + pos_ref[CMP_STRIDE + r], w1_ref[CMP_STRIDE + r])
        second = t if second is None else second + t
    hid = first + pltpu.roll(second, nc - 1, axis=0)
    out = _dot3(_silu(hid), w2_ref[...])
    o_ref[...] = out[:, :dk]
    ot_ref[...] = out.T[dk:, :]


def _nsa_compress(nsa3, pos, w1, w2):
    b, s_len, _ = nsa3.shape
    nc = s_len // CMP_STRIDE
    dk = NSA_DK
    zw = jnp.zeros((CMP_BLOCK, dk, CMP_HIDDEN), F32)
    w1r = w1.reshape(2, CMP_BLOCK, dk, CMP_HIDDEN)
    w1bd = jnp.concatenate([jnp.concatenate([w1r[0], zw], axis=2),
                            jnp.concatenate([zw, w1r[1]], axis=2)], axis=1)
    z2 = jnp.zeros((CMP_HIDDEN, dk), F32)
    w2bd = jnp.concatenate([jnp.concatenate([w2[0], z2], axis=1),
                            jnp.concatenate([z2, w2[1]], axis=1)], axis=0)
    posr = jnp.concatenate([pos[0], pos[1]], axis=1).reshape(CMP_BLOCK, 1, 2 * dk)
    full = lambda a: pl.BlockSpec(a.shape, lambda i: (0,) * a.ndim)
    return pl.pallas_call(
        _cmp_kernel,
        grid=(b,),
        in_specs=[pl.BlockSpec((None, s_len, LANES), lambda i: (i, 0, 6)),
                  full(posr), full(w1bd), full(w2bd)],
        out_specs=[pl.BlockSpec((None, nc, dk), lambda i: (i, 0, 0)),
                   pl.BlockSpec((None, dk, nc), lambda i: (i, 0, 0))],
        out_shape=[jax.ShapeDtypeStruct((b, nc, dk), F32),
                   jax.ShapeDtypeStruct((b, dk, nc), F32)],
        compiler_params=_cparams(("parallel",)),
        name="nsa_compress",
    )(nsa3, posr, w1bd, w2bd)


def _nsa_kernel(q_ref, za_ref, g_ref, ks_ref, kw_ref, kc_ref, vct_ref, pt_ref, bt_ref, imat_ref,
                o_ref, ksb, vst, kwb, vwt, madd, m_s, acc_s, *, n_top, n_win):
    c = pl.program_id(1)
    n_qt = pl.num_programs(1)
    n_kt = ksb.shape[0]
    hd = HEAD_DIM
    nh = N_HEADS
    hq = nh * Q_TILE

    n_blk = imat_ref.shape[0]
    per_q = Q_TILE // SEL_BLOCK
    v_rows = vst.shape[1]

    @pl.when(c == 0)
    def _prep():
        key_blk = lax.broadcasted_iota(jnp.int32, (Q_TILE, n_blk), 0) // SEL_BLOCK
        col = lax.broadcasted_iota(jnp.int32, (Q_TILE, n_blk), 1)
        ones_row = jnp.where(lax.broadcasted_iota(jnp.int32, (v_rows - hd, Q_TILE), 0) == 0, 1.0, 0.0)

        def body(i, carry):
            r0 = pl.multiple_of(i * Q_TILE, Q_TILE)
            t = ks_ref[pl.ds(r0, Q_TILE), :]
            onehot = jnp.where(col == key_blk + per_q * i, 1.0, 0.0)
            ksb[i] = jnp.concatenate([t[:, :hd], onehot], axis=1).astype(BF16)
            vst[i] = jnp.concatenate([t.T[hd:, :], ones_row], axis=0).astype(BF16)
            t = kw_ref[pl.ds(r0, Q_TILE), :]
            kwb[i] = t[:, :hd].astype(BF16)
            vwt[i] = jnp.concatenate([t.T[hd:, :], ones_row], axis=0).astype(BF16)
            return carry
        lax.fori_loop(0, n_kt, body, 0)

    q = q_ref[...] * (hd ** -0.5 * LOG2E)
    q_all = jnp.concatenate([q[:, hd * h:hd * (h + 1)] for h in range(nh)], axis=0)
    qb = q_all.astype(BF16)
    g_t = _sigmoid(g_ref[...]).T
    gate = [jnp.concatenate([g_t[3 * h + br:3 * h + br + 1, :] for h in range(nh)], axis=1) for br in range(3)]

    kc = kc_ref[...]
    n_c = kc.shape[0]
    off = pl.multiple_of((n_qt - 1 - c) * (Q_TILE // CMP_STRIDE), SUBLANES)
    bias = pt_ref[pl.ds(off, n_c), :]
    s = _dot3_nt(kc, q_all) + bias
    valid = bias > 0.5 * NEG
    m = jnp.max(s, axis=0, keepdims=True)
    p = jnp.where(valid, jnp.exp2(s - m), 0.0)
    l = jnp.sum(p, axis=0, keepdims=True)
    pn = p * jnp.where(l > 0.0, 1.0 / l, 0.0)
    y_acc = gate[0] * _bdot(vct_ref[...], pn)
    pc = pn[:, 0:Q_TILE]
    for h in range(1, nh):
        pc = pc + pn[:, h * Q_TILE:(h + 1) * Q_TILE]

    tl = lax.broadcasted_iota(jnp.int32, (n_blk, Q_TILE), 1)
    jb = lax.broadcasted_iota(jnp.int32, (n_blk, Q_TILE), 0)
    cur = c * per_q + tl // SEL_BLOCK
    causal = jb <= cur
    need_rank = (c + 1) * per_q > n_top

    @pl.when(jnp.logical_not(need_rank))
    def _all_causal():
        madd[...] = jnp.where(causal, 0.0, NEG)

    @pl.when(need_rank)
    def _rank():
        imp = _dot_sel_lhs(imat_ref[...], pc)
        forced = (jb == 0) | (jb == cur) | (jb == cur - 1)
        val = jnp.where(causal, jnp.where(forced, FORCE, imp), NEG)
        n_r = n_blk // SUBLANES
        blocks = [val[SUBLANES * r:SUBLANES * (r + 1), :] for r in range(n_r)]
        cnts = [jnp.zeros((SUBLANES, Q_TILE), F32) for _ in range(n_r)]
        jrow = lax.broadcasted_iota(jnp.int32, (SUBLANES, Q_TILE), 0)
        for i in range(n_blk):
            vi = jnp.broadcast_to(val[i:i + 1, :], (SUBLANES, Q_TILE))
            for r in range(n_r):
                if SUBLANES * r > i:
                    beats = vi >= blocks[r]
                elif SUBLANES * r + SUBLANES - 1 < i:
                    beats = vi > blocks[r]
                else:
                    ge = jnp.where(vi >= blocks[r], 1.0, 0.0)
                    gt = jnp.where(vi > blocks[r], 1.0, 0.0)
                    cnts[r] = cnts[r] + jnp.where(jrow + SUBLANES * r > i, ge, gt)
                    continue
                cnts[r] = cnts[r] + jnp.where(beats, 1.0, 0.0)
        cnt = jnp.concatenate(cnts, axis=0)
        madd[...] = jnp.where(causal & (cnt < float(n_top)), 0.0, NEG)

    def reset_state():
        m_s[...] = jnp.full(m_s.shape, NEG, F32)
        acc_s[...] = jnp.zeros(acc_s.shape, F32)

    def update(ss, vts):
        m_old = m_s[...]
        m_new = m_old
        for s_i in ss:
            m_new = jnp.maximum(m_new, jnp.max(s_i, axis=0, keepdims=True))
        pv = None
        for s_i, v_i in zip(ss, vts):
            t = jnp.dot(v_i, jnp.exp2(s_i - m_new).astype(BF16), preferred_element_type=F32)
            pv = t if pv is None else pv + t
        acc_s[...] = jnp.exp2(m_old - m_new) * acc_s[...] + pv
        m_s[...] = m_new

    def branch_out(g):
        acc = acc_s[...]
        return (g / acc[hd:hd + 1, :]) * acc[:hd, :]

    pad = jnp.zeros((LANES - n_blk, Q_TILE), F32)
    m_t = jnp.concatenate([madd[...], pad], axis=0).T[:, :n_blk]
    blk_lane = lax.broadcasted_iota(jnp.int32, m_t.shape, 1)
    m_far = jnp.where(blk_lane < per_q * (c - 1), m_t, NEG)
    q_near = jnp.concatenate([q_all, jnp.concatenate([m_t] * nh, axis=0)], axis=1).astype(BF16)
    q_far = jnp.concatenate([q_all, jnp.concatenate([m_far] * nh, axis=0)], axis=1).astype(BF16)

    reset_state()
    kt1 = jnp.maximum(c - 1, 0)
    s0 = _nt(ksb[c], q_near) + bt_ref[0]
    s1 = _nt(ksb[kt1], q_near) + bt_ref[1] + jnp.where(c >= 1, 0.0, NEG)
    update([s0, s1], [vst[c], vst[kt1]])

    n_far = jnp.maximum(c - 1, 0)

    def far_body(g, carry):
        kts = [g * SEL_GROUP + j for j in range(SEL_GROUP)]
        update([_nt(ksb[kt], q_far) for kt in kts], [vst[kt] for kt in kts])
        return carry
    lax.fori_loop(0, (n_far + SEL_GROUP - 1) // SEL_GROUP, far_body, 0)
    y_acc = y_acc + branch_out(gate[1])

    reset_state()
    ss, vts = [], []
    for i in range(n_win + 1):
        kt = jnp.maximum(c - i, 0)
        s_i = _nt(kwb[kt], qb)
        if i == 0:
            s_i = s_i + bt_ref[0]
        else:
            if i == 1:
                s_i = s_i + bt_ref[1]
            elif i == n_win:
                s_i = s_i + bt_ref[2]
            s_i = s_i + jnp.where(c >= i, 0.0, NEG)
        ss.append(s_i)
        vts.append(vwt[kt])
    update(ss, vts)
    y_acc = y_acc + branch_out(gate[2])

    y_t = jnp.concatenate([y_acc[:, h * Q_TILE:(h + 1) * Q_TILE] for h in range(nh)], axis=0)
    o_ref[...] = y_t.T * _silu(za_ref[...])


def _nsa_kernel_multi(q_ref, za_ref, g_ref, ks_ref, kw_ref, kc_ref, vct_ref, pt_ref, bt_ref, imat_ref,
                      o_ref, ksb, vst, kwb, vwt, madd, val_s, cnt_s, pc_s, yc_s, m_s, acc_s, *, n_top, n_win):
    c = pl.program_id(1)
    n_qt = pl.num_programs(1)
    n_st = q_ref.shape[0]
    n_kt = ksb.shape[1]
    hd = HEAD_DIM
    nh = N_HEADS
    n_blk = imat_ref.shape[0]
    per_q = Q_TILE // SEL_BLOCK
    v_rows = vst.shape[2]
    streams = range(n_st)

    @pl.when(c == 0)
    def _prep():
        key_blk = lax.broadcasted_iota(jnp.int32, (Q_TILE, n_blk), 0) // SEL_BLOCK
        col = lax.broadcasted_iota(jnp.int32, (Q_TILE, n_blk), 1)
        ones_row = jnp.where(lax.broadcasted_iota(jnp.int32, (v_rows - hd, Q_TILE), 0) == 0, 1.0, 0.0)

        def body(i, carry):
            r0 = pl.multiple_of(i * Q_TILE, Q_TILE)
            onehot = jnp.where(col == key_blk + per_q * i, 1.0, 0.0)
            for s in streams:
                t = ks_ref[s, pl.ds(r0, Q_TILE), :]
                ksb[s, i] = jnp.concatenate([t[:, :hd], onehot], axis=1).astype(BF16)
                vst[s, i] = jnp.concatenate([t.T[hd:, :], ones_row], axis=0).astype(BF16)
                t = kw_ref[s, pl.ds(r0, Q_TILE), :]
                kwb[s, i] = t[:, :hd].astype(BF16)
                vwt[s, i] = jnp.concatenate([t.T[hd:, :], ones_row], axis=0).astype(BF16)
            return carry
        lax.fori_loop(0, n_kt, body, 0)

    def reset_state():
        m_s[...] = jnp.full(m_s.shape, NEG, F32)
        acc_s[...] = jnp.zeros(acc_s.shape, F32)

    def update(groups, between=None):
        m_old = [m_s[s] for s in streams]
        m_new = []
        for s in streams:
            m = m_old[s]
            for s_i in groups[s][0]:
                m = jnp.maximum(m, jnp.max(s_i, axis=0, keepdims=True))
            m_new.append(m)
        for s in streams:
            ss, vs = groups[s]
            pv = None
            for j in range(0, len(ss), 2):
                if between is not None:
                    between(s, j)
                p_j = jnp.concatenate([jnp.exp2(s_i - m_new[s]).astype(BF16) for s_i in ss[j:j + 2]], axis=0)
                v_j = jnp.concatenate(vs[j:j + 2], axis=1)
                t = jnp.dot(v_j, p_j, preferred_element_type=F32)
                pv = t if pv is None else pv + t
            acc_s[s] = jnp.exp2(m_old[s] - m_new[s]) * acc_s[s] + pv
            m_s[s] = m_new[s]

    def branch_out(s, g):
        acc = acc_s[s]
        return (g / acc[hd:hd + 1, :]) * acc[:hd, :]

    q_all, qb, gate = [], [], []
    for s in streams:
        q = q_ref[s] * (hd ** -0.5 * LOG2E)
        qa = jnp.concatenate([q[:, hd * h:hd * (h + 1)] for h in range(nh)], axis=0)
        q_all.append(qa)
        qb.append(qa.astype(BF16))
        g_t = _sigmoid(g_ref[s]).T
        gate.append([jnp.concatenate([g_t[3 * h + br:3 * h + br + 1, :] for h in range(nh)], axis=1)
                     for br in range(3)])

    reset_state()
    groups = []
    for s in streams:
        ss, vts = [], []
        for i in range(n_win + 1):
            kt = jnp.maximum(c - i, 0)
            s_i = _nt(kwb[s, kt], qb[s])
            if i == 0:
                s_i = s_i + bt_ref[0]
            else:
                if i == 1:
                    s_i = s_i + bt_ref[1]
                elif i == n_win:
                    s_i = s_i + bt_ref[2]
                s_i = s_i + jnp.where(c >= i, 0.0, NEG)
            ss.append(s_i)
            vts.append(vwt[s, kt])
        groups.append((ss, vts))
    update(groups)
    y_acc = [branch_out(s, gate[s][2]) for s in streams]

    n_c = kc_ref.shape[1]
    per_tile = Q_TILE // CMP_STRIDE
    off = pl.multiple_of((n_qt - 1 - c) * per_tile, SUBLANES)

    def cmp_branch(rows):
        bias = pt_ref[pl.ds(off, rows), :]
        valid = bias > 0.5 * NEG
        scs = [_dot3_nt(kc_ref[s, 0:rows, :], q_all[s]) + bias for s in streams]
        ps = [jnp.where(valid, jnp.exp2(sc - jnp.max(sc, axis=0, keepdims=True)), 0.0) for sc in scs]
        ls = [jnp.sum(p, axis=0, keepdims=True) for p in ps]
        pns = [p * jnp.where(l > 0.0, 1.0 / l, 0.0) for p, l in zip(ps, ls)]
        for s in streams:
            yc_s[s] = gate[s][0] * _bdot(vct_ref[s, :, 0:rows], pns[s])
            pc = pns[s][:, 0:Q_TILE]
            for h in range(1, nh):
                pc = pc + pns[s][:, h * Q_TILE:(h + 1) * Q_TILE]
            pc_s[s, 0:rows, :] = pc
            if rows < n_c:
                pc_s[s, rows:n_c, :] = jnp.zeros((n_c - rows, Q_TILE), F32)

    half = n_c // 2
    few = (c + 1) * per_tile <= half
    pl.when(few)(lambda: cmp_branch(half))
    pl.when(jnp.logical_not(few))(lambda: cmp_branch(n_c))

    tl = lax.broadcasted_iota(jnp.int32, (n_blk, Q_TILE), 1)
    jb = lax.broadcasted_iota(jnp.int32, (n_blk, Q_TILE), 0)
    cur = c * per_q + tl // SEL_BLOCK
    causal = jb <= cur
    need_rank = (c + 1) * per_q > n_top

    @pl.when(jnp.logical_not(need_rank))
    def _all_causal():
        for s in streams:
            madd[s] = jnp.where(causal, 0.0, NEG)

    @pl.when(need_rank)
    def _rank():
        forced = (jb == 0) | (jb == cur) | (jb == cur - 1)
        n_r = n_blk // SUBLANES
        jrow = lax.broadcasted_iota(jnp.int32, (SUBLANES, Q_TILE), 0)
        for s in streams:
            imp = _dot_sel_lhs(imat_ref[...], pc_s[s])
            val_s[s] = jnp.where(causal, jnp.where(forced, FORCE, imp), NEG)
            cnt_s[s] = jnp.zeros((n_blk, Q_TILE), F32)
        for ci in range(n_r):
            @pl.when(SUBLANES * ci < (c + 1) * per_q)
            def _chunk():
                for s in streams:
                    val = val_s[s]
                    blocks = [val[SUBLANES * r:SUBLANES * (r + 1), :] for r in range(n_r)]
                    cnts = [None] * n_r
                    for i in range(SUBLANES * ci, SUBLANES * (ci + 1)):
                        vi = jnp.broadcast_to(val[i:i + 1, :], (SUBLANES, Q_TILE))
                        for r in range(n_r):
                            if r > ci:
                                t = jnp.where(vi >= blocks[r], 1.0, 0.0)
                            elif r < ci:
                                t = jnp.where(vi > blocks[r], 1.0, 0.0)
                            else:
                                ge = jnp.where(vi >= blocks[r], 1.0, 0.0)
                                gt = jnp.where(vi > blocks[r], 1.0, 0.0)
                                t = jnp.where(jrow + SUBLANES * r > i, ge, gt)
                            cnts[r] = t if cnts[r] is None else cnts[r] + t
                    cnt_s[s] = cnt_s[s] + jnp.concatenate(cnts, axis=0)
        for s in streams:
            madd[s] = jnp.where(causal & (cnt_s[s] < float(n_top)), 0.0, NEG)

    reset_state()
    pad = jnp.zeros((LANES - n_blk, Q_TILE), F32)
    kt1 = jnp.maximum(c - 1, 0)
    q_far, groups = [], []
    for s in streams:
        m_t = jnp.concatenate([madd[s], pad], axis=0).T[:, :n_blk]
        blk_lane = lax.broadcasted_iota(jnp.int32, m_t.shape, 1)
        m_far = jnp.where(blk_lane < per_q * (c - 1), m_t, NEG)
        q_near = jnp.concatenate([q_all[s], jnp.concatenate([m_t] * nh, axis=0)], axis=1).astype(BF16)
        q_far.append(jnp.concatenate([q_all[s], jnp.concatenate([m_far] * nh, axis=0)], axis=1).astype(BF16))
        s0 = _nt(ksb[s, c], q_near) + bt_ref[0]
        s1 = _nt(ksb[s, kt1], q_near) + bt_ref[1] + jnp.where(c >= 1, 0.0, NEG)
        groups.append(([s0, s1], [vst[s, c], vst[s, kt1]]))

    update(groups)

    n_far = jnp.maximum(c - 1, 0)

    def far_body(g, carry):
        kts = [g * SEL_GROUP + j for j in range(SEL_GROUP)]
        update([([_nt(ksb[s, kt], q_far[s]) for kt in kts], [vst[s, kt] for kt in kts]) for s in streams])
        return carry
    lax.fori_loop(0, (n_far + SEL_GROUP - 1) // SEL_GROUP, far_body, 0)

    for s in streams:
        y = y_acc[s] + yc_s[s] + branch_out(s, gate[s][1])
        y_t = jnp.concatenate([y[:, h * Q_TILE:(h + 1) * Q_TILE] for h in range(nh)], axis=0)
        o_ref[s] = y_t.T * _silu(za_ref[s])


def _bucket_lookup(relc, idx, visible):
    r, cc = idx.shape
    onehot = (jnp.asarray(idx.reshape(1, -1)) == jnp.arange(relc.shape[0])[:, None]).astype(F32)
    vals = jnp.einsum("bh,bn->hn", relc, onehot, precision=lax.Precision.HIGHEST).reshape(-1, r, cc)
    vals = jnp.where(jnp.asarray(visible)[None], vals, NEG)
    return jnp.transpose(vals, (1, 0, 2)).reshape(r, -1).astype(F32)


def _nsa_tables(rel_bias, s_len):
    n_qt = s_len // Q_TILE
    n_c = s_len // CMP_STRIDE
    relc = rel_bias - rel_bias[REL_BUCKETS - 1][None, :]
    bk = _rel_buckets(s_len + Q_TILE)
    kl = np.arange(Q_TILE)[:, None]
    tq = np.arange(Q_TILE)[None, :]
    d0 = tq - kl
    diag = _bucket_lookup(relc, bk[np.clip(d0, 0, None)], d0 >= 0)
    prev = _bucket_lookup(relc, bk[Q_TILE + d0], np.ones_like(d0, bool))
    edge = jnp.asarray(np.tile(np.where(kl > tq, 0.0, NEG).astype(np.float32), (1, N_HEADS)))
    bt = jnp.stack([diag, prev, edge], axis=0) * LOG2E
    per_q = Q_TILE // CMP_STRIDE
    n_rows = n_c + per_q * (n_qt - 1)
    r = np.arange(n_rows)[:, None]
    dc = tq - CMP_STRIDE * (r - per_q * (n_qt - 1)) - (CMP_BLOCK - 1)
    pt = _bucket_lookup(relc, bk[np.clip(dc, 0, None)], dc >= 0) * LOG2E
    n_blk = s_len // SEL_BLOCK
    ratio = SEL_BLOCK // CMP_STRIDE
    jj = np.arange(n_blk)[:, None]
    ii = np.arange(n_c)[None, :]
    imat = ((ii >= ratio * jj - 1) & (ii <= ratio * jj + ratio - 1) & (ii < n_c - 1)).astype(np.float32)
    return bt, pt, jnp.asarray(imat)


def _nsa_attention(nsa3, kc, vct, bt, pt, imat):
    b, s_len, _ = nsa3.shape
    n_qt = s_len // Q_TILE
    n_c = s_len // CMP_STRIDE
    n_blk = s_len // SEL_BLOCK
    n_win = WINDOW // Q_TILE
    hd = HEAD_DIM
    hq = N_HEADS * Q_TILE
    v_rows = hd + 2 * SUBLANES
    n_st = NSA_STREAMS if b % NSA_STREAMS == 0 else 1
    kern = functools.partial(_nsa_kernel_multi, n_top=min(N_SEL, n_blk), n_win=n_win)
    return pl.pallas_call(
        kern,
        grid=(b // n_st, n_qt),
        in_specs=[pl.BlockSpec((n_st, Q_TILE, 256), lambda i, c: (i, c, 0)),
                  pl.BlockSpec((n_st, Q_TILE, 256), lambda i, c: (i, c, 1)),
                  pl.BlockSpec((n_st, Q_TILE, LANES), lambda i, c: (i, c, 7)),
                  pl.BlockSpec((n_st, s_len, LANES), lambda i, c: (i, 0, 4),
                               pipeline_mode=pl.Buffered(1)),
                  pl.BlockSpec((n_st, s_len, LANES), lambda i, c: (i, 0, 5),
                               pipeline_mode=pl.Buffered(1)),
                  pl.BlockSpec((n_st, n_c, hd), lambda i, c: (i, 0, 0)),
                  pl.BlockSpec((n_st, hd, n_c), lambda i, c: (i, 0, 0)),
                  pl.BlockSpec(pt.shape, lambda i, c: (0, 0)),
                  pl.BlockSpec(bt.shape, lambda i, c: (0, 0, 0)),
                  pl.BlockSpec(imat.shape, lambda i, c: (0, 0))],
        out_specs=pl.BlockSpec((n_st, Q_TILE, 256), lambda i, c: (i, c, 0)),
        out_shape=jax.ShapeDtypeStruct((b, s_len, 256), F32),
        scratch_shapes=[pltpu.VMEM((n_st, n_qt, Q_TILE, hd + n_blk), BF16),
                        pltpu.VMEM((n_st, n_qt, v_rows, Q_TILE), BF16),
                        pltpu.VMEM((n_st, n_qt, Q_TILE, hd), BF16),
                        pltpu.VMEM((n_st, n_qt, v_rows, Q_TILE), BF16),
                        pltpu.VMEM((n_st, n_blk, Q_TILE), F32),
                        pltpu.VMEM((n_st, n_blk, Q_TILE), F32),
                        pltpu.VMEM((n_st, n_blk, Q_TILE), F32),
                        pltpu.VMEM((n_st, n_c, Q_TILE), F32),
                        pltpu.VMEM((n_st, hd, hq), F32),
                        pltpu.VMEM((n_st, 1, hq), F32),
                        pltpu.VMEM((n_st, v_rows, hq), F32)],
        compiler_params=_cparams(("arbitrary", "arbitrary"), NSA_VMEM_LIMIT),
        name="nsa_attention",
    )(nsa3, nsa3, nsa3, nsa3, nsa3, kc, vct, pt, bt, imat)


def _gc_kernel(gm_ref, cv_ref, halo_ref, wall_ref, sb_ref, lng_ref, lnb_ref,
               cw_ref, cb_ref, clg_ref, clb_ref, cpw_ref, cpb_ref, ob_ref, oc_ref, xs, *, n_ct):
    c = pl.program_id(0) % n_ct
    t_len = gm_ref.shape[0]
    w = W_GRP
    gm = gm_ref[...]
    u = _gelu_tanh(gm[:, :w])
    v = _layer_norm(_gelu_tanh(gm[:, w:2 * w]), lng_ref[...], lnb_ref[...])
    wall = wall_ref[...]
    ti = lax.broadcasted_iota(jnp.int32, wall.shape, 0)
    si = lax.broadcasted_iota(jnp.int32, wall.shape, 1) % t_len
    wall = jnp.where(si <= ti, wall, 0.0)
    sv = _bdot(wall, _block_diag(v, N_HEADS)) + sb_ref[...]
    ob_ref[...] = u * sv * _silu(gm[:, 2 * w:])
    cv = cv_ref[...]
    hl = halo_ref[...]
    hx = hl[:, :w] * _sigmoid(hl[:, w:2 * w])
    hrows = hl.shape[0]
    xs[0:hrows, :] = jnp.where(c == 0, 0.0, hx)
    xs[hrows:hrows + t_len, :] = cv[:, :w] * _sigmoid(cv[:, w:2 * w])
    cw = cw_ref[...]
    acc = jnp.zeros((t_len, w), F32)
    base = hrows - (CONV_WIDTH - 1)
    for j in range(CONV_WIDTH):
        acc = acc + cw[j:j + 1, :] * xs[base + j:base + j + t_len, :]
    y = _layer_norm(acc + cb_ref[...], clg_ref[...], clb_ref[...])
    y = _bdot(_silu(y), cpw_ref[...]) + cpb_ref[...]
    oc_ref[...] = y * _silu(cv[:, 2 * w:])


def _gmlp_conv(gm, cv, s_len, wall, sb, lng, lnb, cw, cb, clg, clb, cpw, cpb, t_len=128, halo=32):
    n = gm.shape[0]
    n_ct = s_len // t_len
    w = W_GRP
    row = lambda a: a.reshape(1, w)
    full = lambda a: pl.BlockSpec(a.shape, lambda i: (0,) * a.ndim)
    cwp = jnp.concatenate([cw, jnp.zeros((32 - CONV_WIDTH, w), F32)], axis=0)
    args = (wall, sb, row(lng), row(lnb), cwp, row(cb), row(clg), row(clb), cpw.astype(BF16), row(cpb))
    per = t_len // halo
    return pl.pallas_call(
        functools.partial(_gc_kernel, n_ct=n_ct),
        grid=(n // t_len,),
        in_specs=[pl.BlockSpec((t_len, GM_W), lambda i: (i, 0)),
                  pl.BlockSpec((t_len, CV_W), lambda i: (i, 0)),
                  pl.BlockSpec((halo, CV_W), lambda i: (jnp.maximum(i * per - 1, 0), 0))]
                 + [full(a) for a in args],
        out_specs=[pl.BlockSpec((t_len, w), lambda i: (i, 0))] * 2,
        out_shape=[jax.ShapeDtypeStruct((n, w), F32)] * 2,
        scratch_shapes=[pltpu.VMEM((halo + t_len, w), F32)],
        compiler_params=_cparams(("parallel",)),
        name="gmlp_conv",
    )(gm, cv, cv, *args)


def _proj_gc_kernel(x_ref, g_ref, w_ref, wall_ref, sb_ref, lng_ref, lnb_ref, cw_ref, cb_ref, clg_ref, clb_ref,
                    cpw_ref, cpb_ref, o_nsa, o_rw, ob_ref, oc_ref, xs, *, tiles_per_seq, t_len):
    first = (pl.program_id(0) % tiles_per_seq) == 0
    tm = x_ref.shape[0]
    halo = xs.shape[0] - tm
    w = W_GRP
    x = x_ref[...]
    ms = jnp.mean(x * x, axis=-1, keepdims=True)
    h = (x * lax.rsqrt(ms + NORM_EPS) * g_ref[...]).astype(BF16)
    c_gm, c_cv, c_rw = NSA_W, NSA_W + GM_W, NSA_W + GM_W + CV_W
    gm = jnp.dot(h, w_ref[:, c_gm:c_cv], preferred_element_type=F32)
    cv = jnp.dot(h, w_ref[:, c_cv:c_rw], preferred_element_type=F32)

    u = _gelu_tanh(gm[:, :w])
    v = _layer_norm(_gelu_tanh(gm[:, w:2 * w]), lng_ref[...], lnb_ref[...])
    zb = _silu(gm[:, 2 * w:])
    wall = wall_ref[...]
    ti = lax.broadcasted_iota(jnp.int32, wall.shape, 0)
    si = lax.broadcasted_iota(jnp.int32, wall.shape, 1) % t_len
    wall = jnp.where(si <= ti, wall, 0.0).astype(BF16)
    for ch in range(tm // t_len):
        r0, r1 = ch * t_len, (ch + 1) * t_len
        sv = jnp.dot(wall, _block_diag(v[r0:r1], N_HEADS).astype(BF16), preferred_element_type=F32) + sb_ref[...]
        ob_ref[r0:r1, :] = u[r0:r1] * sv * zb[r0:r1]

    o_nsa[...] = jnp.dot(h, w_ref[:, :c_gm], preferred_element_type=F32)

    @pl.when(first)
    def _zero_halo():
        xs[0:halo, :] = jnp.zeros((halo, w), F32)
    xs[halo:halo + tm, :] = cv[:, :w] * _sigmoid(cv[:, w:2 * w])
    cw = cw_ref[...]
    acc = jnp.zeros((tm, w), F32)
    base = halo - (CONV_WIDTH - 1)
    for j in range(CONV_WIDTH):
        acc = acc + cw[j:j + 1, :] * xs[base + j:base + j + tm, :]
    xs[0:halo, :] = xs[tm:tm + halo, :]
    y = _layer_norm(acc + cb_ref[...], clg_ref[...], clb_ref[...])
    y = _bdot(_silu(y), cpw_ref[...]) + cpb_ref[...]
    oc_ref[...] = y * _silu(cv[:, 2 * w:])

    o_rw[...] = jnp.dot(h, w_ref[:, c_rw:], preferred_element_type=F32)


def _in_proj_gmlp_conv(x2, g_pre, w_p_all, layer, s_len, wall, sb, lng, lnb, cw, cb, clg, clb, cpw, cpb, t_len, tm=256, halo=32):
    n, d = x2.shape
    w = W_GRP
    row = lambda a: a.reshape(1, w)
    full = lambda a: pl.BlockSpec(a.shape, lambda i: (0,) * a.ndim)
    cwp = jnp.concatenate([cw, jnp.zeros((32 - CONV_WIDTH, w), F32)], axis=0)
    args = (wall, sb, row(lng), row(lnb), cwp, row(cb), row(clg), row(clb), cpw.astype(BF16), row(cpb))
    widths = (NSA_W, RW_W, w, w)
    return pl.pallas_call(
        functools.partial(_proj_gc_kernel, tiles_per_seq=s_len // tm, t_len=t_len),
        grid=(n // tm,),
        in_specs=[pl.BlockSpec((tm, d), lambda i: (i, 0)),
                  pl.BlockSpec((1, d), lambda i: (0, 0)),
                  pl.BlockSpec((None, d, N_PROJ), lambda i: (layer, 0, 0))] + [full(a) for a in args],
        out_specs=[pl.BlockSpec((tm, wd), lambda i: (i, 0)) for wd in widths],
        out_shape=[jax.ShapeDtypeStruct((n, wd), F32) for wd in widths],
        scratch_shapes=[pltpu.VMEM((halo + tm, w), F32)],
        compiler_params=_cparams(("arbitrary",)),
        name="in_proj_gmlp_conv",
    )(x2, g_pre.reshape(1, d), w_p_all, *args)


def _rwkv_kernel(rw_ref, prev_ref, mu_ref, wup_ref, aup_ref, vec_ref, o_ref, st):
    c = pl.program_id(0)
    n_b = rw_ref.shape[0]
    L = RW_CHUNK
    n_sub = rw_ref.shape[1] // L
    w = W_GRP
    nh = N_HEADS
    hd = HEAD_DIM

    @pl.when(c == 0)
    def _init():
        st[...] = jnp.zeros(st.shape, F32)

    vec = vec_ref[...]
    w0, a0, k_k, k_a, r_k, gn_g, gn_b = [vec[i:i + 1, :] for i in range(7)]
    mu = mu_ref[...]
    rows = lax.broadcasted_iota(jnp.int32, (L, w), 0)
    lane = lax.broadcasted_iota(jnp.int32, (L, w), 1)
    s_of = lane % hd
    ones_bd = jnp.where((lax.broadcasted_iota(jnp.int32, (w, w), 0) // hd)
                        == (lax.broadcasted_iota(jnp.int32, (w, w), 1) // hd), 1.0, 0.0)
    tri = jnp.where(lax.broadcasted_iota(jnp.int32, (L, L), 1) <= lax.broadcasted_iota(jnp.int32, (L, L), 0), 1.0, 0.0)
    bd_mask = (lax.broadcasted_iota(jnp.int32, (w, w), 0) // hd) == (lax.broadcasted_iota(jnp.int32, (w, w), 1) // hd)
    strict = s_of < rows
    incl = s_of <= rows
    eye_all = jnp.where(s_of == rows, 1.0, 0.0)

    def bd(x):
        xb = x.astype(BF16)
        return jnp.where(bd_mask, jnp.concatenate([xb] * nh, axis=0), jnp.zeros((), BF16))

    def mm(a, b):
        return jnp.dot(a.astype(BF16), b.astype(BF16), preferred_element_type=F32)

    def mm_nt(a, b):
        return _nt(a.astype(BF16), b.astype(BF16))

    nb = range(n_b)
    probs = range(n_b * n_sub)
    stack = lambda parts: jnp.concatenate(parts, axis=0)
    part = lambda x, p: x[p * L:(p + 1) * L]
    zs = [rw_ref[b] for b in nb]
    zrow = lax.broadcasted_iota(jnp.int32, zs[0].shape, 0)
    n_prev = prev_ref.shape[1]
    xs = []
    for b in nb:
        last = jnp.where(c == 0, 0.0, prev_ref[b, n_prev - 1:n_prev, :])
        zprev = jnp.where(zrow == 0, last, pltpu.roll(zs[b], 1, axis=0))
        xs.append(zs[b] + mu * (zprev - zs[b]))
    xs = stack(xs)
    r = xs[:, 0:w]
    k = xs[:, w:2 * w]
    v = xs[:, 2 * w:3 * w]
    wa = xs[:, 3 * w:3 * w + LANES]
    zd = stack([z[:, 3 * w + LANES:] for z in zs])

    zz = w0 + _dot3(jnp.tanh(wa), wup_ref[...])
    lw = (-math.exp(-0.5)) * _sigmoid(zz)
    a = _sigmoid(a0 + _dot3(wa, aup_ref[...]))
    kkr = k * k_k
    kk = kkr * lax.rsqrt(jnp.maximum(mm(kkr * kkr, ones_bd), 1e-24))
    k2 = k * (1.0 + (a - 1.0) * k_a)
    bb = kk * a

    lw_wide = jnp.concatenate([part(lw, p) for p in probs], axis=1)
    cs_wide = _dot_sel_lhs(tri, lw_wide)
    cs = stack([cs_wide[:, p * w:(p + 1) * w] for p in probs])
    g_t = jnp.exp(cs)
    g_inv = jnp.exp(-cs)
    kq = (kk * jnp.exp(cs - lw)).astype(BF16)
    rq = (r * g_t).astype(BF16)
    kt = k2 * g_inv
    bt = bb * g_inv

    lhs = [stack([part(kq, p), part(rq, p)]) for p in probs]
    ab_b = [_nt(lhs[p], bd(part(bt, p))) for p in probs]
    ab_k = [_nt(lhs[p], bd(part(kt, p))) for p in probs]
    a_b = [jnp.where(strict, x[:L], 0.0) for x in ab_b]
    b_b = [jnp.where(incl, x[L:], 0.0) for x in ab_b]
    ak_bk = [stack([jnp.where(strict, x[:L], 0.0), jnp.where(incl, x[L:], 0.0)]) for x in ab_k]
    akv = [mm(ak_bk[p], bd(part(v, p))) for p in probs]

    npow = a_b
    tinv = [eye_all - x for x in a_b]
    for i in range(int(math.log2(L)) - 1):
        npow = [mm(x, bd(x)) for x in npow]
        tinv = [t + mm(t, bd(x)) for t, x in zip(tinv, npow)]

    zpad = jnp.zeros((w - 2 * L, w), F32)
    y = [None] * len(probs)
    for sub in range(n_sub):
        ps = [b * n_sub + sub for b in nb]
        s0 = [st[b] for b in nb]
        kh = [_nt(lhs[p], s0[b].astype(BF16)) for b, p in zip(nb, ps)]
        u = [mm(tinv[p], bd(kh[b][:L] + akv[p][:L])) for b, p in zip(nb, ps)]
        for b, p in zip(nb, ps):
            y[p] = kh[b][L:] + akv[p][L:] - mm(b_b[p], bd(u[b]))
        for b, p in zip(nb, ps):
            vu_t = stack([part(v, p), u[b], zpad]).T
            kb = stack([part(kt, p), -part(bt, p), zpad])
            d = mm(vu_t, kb)
            st[b] = g_t[(p + 1) * L - 1:(p + 1) * L, :] * (s0[b] + jnp.where(bd_mask, d, 0.0))

    y = stack(y)
    y_hi, y_lo = _split2(y)
    mean = (mm(y_hi, ones_bd) + mm(y_lo, ones_bd)) * (1.0 / hd)
    yc = y - mean
    var = mm(yc * yc, ones_bd) * (1.0 / hd)
    yn = yc * lax.rsqrt(var + RWKV_GN_EPS) * gn_g + gn_b
    bonus = mm(r * k2 * r_k, ones_bd) * v
    out = (yn + bonus) * _silu(zd)
    for b in nb:
        o_ref[b] = out[b * n_sub * L:(b + 1) * n_sub * L]


def _rwkv(rw3, mu_p, wup_p, aup_p, vec):
    b, s_len, _ = rw3.shape
    rows = RW_CHUNK * RW_SUB
    w = W_GRP
    full = lambda a: pl.BlockSpec(a.shape, lambda c: (0,) * a.ndim)
    return pl.pallas_call(
        _rwkv_kernel,
        grid=(s_len // rows,),
        in_specs=[pl.BlockSpec((b, rows, RW_W), lambda c: (0, c, 0)),
                  pl.BlockSpec((b, SUBLANES, RW_W), lambda c: (0, jnp.maximum(c * (rows // SUBLANES) - 1, 0), 0)),
                  full(mu_p), full(wup_p), full(aup_p), full(vec)],
        out_specs=pl.BlockSpec((b, rows, w), lambda c: (0, c, 0)),
        out_shape=jax.ShapeDtypeStruct((b, s_len, w), F32),
        scratch_shapes=[pltpu.VMEM((b, w, w), F32)],
        compiler_params=_cparams(("arbitrary",)),
        name="rwkv7",
    )(rw3, rw3, mu_p, wup_p, aup_p, vec)


def _out_kernel(ya_ref, yb_ref, yc_ref, yd_ref, x_ref, p_ref, wo_ref, gp_ref, pp_ref, pg_ref, o_ref):
    w = W_GRP
    acc = None
    for i, y in enumerate((ya_ref, yb_ref, yc_ref, yd_ref)):
        t = jnp.dot(y[...].astype(BF16), wo_ref[i * w:(i + 1) * w, :], preferred_element_type=F32)
        acc = t if acc is None else acc + t
    ms = jnp.mean(acc * acc, axis=-1, keepdims=True)
    x1 = x_ref[...] + acc * lax.rsqrt(ms + NORM_EPS) * gp_ref[...]
    gate = _sigmoid(jnp.dot(x1.astype(BF16), pg_ref[...], preferred_element_type=F32))
    pe = jnp.dot(p_ref[...].astype(BF16), pp_ref[...], preferred_element_type=F32)
    o_ref[...] = x1 + gate * pe


def _out_proj(ys, x2, p3, layer, wo_all, g_post, pp_all, pg_all, tm=512):
    n, d = x2.shape
    w = W_GRP
    gp = g_post.reshape(1, d)
    of_layer = lambda a: pl.BlockSpec((None,) + a.shape[1:], lambda i: (layer,) + (0,) * (a.ndim - 1))
    return pl.pallas_call(
        _out_kernel,
        grid=(n // tm,),
        in_specs=[pl.BlockSpec((tm, w), lambda i: (i, 0))] * 4
                 + [pl.BlockSpec((tm, d), lambda i: (i, 0)),
                    pl.BlockSpec((None, tm, p3.shape[2]), lambda i: (layer, i, 0)),
                    of_layer(wo_all), pl.BlockSpec(gp.shape, lambda i: (0, 0)), of_layer(pp_all), of_layer(pg_all)],
        out_specs=pl.BlockSpec((tm, d), lambda i: (i, 0)),
        out_shape=jax.ShapeDtypeStruct((n, d), F32),
        compiler_params=_cparams(("parallel",)),
        name="out_proj",
    )(*ys, x2, p3, wo_all, gp, pp_all, pg_all)


def _prep_w_kernel(w_ref, o_ref):
    wv = w_ref[...]
    rows = wv.shape[0]
    o_ref[...] = jnp.concatenate([jnp.zeros((rows, b_), F32) if a_ is None else wv[:, a_:b_]
                                  for a_, b_ in _SEGMENTS], axis=1).astype(BF16)


def _prep_w_in(w_in, tr=128):
    depth, d, n_in = w_in.shape
    return pl.pallas_call(
        _prep_w_kernel,
        grid=(depth, d // tr),
        in_specs=[pl.BlockSpec((None, tr, n_in), lambda l, r: (l, r, 0))],
        out_specs=pl.BlockSpec((None, tr, N_PROJ), lambda l, r: (l, r, 0)),
        out_shape=jax.ShapeDtypeStruct((depth, d, N_PROJ), BF16),
        compiler_params=_cparams(("parallel", "parallel")),
        name="prep_w_in",
    )(w_in)


def _layer(x2, p3, layer, b, s_len, bt, pt, imat, w_p_all, wo_all, pp_all, pg_all,
           g_pre, g_post, nsa_pos, nsa_w1, nsa_w2,
           sgu_ln_g, sgu_ln_b, sgu_w, sgu_b, conv_w, conv_b, conv_ln_g, conv_ln_b, conv_pw, conv_pw_b,
           rwkv_mu, rwkv_w0, rwkv_w_up, rwkv_a0, rwkv_a_up, rwkv_k_k, rwkv_k_a, rwkv_r_k,
           rwkv_gn_g, rwkv_gn_b):
    n = x2.shape[0]
    w = W_GRP
    t_len = sgu_w.shape[-1]
    wall = jnp.transpose(sgu_w, (1, 0, 2)).reshape(t_len, N_HEADS * t_len)
    sb = jnp.repeat(sgu_b.T, HEAD_DIM, axis=1)
    nsa, rw, yb, yc = _in_proj_gmlp_conv(x2, g_pre, w_p_all, layer, s_len, wall, sb, sgu_ln_g, sgu_ln_b, conv_w, conv_b,
                                         conv_ln_g, conv_ln_b, conv_pw, conv_pw_b, t_len)

    nsa3 = nsa.reshape(b, s_len, NSA_W)
    kc, vct = _nsa_compress(nsa3, nsa_pos, nsa_w1, nsa_w2)
    ya = _nsa_attention(nsa3, kc, vct, bt, pt, imat).reshape(n, w)

    lora = RWKV_LORA
    mu_p = jnp.concatenate([rwkv_mu, jnp.zeros((RW_W - rwkv_mu.shape[0],), F32)]).reshape(1, RW_W)
    wup_p = jnp.concatenate([rwkv_w_up, jnp.zeros((LANES - lora, w), F32)], axis=0)
    aup_p = jnp.concatenate([jnp.zeros((lora, w), F32), rwkv_a_up, jnp.zeros((LANES - 2 * lora, w), F32)], axis=0)
    vec = jnp.stack([rwkv_w0, rwkv_a0, rwkv_k_k, rwkv_k_a, rwkv_r_k.reshape(w), rwkv_gn_g, rwkv_gn_b,
                     jnp.zeros((w,), F32)], axis=0)
    yd = _rwkv(rw.reshape(b, s_len, RW_W), mu_p, wup_p, aup_p, vec).reshape(n, w)

    return _out_proj((ya, yb, yc, yd), x2, p3, layer, wo_all, g_post, pp_all, pg_all)


def kernel(x, p, rel_bias, w_in, w_out, g_pre, g_post, nsa_pos, nsa_w1, nsa_w2, sgu_ln_g, sgu_ln_b, sgu_w, sgu_b, conv_w, conv_b, conv_ln_g, conv_ln_b, conv_pw, conv_pw_b, rwkv_mu, rwkv_w0, rwkv_w_up, rwkv_a0, rwkv_a_up, rwkv_k_k, rwkv_k_a, rwkv_r_k, rwkv_gn_g, rwkv_gn_b, ple_proj, ple_gate):
    b, s_len, d = x.shape
    depth = w_in.shape[0]
    bt, pt, imat = _nsa_tables(rel_bias, s_len)
    x2 = x.reshape(b * s_len, d)
    p3 = p.reshape(depth, b * s_len, p.shape[-1])
    w_p_all = _prep_w_in(w_in)
    wo_all, pp_all, pg_all = w_out.astype(BF16), ple_proj.astype(BF16), ple_gate.astype(BF16)
    per_layer = (g_pre, g_post, nsa_pos, nsa_w1, nsa_w2, sgu_ln_g, sgu_ln_b, sgu_w, sgu_b,
                 conv_w, conv_b, conv_ln_g, conv_ln_b, conv_pw, conv_pw_b, rwkv_mu, rwkv_w0, rwkv_w_up,
                 rwkv_a0, rwkv_a_up, rwkv_k_k, rwkv_k_a, rwkv_r_k, rwkv_gn_g, rwkv_gn_b)
    for i in range(depth):
        x2 = _layer(x2, p3, i, b, s_len, bt, pt, imat, w_p_all, wo_all, pp_all, pg_all,
                    *[a[i] for a in per_layer])
    return x2.reshape(b, s_len, d)
```

```python
import functools
import math

import numpy as np
import jax
import jax.numpy as jnp
from jax import lax
from jax.experimental import pallas as pl
from jax.experimental.pallas import tpu as pltpu

F32 = jnp.float32
BF16 = jnp.bfloat16

W_GRP = 256
HEAD_DIM = 64
N_HEADS = 4
NSA_DK = 64
CMP_STRIDE = 16
CMP_BLOCK = 32
CMP_HIDDEN = 128
SEL_BLOCK = 64
N_SEL = 16
WINDOW = 512
REL_BUCKETS = 32
REL_MAX_EXACT = 16
REL_MAX_DIST = 128
CONV_WIDTH = 31
RWKV_LORA = 32
RWKV_GN_EPS = 64e-5
NORM_EPS = 1e-6
LN_EPS = 1e-5
NEG = -1e30
FORCE = 1e4
LOG2E = math.log2(math.e)

LANES = 128
SUBLANES = 8
Q_TILE = 128
SEL_GROUP = 4
NSA_STREAMS = 4
RW_CHUNK = 64
RW_SUB = 4
VMEM_LIMIT = 48 * 1024 * 1024
NSA_VMEM_LIMIT = 56 * 1024 * 1024

NSA_W, GM_W, CV_W, RW_W = 1024, 768, 768, 1152
N_PROJ = NSA_W + GM_W + CV_W + RW_W


def _proj_segments():
    names = ["q", "kc", "vc", "ks", "vs", "kw", "vw", "g", "za", "u", "v", "zb", "ga", "gb", "zc", "rw", "zd"]
    widths = [256, 64, 64, 64, 64, 64, 64, 12, 256, 256, 256, 256, 256, 256, 256, 832, 256]
    o, off = {}, 0
    for n, w in zip(names, widths):
        o[n] = (off, off + w)
        off += w
    return [o["q"], o["za"], (o["ks"][0], o["vw"][1]), (o["kc"][0], o["vc"][1]), o["g"], (None, 116),
            (o["u"][0], o["zc"][1]), o["rw"], (None, 64), o["zd"]]


_SEGMENTS = _proj_segments()


def _rel_buckets(n):
    d = np.arange(n)
    nf = np.maximum(d, REL_MAX_EXACT).astype(np.float32)
    large = REL_MAX_EXACT + (np.log(nf / np.float32(REL_MAX_EXACT)) / np.float32(math.log(REL_MAX_DIST / REL_MAX_EXACT))
                             * np.float32(REL_BUCKETS - REL_MAX_EXACT)).astype(np.int32)
    large = np.minimum(large, REL_BUCKETS - 1)
    return np.where(d < REL_MAX_EXACT, d, large)


def _bdot(a, b):
    return jnp.dot(a.astype(BF16), b.astype(BF16), preferred_element_type=F32)


def _nt(a, b):
    return lax.dot_general(a, b, (((1,), (1,)), ((), ())), preferred_element_type=F32)


def _split2(a):
    hi = a.astype(BF16)
    lo = (a - hi.astype(F32)).astype(BF16)
    return hi, lo


def _split3(a):
    hi = a.astype(BF16)
    r = a - hi.astype(F32)
    mid = r.astype(BF16)
    lo = (r - mid.astype(F32)).astype(BF16)
    return hi, mid, lo


def _dot3(a, b):
    ah, al = _split2(a)
    bh, bl = _split2(b)
    d = lambda x, y: jnp.dot(x, y, preferred_element_type=F32)
    return d(ah, bh) + (d(ah, bl) + d(al, bh))


def _dot3_nt(a, b):
    ah, al = _split2(a)
    bh, bl = _split2(b)
    return _nt(ah, bh) + (_nt(ah, bl) + _nt(al, bh))


def _dot_sel_lhs(a01, b):
    a = a01.astype(BF16)
    bh, bm, bl = _split3(b)
    d = lambda y: jnp.dot(a, y, preferred_element_type=F32)
    return d(bh) + (d(bm) + d(bl))


def _dot_sel_rhs(a, b01):
    b = b01.astype(BF16)
    ah, am, al = _split3(a)
    d = lambda x: jnp.dot(x, b, preferred_element_type=F32)
    return d(ah) + (d(am) + d(al))


def _sigmoid(x):
    return 1.0 / (1.0 + jnp.exp(-x))


def _silu(x):
    return x * _sigmoid(x)


def _gelu_tanh(x):
    c = math.sqrt(2.0 / math.pi)
    return 0.5 * x * (1.0 + jnp.tanh(c * (x + 0.044715 * (x * x * x))))


def _softplus(x):
    return jnp.maximum(x, 0.0) + jnp.log(1.0 + jnp.exp(-jnp.abs(x)))


def _layer_norm(x, g, b):
    mu = jnp.mean(x, axis=-1, keepdims=True)
    xc = x - mu
    var = jnp.mean(xc * xc, axis=-1, keepdims=True)
    return xc * lax.rsqrt(var + LN_EPS) * g + b


def _block_diag(x, n):
    r, c = x.shape
    t = jnp.concatenate([x] * n, axis=0)
    ri = lax.broadcasted_iota(jnp.int32, t.shape, 0) // r
    ci = lax.broadcasted_iota(jnp.int32, t.shape, 1) // (c // n)
    return jnp.where(ri == ci, t, 0.0)


def _cparams(sem, vmem=VMEM_LIMIT):
    return pltpu.CompilerParams(dimension_semantics=sem, vmem_limit_bytes=vmem)


def _proj_kernel(x_ref, g_ref, w_ref, o_nsa, o_gm, o_cv, o_rw):
    x = x_ref[...]
    ms = jnp.mean(x * x, axis=-1, keepdims=True)
    h = (x * lax.rsqrt(ms + NORM_EPS) * g_ref[...]).astype(BF16)
    off = 0
    for o in (o_nsa, o_gm, o_cv, o_rw):
        wd = o.shape[-1]
        o[...] = jnp.dot(h, w_ref[:, off:off + wd], preferred_element_type=F32)
        off += wd


def _in_proj(x2, g_pre, w_p, tm=256):
    n, d = x2.shape
    widths = (NSA_W, GM_W, CV_W, RW_W)
    return pl.pallas_call(
        _proj_kernel,
        grid=(n // tm,),
        in_specs=[pl.BlockSpec((tm, d), lambda i: (i, 0)),
                  pl.BlockSpec((1, d), lambda i: (0, 0)),
                  pl.BlockSpec((d, N_PROJ), lambda i: (0, 0))],
        out_specs=[pl.BlockSpec((tm, w), lambda i: (i, 0)) for w in widths],
        out_shape=[jax.ShapeDtypeStruct((n, w), F32) for w in widths],
        compiler_params=_cparams(("parallel",)),
        name="in_proj",
    )(x2, g_pre.reshape(1, d), w_p)


def _cmp_kernel(kv_ref, pos_ref, w1_ref, w2_ref, o_ref, ot_ref):
    nc = o_ref.shape[0]
    dk = o_ref.shape[1]
    first = None
    second = None
    for r in range(CMP_STRIDE):
        x = kv_ref[pl.ds(r, nc, stride=CMP_STRIDE), :]
        t = _dot3(x + pos_ref[r], w1_ref[r])
        first = t if first is None else first + t
        t = _dot3(x + pos_ref[CMP_STRIDE + r], w1_ref[CMP_STRIDE + r])
        second = t if second is None else second + t
    hid = first + pltpu.roll(second, nc - 1, axis=0)
    out = _dot3(_silu(hid), w2_ref[...])
    o_ref[...] = out[:, :dk]
    ot_ref[...] = out.T[dk:, :]


def _nsa_compress(nsa3, pos, w1, w2):
    b, s_len, _ = nsa3.shape
    nc = s_len // CMP_STRIDE
    dk = NSA_DK
    zw = jnp.zeros((CMP_BLOCK, dk, CMP_HIDDEN), F32)
    w1r = w1.reshape(2, CMP_BLOCK, dk, CMP_HIDDEN)
    w1bd = jnp.concatenate([jnp.concatenate([w1r[0], zw], axis=2),
                            jnp.concatenate([zw, w1r[1]], axis=2)], axis=1)
    z2 = jnp.zeros((CMP_HIDDEN, dk), F32)
    w2bd = jnp.concatenate([jnp.concatenate([w2[0], z2], axis=1),
                            jnp.concatenate([z2, w2[1]], axis=1)], axis=0)
    posr = jnp.concatenate([pos[0], pos[1]], axis=1).reshape(CMP_BLOCK, 1, 2 * dk)
    full = lambda a: pl.BlockSpec(a.shape, lambda i: (0,) * a.ndim)
    return pl.pallas_call(
        _cmp_kernel,
        grid=(b,),
        in_specs=[pl.BlockSpec((None, s_len, LANES), lambda i: (i, 0, 6)),
                  full(posr), full(w1bd), full(w2bd)],
        out_specs=[pl.BlockSpec((None, nc, dk), lambda i: (i, 0, 0)),
                   pl.BlockSpec((None, dk, nc), lambda i: (i, 0, 0))],
        out_shape=[jax.ShapeDtypeStruct((b, nc, dk), F32),
                   jax.ShapeDtypeStruct((b, dk, nc), F32)],
        compiler_params=_cparams(("parallel",)),
        name="nsa_compress",
    )(nsa3, posr, w1bd, w2bd)


def _nsa_kernel(q_ref, za_ref, g_ref, ks_ref, kw_ref, kc_ref, vct_ref, pt_ref, bt_ref, imat_ref,
                o_ref, ksb, vst, kwb, vwt, madd, m_s, acc_s, *, n_top, n_win):
    c = pl.program_id(1)
    n_qt = pl.num_programs(1)
    n_kt = ksb.shape[0]
    hd = HEAD_DIM
    nh = N_HEADS
    hq = nh * Q_TILE

    n_blk = imat_ref.shape[0]
    per_q = Q_TILE // SEL_BLOCK
    v_rows = vst.shape[1]

    @pl.when(c == 0)
    def _prep():
        key_blk = lax.broadcasted_iota(jnp.int32, (Q_TILE, n_blk), 0) // SEL_BLOCK
        col = lax.broadcasted_iota(jnp.int32, (Q_TILE, n_blk), 1)
        ones_row = jnp.where(lax.broadcasted_iota(jnp.int32, (v_rows - hd, Q_TILE), 0) == 0, 1.0, 0.0)

        def body(i, carry):
            r0 = pl.multiple_of(i * Q_TILE, Q_TILE)
            t = ks_ref[pl.ds(r0, Q_TILE), :]
            onehot = jnp.where(col == key_blk + per_q * i, 1.0, 0.0)
            ksb[i] = jnp.concatenate([t[:, :hd], onehot], axis=1).astype(BF16)
            vst[i] = jnp.concatenate([t.T[hd:, :], ones_row], axis=0).astype(BF16)
            t = kw_ref[pl.ds(r0, Q_TILE), :]
            kwb[i] = t[:, :hd].astype(BF16)
            vwt[i] = jnp.concatenate([t.T[hd:, :], ones_row], axis=0).astype(BF16)
            return carry
        lax.fori_loop(0, n_kt, body, 0)

    q = q_ref[...] * (hd ** -0.5 * LOG2E)
    q_all = jnp.concatenate([q[:, hd * h:hd * (h + 1)] for h in range(nh)], axis=0)
    qb = q_all.astype(BF16)
    g_t = _sigmoid(g_ref[...]).T
    gate = [jnp.concatenate([g_t[3 * h + br:3 * h + br + 1, :] for h in range(nh)], axis=1) for br in range(3)]

    kc = kc_ref[...]
    n_c = kc.shape[0]
    off = pl.multiple_of((n_qt - 1 - c) * (Q_TILE // CMP_STRIDE), SUBLANES)
    bias = pt_ref[pl.ds(off, n_c), :]
    s = _dot3_nt(kc, q_all) + bias
    valid = bias > 0.5 * NEG
    m = jnp.max(s, axis=0, keepdims=True)
    p = jnp.where(valid, jnp.exp2(s - m), 0.0)
    l = jnp.sum(p, axis=0, keepdims=True)
    pn = p * jnp.where(l > 0.0, 1.0 / l, 0.0)
    y_acc = gate[0] * _bdot(vct_ref[...], pn)
    pc = pn[:, 0:Q_TILE]
    for h in range(1, nh):
        pc = pc + pn[:, h * Q_TILE:(h + 1) * Q_TILE]

    tl = lax.broadcasted_iota(jnp.int32, (n_blk, Q_TILE), 1)
    jb = lax.broadcasted_iota(jnp.int32, (n_blk, Q_TILE), 0)
    cur = c * per_q + tl // SEL_BLOCK
    causal = jb <= cur
    need_rank = (c + 1) * per_q > n_top

    @pl.when(jnp.logical_not(need_rank))
    def _all_causal():
        madd[...] = jnp.where(causal, 0.0, NEG)

    @pl.when(need_rank)
    def _rank():
        imp = _dot_sel_lhs(imat_ref[...], pc)
        forced = (jb == 0) | (jb == cur) | (jb == cur - 1)
        val = jnp.where(causal, jnp.where(forced, FORCE, imp), NEG)
        n_r = n_blk // SUBLANES
        blocks = [val[SUBLANES * r:SUBLANES * (r + 1), :] for r in range(n_r)]
        cnts = [jnp.zeros((SUBLANES, Q_TILE), F32) for _ in range(n_r)]
        jrow = lax.broadcasted_iota(jnp.int32, (SUBLANES, Q_TILE), 0)
        for i in range(n_blk):
            vi = jnp.broadcast_to(val[i:i + 1, :], (SUBLANES, Q_TILE))
            for r in range(n_r):
                if SUBLANES * r > i:
                    beats = vi >= blocks[r]
                elif SUBLANES * r + SUBLANES - 1 < i:
                    beats = vi > blocks[r]
                else:
                    ge = jnp.where(vi >= blocks[r], 1.0, 0.0)
                    gt = jnp.where(vi > blocks[r], 1.0, 0.0)
                    cnts[r] = cnts[r] + jnp.where(jrow + SUBLANES * r > i, ge, gt)
                    continue
                cnts[r] = cnts[r] + jnp.where(beats, 1.0, 0.0)
        cnt = jnp.concatenate(cnts, axis=0)
        madd[...] = jnp.where(causal & (cnt < float(n_top)), 0.0, NEG)

    def reset_state():
        m_s[...] = jnp.full(m_s.shape, NEG, F32)
        acc_s[...] = jnp.zeros(acc_s.shape, F32)

    def update(ss, vts):
        m_old = m_s[...]
        m_new = m_old
        for s_i in ss:
            m_new = jnp.maximum(m_new, jnp.max(s_i, axis=0, keepdims=True))
        pv = None
        for s_i, v_i in zip(ss, vts):
            t = jnp.dot(v_i, jnp.exp2(s_i - m_new).astype(BF16), preferred_element_type=F32)
            pv = t if pv is None else pv + t
        acc_s[...] = jnp.exp2(m_old - m_new) * acc_s[...] + pv
        m_s[...] = m_new

    def branch_out(g):
        acc = acc_s[...]
        return (g / acc[hd:hd + 1, :]) * acc[:hd, :]

    pad = jnp.zeros((LANES - n_blk, Q_TILE), F32)
    m_t = jnp.concatenate([madd[...], pad], axis=0).T[:, :n_blk]
    blk_lane = lax.broadcasted_iota(jnp.int32, m_t.shape, 1)
    m_far = jnp.where(blk_lane < per_q * (c - 1), m_t, NEG)
    q_near = jnp.concatenate([q_all, jnp.concatenate([m_t] * nh, axis=0)], axis=1).astype(BF16)
    q_far = jnp.concatenate([q_all, jnp.concatenate([m_far] * nh, axis=0)], axis=1).astype(BF16)

    reset_state()
    kt1 = jnp.maximum(c - 1, 0)
    s0 = _nt(ksb[c], q_near) + bt_ref[0]
    s1 = _nt(ksb[kt1], q_near) + bt_ref[1] + jnp.where(c >= 1, 0.0, NEG)
    update([s0, s1], [vst[c], vst[kt1]])

    n_far = jnp.maximum(c - 1, 0)

    def far_body(g, carry):
        kts = [g * SEL_GROUP + j for j in range(SEL_GROUP)]
        update([_nt(ksb[kt], q_far) for kt in kts], [vst[kt] for kt in kts])
        return carry
    lax.fori_loop(0, (n_far + SEL_GROUP - 1) // SEL_GROUP, far_body, 0)
    y_acc = y_acc + branch_out(gate[1])

    reset_state()
    ss, vts = [], []
    for i in range(n_win + 1):
        kt = jnp.maximum(c - i, 0)
        s_i = _nt(kwb[kt], qb)
        if i == 0:
            s_i = s_i + bt_ref[0]
        else:
            if i == 1:
                s_i = s_i + bt_ref[1]
            elif i == n_win:
                s_i = s_i + bt_ref[2]
            s_i = s_i + jnp.where(c >= i, 0.0, NEG)
        ss.append(s_i)
        vts.append(vwt[kt])
    update(ss, vts)
    y_acc = y_acc + branch_out(gate[2])

    y_t = jnp.concatenate([y_acc[:, h * Q_TILE:(h + 1) * Q_TILE] for h in range(nh)], axis=0)
    o_ref[...] = y_t.T * _silu(za_ref[...])


def _nsa_kernel_multi(q_ref, za_ref, g_ref, ks_ref, kw_ref, kc_ref, vct_ref, pt_ref, bt_ref, imat_ref,
                      o_ref, ksb, vst, kwb, vwt, madd, val_s, cnt_s, pc_s, yc_s, m_s, acc_s, *, n_top, n_win):
    c = pl.program_id(1)
    n_qt = pl.num_programs(1)
    n_st = q_ref.shape[0]
    n_kt = ksb.shape[1]
    hd = HEAD_DIM
    nh = N_HEADS
    n_blk = imat_ref.shape[0]
    per_q = Q_TILE // SEL_BLOCK
    v_rows = vst.shape[2]
    streams = range(n_st)

    @pl.when(c == 0)
    def _prep():
        key_blk = lax.broadcasted_iota(jnp.int32, (Q_TILE, n_blk), 0) // SEL_BLOCK
        col = lax.broadcasted_iota(jnp.int32, (Q_TILE, n_blk), 1)
        ones_row = jnp.where(lax.broadcasted_iota(jnp.int32, (v_rows - hd, Q_TILE), 0) == 0, 1.0, 0.0)

        def body(i, carry):
            r0 = pl.multiple_of(i * Q_TILE, Q_TILE)
            onehot = jnp.where(col == key_blk + per_q * i, 1.0, 0.0)
            for s in streams:
                t = ks_ref[s, pl.ds(r0, Q_TILE), :]
                ksb[s, i] = jnp.concatenate([t[:, :hd], onehot], axis=1).astype(BF16)
                vst[s, i] = jnp.concatenate([t.T[hd:, :], ones_row], axis=0).astype(BF16)
                t = kw_ref[s, pl.ds(r0, Q_TILE), :]
                kwb[s, i] = t[:, :hd].astype(BF16)
                vwt[s, i] = jnp.concatenate([t.T[hd:, :], ones_row], axis=0).astype(BF16)
            return carry
        lax.fori_loop(0, n_kt, body, 0)

    def reset_state():
        m_s[...] = jnp.full(m_s.shape, NEG, F32)
        acc_s[...] = jnp.zeros(acc_s.shape, F32)

    def update(groups, between=None):
        m_old = [m_s[s] for s in streams]
        m_new = []
        for s in streams:
            m = m_old[s]
            for s_i in groups[s][0]:
                m = jnp.maximum(m, jnp.max(s_i, axis=0, keepdims=True))
            m_new.append(m)
        for s in streams:
            ss, vs = groups[s]
            pv = None
            for j in range(0, len(ss), 2):
                if between is not None:
                    between(s, j)
                p_j = jnp.concatenate([jnp.exp2(s_i - m_new[s]).astype(BF16) for s_i in ss[j:j + 2]], axis=0)
                v_j = jnp.concatenate(vs[j:j + 2], axis=1)
                t = jnp.dot(v_j, p_j, preferred_element_type=F32)
                pv = t if pv is None else pv + t
            acc_s[s] = jnp.exp2(m_old[s] - m_new[s]) * acc_s[s] + pv
            m_s[s] = m_new[s]

    def branch_out(s, g):
        acc = acc_s[s]
        return (g / acc[hd:hd + 1, :]) * acc[:hd, :]

    q_all, qb, gate = [], [], []
    for s in streams:
        q = q_ref[s] * (hd ** -0.5 * LOG2E)
        qa = jnp.concatenate([q[:, hd * h:hd * (h + 1)] for h in range(nh)], axis=0)
        q_all.append(qa)
        qb.append(qa.astype(BF16))
        g_t = _sigmoid(g_ref[s]).T
        gate.append([jnp.concatenate([g_t[3 * h + br:3 * h + br + 1, :] for h in range(nh)], axis=1)
                     for br in range(3)])

    reset_state()
    groups = []
    for s in streams:
        ss, vts = [], []
        for i in range(n_win + 1):
            kt = jnp.maximum(c - i, 0)
            s_i = _nt(kwb[s, kt], qb[s])
            if i == 0:
                s_i = s_i + bt_ref[0]
            else:
                if i == 1:
                    s_i = s_i + bt_ref[1]
                elif i == n_win:
                    s_i = s_i + bt_ref[2]
                s_i = s_i + jnp.where(c >= i, 0.0, NEG)
            ss.append(s_i)
            vts.append(vwt[s, kt])
        groups.append((ss, vts))
    update(groups)
    y_acc = [branch_out(s, gate[s][2]) for s in streams]

    n_c = kc_ref.shape[1]
    per_tile = Q_TILE // CMP_STRIDE
    off = pl.multiple_of((n_qt - 1 - c) * per_tile, SUBLANES)

    def cmp_branch(rows):
        bias = pt_ref[pl.ds(off, rows), :]
        valid = bias > 0.5 * NEG
        scs = [_dot3_nt(kc_ref[s, 0:rows, :], q_all[s]) + bias for s in streams]
        ps = [jnp.where(valid, jnp.exp2(sc - jnp.max(sc, axis=0, keepdims=True)), 0.0) for sc in scs]
        ls = [jnp.sum(p, axis=0, keepdims=True) for p in ps]
        pns = [p * jnp.where(l > 0.0, 1.0 / l, 0.0) for p, l in zip(ps, ls)]
        for s in streams:
            yc_s[s] = gate[s][0] * _bdot(vct_ref[s, :, 0:rows], pns[s])
            pc = pns[s][:, 0:Q_TILE]
            for h in range(1, nh):
                pc = pc + pns[s][:, h * Q_TILE:(h + 1) * Q_TILE]
            pc_s[s, 0:rows, :] = pc
            if rows < n_c:
                pc_s[s, rows:n_c, :] = jnp.zeros((n_c - rows, Q_TILE), F32)

    half = n_c // 2
    few = (c + 1) * per_tile <= half
    pl.when(few)(lambda: cmp_branch(half))
    pl.when(jnp.logical_not(few))(lambda: cmp_branch(n_c))

    tl = lax.broadcasted_iota(jnp.int32, (n_blk, Q_TILE), 1)
    jb = lax.broadcasted_iota(jnp.int32, (n_blk, Q_TILE), 0)
    cur = c * per_q + tl // SEL_BLOCK
    causal = jb <= cur
    need_rank = (c + 1) * per_q > n_top

    @pl.when(jnp.logical_not(need_rank))
    def _all_causal():
        for s in streams:
            madd[s] = jnp.where(causal, 0.0, NEG)

    @pl.when(need_rank)
    def _rank():
        forced = (jb == 0) | (jb == cur) | (jb == cur - 1)
        n_r = n_blk // SUBLANES
        jrow = lax.broadcasted_iota(jnp.int32, (SUBLANES, Q_TILE), 0)
        for s in streams:
            imp = _dot_sel_lhs(imat_ref[...], pc_s[s])
            val_s[s] = jnp.where(causal, jnp.where(forced, FORCE, imp), NEG)
            cnt_s[s] = jnp.zeros((n_blk, Q_TILE), F32)
        for ci in range(n_r):
            @pl.when(SUBLANES * ci < (c + 1) * per_q)
            def _chunk():
                for s in streams:
                    val = val_s[s]
                    blocks = [val[SUBLANES * r:SUBLANES * (r + 1), :] for r in range(n_r)]
                    cnts = [None] * n_r
                    for i in range(SUBLANES * ci, SUBLANES * (ci + 1)):
                        vi = jnp.broadcast_to(val[i:i + 1, :], (SUBLANES, Q_TILE))
                        for r in range(n_r):
                            if r > ci:
                                t = jnp.where(vi >= blocks[r], 1.0, 0.0)
                            elif r < ci:
                                t = jnp.where(vi > blocks[r], 1.0, 0.0)
                            else:
                                ge = jnp.where(vi >= blocks[r], 1.0, 0.0)
                                gt = jnp.where(vi > blocks[r], 1.0, 0.0)
                                t = jnp.where(jrow + SUBLANES * r > i, ge, gt)
                            cnts[r] = t if cnts[r] is None else cnts[r] + t
                    cnt_s[s] = cnt_s[s] + jnp.concatenate(cnts, axis=0)
        for s in streams:
            madd[s] = jnp.where(causal & (cnt_s[s] < float(n_top)), 0.0, NEG)

    reset_state()
    pad = jnp.zeros((LANES - n_blk, Q_TILE), F32)
    kt1 = jnp.maximum(c - 1, 0)
    q_far, groups = [], []
    for s in streams:
        m_t = jnp.concatenate([madd[s], pad], axis=0).T[:, :n_blk]
        blk_lane = lax.broadcasted_iota(jnp.int32, m_t.shape, 1)
        m_far = jnp.where(blk_lane < per_q * (c - 1), m_t, NEG)
        q_near = jnp.concatenate([q_all[s], jnp.concatenate([m_t] * nh, axis=0)], axis=1).astype(BF16)
        q_far.append(jnp.concatenate([q_all[s], jnp.concatenate([m_far] * nh, axis=0)], axis=1).astype(BF16))
        s0 = _nt(ksb[s, c], q_near) + bt_ref[0]
        s1 = _nt(ksb[s, kt1], q_near) + bt_ref[1] + jnp.where(c >= 1, 0.0, NEG)
        groups.append(([s0, s1], [vst[s, c], vst[s, kt1]]))

    update(groups)

    n_far = jnp.maximum(c - 1, 0)

    def far_body(g, carry):
        kts = [g * SEL_GROUP + j for j in range(SEL_GROUP)]
        update([([_nt(ksb[s, kt], q_far[s]) for kt in kts], [vst[s, kt] for kt in kts]) for s in streams])
        return carry
    lax.fori_loop(0, (n_far + SEL_GROUP - 1) // SEL_GROUP, far_body, 0)

    for s in streams:
        y = y_acc[s] + yc_s[s] + branch_out(s, gate[s][1])
        y_t = jnp.concatenate([y[:, h * Q_TILE:(h + 1) * Q_TILE] for h in range(nh)], axis=0)
        o_ref[s] = y_t.T * _silu(za_ref[s])


def _bucket_lookup(relc, idx, visible):
    r, cc = idx.shape
    onehot = (jnp.asarray(idx.reshape(1, -1)) == jnp.arange(relc.shape[0])[:, None]).astype(F32)
    vals = jnp.einsum("bh,bn->hn", relc, onehot, precision=lax.Precision.HIGHEST).reshape(-1, r, cc)
    vals = jnp.where(jnp.asarray(visible)[None], vals, NEG)
    return jnp.transpose(vals, (1, 0, 2)).reshape(r, -1).astype(F32)


def _nsa_tables(rel_bias, s_len):
    n_qt = s_len // Q_TILE
    n_c = s_len // CMP_STRIDE
    relc = rel_bias - rel_bias[REL_BUCKETS - 1][None, :]
    bk = _rel_buckets(s_len + Q_TILE)
    kl = np.arange(Q_TILE)[:, None]
    tq = np.arange(Q_TILE)[None, :]
    d0 = tq - kl
    diag = _bucket_lookup(relc, bk[np.clip(d0, 0, None)], d0 >= 0)
    prev = _bucket_lookup(relc, bk[Q_TILE + d0], np.ones_like(d0, bool))
    edge = jnp.asarray(np.tile(np.where(kl > tq, 0.0, NEG).astype(np.float32), (1, N_HEADS)))
    bt = jnp.stack([diag, prev, edge], axis=0) * LOG2E
    per_q = Q_TILE // CMP_STRIDE
    n_rows = n_c + per_q * (n_qt - 1)
    r = np.arange(n_rows)[:, None]
    dc = tq - CMP_STRIDE * (r - per_q * (n_qt - 1)) - (CMP_BLOCK - 1)
    pt = _bucket_lookup(relc, bk[np.clip(dc, 0, None)], dc >= 0) * LOG2E
    n_blk = s_len // SEL_BLOCK
    ratio = SEL_BLOCK // CMP_STRIDE
    jj = np.arange(n_blk)[:, None]
    ii = np.arange(n_c)[None, :]
    imat = ((ii >= ratio * jj - 1) & (ii <= ratio * jj + ratio - 1) & (ii < n_c - 1)).astype(np.float32)
    return bt, pt, jnp.asarray(imat)


def _nsa_attention(nsa3, kc, vct, bt, pt, imat):
    b, s_len, _ = nsa3.shape
    n_qt = s_len // Q_TILE
    n_c = s_len // CMP_STRIDE
    n_blk = s_len // SEL_BLOCK
    n_win = WINDOW // Q_TILE
    hd = HEAD_DIM
    hq = N_HEADS * Q_TILE
    v_rows = hd + 2 * SUBLANES
    n_st = NSA_STREAMS if b % NSA_STREAMS == 0 else 1
    kern = functools.partial(_nsa_kernel_multi, n_top=min(N_SEL, n_blk), n_win=n_win)
    return pl.pallas_call(
        kern,
        grid=(b // n_st, n_qt),
        in_specs=[pl.BlockSpec((n_st, Q_TILE, 256), lambda i, c: (i, c, 0)),
                  pl.BlockSpec((n_st, Q_TILE, 256), lambda i, c: (i, c, 1)),
                  pl.BlockSpec((n_st, Q_TILE, LANES), lambda i, c: (i, c, 7)),
                  pl.BlockSpec((n_st, s_len, LANES), lambda i, c: (i, 0, 4),
                               pipeline_mode=pl.Buffered(1)),
                  pl.BlockSpec((n_st, s_len, LANES), lambda i, c: (i, 0, 5),
                               pipeline_mode=pl.Buffered(1)),
                  pl.BlockSpec((n_st, n_c, hd), lambda i, c: (i, 0, 0)),
                  pl.BlockSpec((n_st, hd, n_c), lambda i, c: (i, 0, 0)),
                  pl.BlockSpec(pt.shape, lambda i, c: (0, 0)),
                  pl.BlockSpec(bt.shape, lambda i, c: (0, 0, 0)),
                  pl.BlockSpec(imat.shape, lambda i, c: (0, 0))],
        out_specs=pl.BlockSpec((n_st, Q_TILE, 256), lambda i, c: (i, c, 0)),
        out_shape=jax.ShapeDtypeStruct((b, s_len, 256), F32),
        scratch_shapes=[pltpu.VMEM((n_st, n_qt, Q_TILE, hd + n_blk), BF16),
                        pltpu.VMEM((n_st, n_qt, v_rows, Q_TILE), BF16),
                        pltpu.VMEM((n_st, n_qt, Q_TILE, hd), BF16),
                        pltpu.VMEM((n_st, n_qt, v_rows, Q_TILE), BF16),
                        pltpu.VMEM((n_st, n_blk, Q_TILE), F32),
                        pltpu.VMEM((n_st, n_blk, Q_TILE), F32),
                        pltpu.VMEM((n_st, n_blk, Q_TILE), F32),
                        pltpu.VMEM((n_st, n_c, Q_TILE), F32),
                        pltpu.VMEM((n_st, hd, hq), F32),
                        pltpu.VMEM((n_st, 1, hq), F32),
                        pltpu.VMEM((n_st, v_rows, hq), F32)],
        compiler_params=_cparams(("arbitrary", "arbitrary"), NSA_VMEM_LIMIT),
        name="nsa_attention",
    )(nsa3, nsa3, nsa3, nsa3, nsa3, kc, vct, pt, bt, imat)


def _gc_kernel(gm_ref, cv_ref, halo_ref, wall_ref, sb_ref, lng_ref, lnb_ref,
               cw_ref, cb_ref, clg_ref, clb_ref, cpw_ref, cpb_ref, ob_ref, oc_ref, xs, *, n_ct):
    c = pl.program_id(0) % n_ct
    t_len = gm_ref.shape[0]
    w = W_GRP
    gm = gm_ref[...]
    u = _gelu_tanh(gm[:, :w])
    v = _layer_norm(_gelu_tanh(gm[:, w:2 * w]), lng_ref[...], lnb_ref[...])
    wall = wall_ref[...]
    ti = lax.broadcasted_iota(jnp.int32, wall.shape, 0)
    si = lax.broadcasted_iota(jnp.int32, wall.shape, 1) % t_len
    wall = jnp.where(si <= ti, wall, 0.0)
    sv = _bdot(wall, _block_diag(v, N_HEADS)) + sb_ref[...]
    ob_ref[...] = u * sv * _silu(gm[:, 2 * w:])
    cv = cv_ref[...]
    hl = halo_ref[...]
    hx = hl[:, :w] * _sigmoid(hl[:, w:2 * w])
    hrows = hl.shape[0]
    xs[0:hrows, :] = jnp.where(c == 0, 0.0, hx)
    xs[hrows:hrows + t_len, :] = cv[:, :w] * _sigmoid(cv[:, w:2 * w])
    cw = cw_ref[...]
    acc = jnp.zeros((t_len, w), F32)
    base = hrows - (CONV_WIDTH - 1)
    for j in range(CONV_WIDTH):
        acc = acc + cw[j:j + 1, :] * xs[base + j:base + j + t_len, :]
    y = _layer_norm(acc + cb_ref[...], clg_ref[...], clb_ref[...])
    y = _bdot(_silu(y), cpw_ref[...]) + cpb_ref[...]
    oc_ref[...] = y * _silu(cv[:, 2 * w:])


def _gmlp_conv(gm, cv, s_len, wall, sb, lng, lnb, cw, cb, clg, clb, cpw, cpb, t_len=128, halo=32):
    n = gm.shape[0]
    n_ct = s_len // t_len
    w = W_GRP
    row = lambda a: a.reshape(1, w)
    full = lambda a: pl.BlockSpec(a.shape, lambda i: (0,) * a.ndim)
    cwp = jnp.concatenate([cw, jnp.zeros((32 - CONV_WIDTH, w), F32)], axis=0)
    args = (wall, sb, row(lng), row(lnb), cwp, row(cb), row(clg), row(clb), cpw.astype(BF16), row(cpb))
    per = t_len // halo
    return pl.pallas_call(
        functools.partial(_gc_kernel, n_ct=n_ct),
        grid=(n // t_len,),
        in_specs=[pl.BlockSpec((t_len, GM_W), lambda i: (i, 0)),
                  pl.BlockSpec((t_len, CV_W), lambda i: (i, 0)),
                  pl.BlockSpec((halo, CV_W), lambda i: (jnp.maximum(i * per - 1, 0), 0))]
                 + [full(a) for a in args],
        out_specs=[pl.BlockSpec((t_len, w), lambda i: (i, 0))] * 2,
        out_shape=[jax.ShapeDtypeStruct((n, w), F32)] * 2,
        scratch_shapes=[pltpu.VMEM((halo + t_len, w), F32)],
        compiler_params=_cparams(("parallel",)),
        name="gmlp_conv",
    )(gm, cv, cv, *args)


def _proj_gc_kernel(x_ref, g_ref, w_ref, wall_ref, sb_ref, lng_ref, lnb_ref, cw_ref, cb_ref, clg_ref, clb_ref,
                    cpw_ref, cpb_ref, o_nsa, o_rw, ob_ref, oc_ref, xs, ph_s, *, tiles_per_seq, t_len):
    first = (pl.program_id(0) % tiles_per_seq) == 0
    tm = x_ref.shape[0]
    halo = xs.shape[0] - tm - SUBLANES
    w = W_GRP
    x = x_ref[...]
    ms = jnp.mean(x * x, axis=-1, keepdims=True)
    h = (x * lax.rsqrt(ms + NORM_EPS) * g_ref[...]).astype(BF16)
    c_gm, c_cv, c_rw = NSA_W, NSA_W + GM_W, NSA_W + GM_W + CV_W
    gm = jnp.dot(h, w_ref[:, c_gm:c_cv], preferred_element_type=F32)
    cv = jnp.dot(h, w_ref[:, c_cv:c_rw], preferred_element_type=F32)

    u = _gelu_tanh(gm[:, :w])
    v = _layer_norm(_gelu_tanh(gm[:, w:2 * w]), lng_ref[...], lnb_ref[...])
    zb = _silu(gm[:, 2 * w:])
    wall = wall_ref[...]
    ti = lax.broadcasted_iota(jnp.int32, wall.shape, 0)
    si = lax.broadcasted_iota(jnp.int32, wall.shape, 1) % t_len
    wall = jnp.where(si <= ti, wall, 0.0).astype(BF16)
    for ch in range(tm // t_len):
        r0, r1 = ch * t_len, (ch + 1) * t_len
        sv = jnp.dot(wall, _block_diag(v[r0:r1], N_HEADS).astype(BF16), preferred_element_type=F32) + sb_ref[...]
        ob_ref[r0:r1, :] = u[r0:r1] * sv * zb[r0:r1]

    o_nsa[...] = jnp.dot(h, w_ref[:, :c_gm], preferred_element_type=F32)

    o_rw[...] = jnp.dot(h, w_ref[:, c_rw:], preferred_element_type=F32)
    xs[0:halo, :] = jnp.where(first, 0.0, xs[0:halo, :])
    xs[halo:halo + tm, :] = cv[:, :w] * _sigmoid(cv[:, w:2 * w])
    xs[halo + tm:, :] = jnp.zeros((xs.shape[0] - halo - tm, w), F32)
    cw = cw_ref[...]
    for ph in range(1, SUBLANES):
        ph_s[ph - 1] = xs[ph:ph + halo + tm, :]
    base = halo - (CONV_WIDTH - 1)
    rows = 64
    accs = []
    for q0 in range(0, tm, rows):
        acc_q = jnp.zeros((rows, w), F32)
        for j in range(CONV_WIDTH):
            a, ph = divmod(base + j, SUBLANES)
            r0 = SUBLANES * a + q0
            win = xs[r0:r0 + rows, :] if ph == 0 else ph_s[ph - 1, r0:r0 + rows, :]
            acc_q = acc_q + cw[j:j + 1, :] * win
        accs.append(acc_q)
    acc = jnp.concatenate(accs, axis=0)
    xs[0:halo, :] = xs[tm:tm + halo, :]
    y = _layer_norm(acc + cb_ref[...], clg_ref[...], clb_ref[...])
    y = _bdot(_silu(y), cpw_ref[...]) + cpb_ref[...]
    oc_ref[...] = y * _silu(cv[:, 2 * w:])


def _in_proj_gmlp_conv(x2, g_pre, w_p_all, layer, s_len, wall, sb, lng, lnb, cw, cb, clg, clb, cpw, cpb, t_len, tm=256, halo=32):
    n, d = x2.shape
    w = W_GRP
    row = lambda a: a.reshape(1, w)
    full = lambda a: pl.BlockSpec(a.shape, lambda i: (0,) * a.ndim)
    cwp = jnp.concatenate([cw, jnp.zeros((32 - CONV_WIDTH, w), F32)], axis=0)
    args = (wall, sb, row(lng), row(lnb), cwp, row(cb), row(clg), row(clb), cpw.astype(BF16), row(cpb))
    widths = (NSA_W, RW_W, w, w)
    return pl.pallas_call(
        functools.partial(_proj_gc_kernel, tiles_per_seq=s_len // tm, t_len=t_len),
        grid=(n // tm,),
        in_specs=[pl.BlockSpec((tm, d), lambda i: (i, 0)),
                  pl.BlockSpec((1, d), lambda i: (0, 0)),
                  pl.BlockSpec((None, d, N_PROJ), lambda i: (layer, 0, 0))] + [full(a) for a in args],
        out_specs=[pl.BlockSpec((tm, wd), lambda i: (i, 0)) for wd in widths],
        out_shape=[jax.ShapeDtypeStruct((n, wd), F32) for wd in widths],
        scratch_shapes=[pltpu.VMEM((halo + tm + SUBLANES, w), F32),
                        pltpu.VMEM((SUBLANES - 1, halo + tm, w), F32)],
        compiler_params=_cparams(("arbitrary",)),
        name="in_proj_gmlp_conv",
    )(x2, g_pre.reshape(1, d), w_p_all, *args)


def _rwkv_kernel(rw_ref, prev_ref, mu_ref, wup_ref, aup_ref, vec_ref, o_ref, st):
    c = pl.program_id(0)
    n_b = rw_ref.shape[0]
    L = RW_CHUNK
    n_sub = rw_ref.shape[1] // L
    w = W_GRP
    nh = N_HEADS
    hd = HEAD_DIM

    @pl.when(c == 0)
    def _init():
        st[...] = jnp.zeros(st.shape, F32)

    vec = vec_ref[...]
    w0, a0, k_k, k_a, r_k, gn_g, gn_b = [vec[i:i + 1, :] for i in range(7)]
    mu = mu_ref[...]
    rows = lax.broadcasted_iota(jnp.int32, (L, w), 0)
    lane = lax.broadcasted_iota(jnp.int32, (L, w), 1)
    s_of = lane % hd
    ones_bd = jnp.where((lax.broadcasted_iota(jnp.int32, (w, w), 0) // hd)
                        == (lax.broadcasted_iota(jnp.int32, (w, w), 1) // hd), 1.0, 0.0)
    tri = jnp.where(lax.broadcasted_iota(jnp.int32, (L, L), 1) <= lax.broadcasted_iota(jnp.int32, (L, L), 0), 1.0, 0.0)
    bd_mask = (lax.broadcasted_iota(jnp.int32, (w, w), 0) // hd) == (lax.broadcasted_iota(jnp.int32, (w, w), 1) // hd)
    strict = s_of < rows
    incl = s_of <= rows
    eye_all = jnp.where(s_of == rows, 1.0, 0.0)

    def bd(x):
        xb = x.astype(BF16)
        return jnp.where(bd_mask, jnp.concatenate([xb] * nh, axis=0), jnp.zeros((), BF16))

    def mm(a, b):
        return jnp.dot(a.astype(BF16), b.astype(BF16), preferred_element_type=F32)

    def mm_nt(a, b):
        return _nt(a.astype(BF16), b.astype(BF16))

    nb = range(n_b)
    probs = range(n_b * n_sub)
    stack = lambda parts: jnp.concatenate(parts, axis=0)
    part = lambda x, p: x[p * L:(p + 1) * L]
    zs = [rw_ref[b] for b in nb]
    zrow = lax.broadcasted_iota(jnp.int32, zs[0].shape, 0)
    n_prev = prev_ref.shape[1]
    xs = []
    for b in nb:
        last = jnp.where(c == 0, 0.0, prev_ref[b, n_prev - 1:n_prev, :])
        zprev = jnp.where(zrow == 0, last, pltpu.roll(zs[b], 1, axis=0))
        xs.append(zs[b] + mu * (zprev - zs[b]))
    xs = stack(xs)
    r = xs[:, 0:w]
    k = xs[:, w:2 * w]
    v = xs[:, 2 * w:3 * w]
    wa = xs[:, 3 * w:3 * w + LANES]
    zd = stack([z[:, 3 * w + LANES:] for z in zs])

    zz = w0 + _dot3(jnp.tanh(wa), wup_ref[...])
    lw = (-math.exp(-0.5)) * _sigmoid(zz)
    a = _sigmoid(a0 + _dot3(wa, aup_ref[...]))
    kkr = k * k_k
    kk = kkr * lax.rsqrt(jnp.maximum(mm(kkr * kkr, ones_bd), 1e-24))
    k2 = k * (1.0 + (a - 1.0) * k_a)
    bb = kk * a

    lw_wide = jnp.concatenate([part(lw, p) for p in probs], axis=1)
    cs_wide = _dot_sel_lhs(tri, lw_wide)
    cs = stack([cs_wide[:, p * w:(p + 1) * w] for p in probs])
    g_t = jnp.exp(cs)
    g_inv = jnp.exp(-cs)
    kq = (kk * jnp.exp(cs - lw)).astype(BF16)
    rq = (r * g_t).astype(BF16)
    kt = k2 * g_inv
    bt = bb * g_inv

    lhs = [stack([part(kq, p), part(rq, p)]) for p in probs]
    ab_b = [_nt(lhs[p], bd(part(bt, p))) for p in probs]
    ab_k = [_nt(lhs[p], bd(part(kt, p))) for p in probs]
    a_b = [jnp.where(strict, x[:L], 0.0) for x in ab_b]
    b_b = [jnp.where(incl, x[L:], 0.0) for x in ab_b]
    ak_bk = [stack([jnp.where(strict, x[:L], 0.0), jnp.where(incl, x[L:], 0.0)]) for x in ab_k]
    akv = [mm(ak_bk[p], bd(part(v, p))) for p in probs]

    npow = a_b
    tinv = [eye_all - x for x in a_b]
    for i in range(int(math.log2(L)) - 1):
        npow = [mm(x, bd(x)) for x in npow]
        tinv = [t + mm(t, bd(x)) for t, x in zip(tinv, npow)]

    zpad = jnp.zeros((w - 2 * L, w), F32)
    y = [None] * len(probs)
    for sub in range(n_sub):
        ps = [b * n_sub + sub for b in nb]
        s0 = [st[b] for b in nb]
        kh = [_nt(lhs[p], s0[b].astype(BF16)) for b, p in zip(nb, ps)]
        u = [mm(tinv[p], bd(kh[b][:L] + akv[p][:L])) for b, p in zip(nb, ps)]
        for b, p in zip(nb, ps):
            y[p] = kh[b][L:] + akv[p][L:] - mm(b_b[p], bd(u[b]))
        for b, p in zip(nb, ps):
            vu_t = stack([part(v, p), u[b], zpad]).T
            kb = stack([part(kt, p), -part(bt, p), zpad])
            d = mm(vu_t, kb)
            st[b] = g_t[(p + 1) * L - 1:(p + 1) * L, :] * (s0[b] + jnp.where(bd_mask, d, 0.0))

    y = stack(y)
    y_hi, y_lo = _split2(y)
    mean = (mm(y_hi, ones_bd) + mm(y_lo, ones_bd)) * (1.0 / hd)
    yc = y - mean
    var = mm(yc * yc, ones_bd) * (1.0 / hd)
    yn = yc * lax.rsqrt(var + RWKV_GN_EPS) * gn_g + gn_b
    bonus = mm(r * k2 * r_k, ones_bd) * v
    out = (yn + bonus) * _silu(zd)
    for b in nb:
        o_ref[b] = out[b * n_sub * L:(b + 1) * n_sub * L]


def _rwkv(rw3, mu_p, wup_p, aup_p, vec):
    b, s_len, _ = rw3.shape
    rows = RW_CHUNK * RW_SUB
    w = W_GRP
    full = lambda a: pl.BlockSpec(a.shape, lambda c: (0,) * a.ndim)
    return pl.pallas_call(
        _rwkv_kernel,
        grid=(s_len // rows,),
        in_specs=[pl.BlockSpec((b, rows, RW_W), lambda c: (0, c, 0)),
                  pl.BlockSpec((b, SUBLANES, RW_W), lambda c: (0, jnp.maximum(c * (rows // SUBLANES) - 1, 0), 0)),
                  full(mu_p), full(wup_p), full(aup_p), full(vec)],
        out_specs=pl.BlockSpec((b, rows, w), lambda c: (0, c, 0)),
        out_shape=jax.ShapeDtypeStruct((b, s_len, w), F32),
        scratch_shapes=[pltpu.VMEM((b, w, w), F32)],
        compiler_params=_cparams(("arbitrary",)),
        name="rwkv7",
    )(rw3, rw3, mu_p, wup_p, aup_p, vec)


def _out_kernel(ya_ref, yb_ref, yc_ref, yd_ref, x_ref, p_ref, wo_ref, gp_ref, pp_ref, pg_ref, o_ref):
    w = W_GRP
    acc = None
    for i, y in enumerate((ya_ref, yb_ref, yc_ref, yd_ref)):
        t = jnp.dot(y[...].astype(BF16), wo_ref[i * w:(i + 1) * w, :], preferred_element_type=F32)
        acc = t if acc is None else acc + t
    ms = jnp.mean(acc * acc, axis=-1, keepdims=True)
    x1 = x_ref[...] + acc * lax.rsqrt(ms + NORM_EPS) * gp_ref[...]
    gate = _sigmoid(jnp.dot(x1.astype(BF16), pg_ref[...], preferred_element_type=F32))
    pe = jnp.dot(p_ref[...].astype(BF16), pp_ref[...], preferred_element_type=F32)
    o_ref[...] = x1 + gate * pe


def _out_proj(ys, x2, p3, layer, wo_all, g_post, pp_all, pg_all, tm=512):
    n, d = x2.shape
    w = W_GRP
    gp = g_post.reshape(1, d)
    of_layer = lambda a: pl.BlockSpec((None,) + a.shape[1:], lambda i: (layer,) + (0,) * (a.ndim - 1))
    return pl.pallas_call(
        _out_kernel,
        grid=(n // tm,),
        in_specs=[pl.BlockSpec((tm, w), lambda i: (i, 0))] * 4
                 + [pl.BlockSpec((tm, d), lambda i: (i, 0)),
                    pl.BlockSpec((None, tm, p3.shape[2]), lambda i: (layer, i, 0)),
                    of_layer(wo_all), pl.BlockSpec(gp.shape, lambda i: (0, 0)), of_layer(pp_all), of_layer(pg_all)],
        out_specs=pl.BlockSpec((tm, d), lambda i: (i, 0)),
        out_shape=jax.ShapeDtypeStruct((n, d), F32),
        compiler_params=_cparams(("parallel",)),
        name="out_proj",
    )(*ys, x2, p3, wo_all, gp, pp_all, pg_all)


def _prep_w_kernel(w_ref, o_ref):
    wv = w_ref[...]
    rows = wv.shape[0]
    o_ref[...] = jnp.concatenate([jnp.zeros((rows, b_), F32) if a_ is None else wv[:, a_:b_]
                                  for a_, b_ in _SEGMENTS], axis=1).astype(BF16)


def _prep_w_in(w_in, tr=128):
    depth, d, n_in = w_in.shape
    return pl.pallas_call(
        _prep_w_kernel,
        grid=(depth, d // tr),
        in_specs=[pl.BlockSpec((None, tr, n_in), lambda l, r: (l, r, 0))],
        out_specs=pl.BlockSpec((None, tr, N_PROJ), lambda l, r: (l, r, 0)),
        out_shape=jax.ShapeDtypeStruct((depth, d, N_PROJ), BF16),
        compiler_params=_cparams(("parallel", "parallel")),
        name="prep_w_in",
    )(w_in)


def _layer(x2, p3, layer, b, s_len, bt, pt, imat, w_p_all, wo_all, pp_all, pg_all,
           g_pre, g_post, nsa_pos, nsa_w1, nsa_w2,
           sgu_ln_g, sgu_ln_b, sgu_w, sgu_b, conv_w, conv_b, conv_ln_g, conv_ln_b, conv_pw, conv_pw_b,
           rwkv_mu, rwkv_w0, rwkv_w_up, rwkv_a0, rwkv_a_up, rwkv_k_k, rwkv_k_a, rwkv_r_k,
           rwkv_gn_g, rwkv_gn_b):
    n = x2.shape[0]
    w = W_GRP
    t_len = sgu_w.shape[-1]
    wall = jnp.transpose(sgu_w, (1, 0, 2)).reshape(t_len, N_HEADS * t_len)
    sb = jnp.repeat(sgu_b.T, HEAD_DIM, axis=1)
    nsa, rw, yb, yc = _in_proj_gmlp_conv(x2, g_pre, w_p_all, layer, s_len, wall, sb, sgu_ln_g, sgu_ln_b, conv_w, conv_b,
                                         conv_ln_g, conv_ln_b, conv_pw, conv_pw_b, t_len)

    nsa3 = nsa.reshape(b, s_len, NSA_W)
    kc, vct = _nsa_compress(nsa3, nsa_pos, nsa_w1, nsa_w2)
    ya = _nsa_attention(nsa3, kc, vct, bt, pt, imat).reshape(n, w)

    lora = RWKV_LORA
    mu_p = jnp.concatenate([rwkv_mu, jnp.zeros((RW_W - rwkv_mu.shape[0],), F32)]).reshape(1, RW_W)
    wup_p = jnp.concatenate([rwkv_w_up, jnp.zeros((LANES - lora, w), F32)], axis=0)
    aup_p = jnp.concatenate([jnp.zeros((lora, w), F32), rwkv_a_up, jnp.zeros((LANES - 2 * lora, w), F32)], axis=0)
    vec = jnp.stack([rwkv_w0, rwkv_a0, rwkv_k_k, rwkv_k_a, rwkv_r_k.reshape(w), rwkv_gn_g, rwkv_gn_b,
                     jnp.zeros((w,), F32)], axis=0)
    yd = _rwkv(rw.reshape(b, s_len, RW_W), mu_p, wup_p, aup_p, vec).reshape(n, w)

    return _out_proj((ya, yb, yc, yd), x2, p3, layer, wo_all, g_post, pp_all, pg_all)


def kernel(x, p, rel_bias, w_in, w_out, g_pre, g_post, nsa_pos, nsa_w1, nsa_w2, sgu_ln_g, sgu_ln_b, sgu_w, sgu_b, conv_w, conv_b, conv_ln_g, conv_ln_b, conv_pw, conv_pw_b, rwkv_mu, rwkv_w0, rwkv_w_up, rwkv_a0, rwkv_a_up, rwkv_k_k, rwkv_k_a, rwkv_r_k, rwkv_gn_g, rwkv_gn_b, ple_proj, ple_gate):
    b, s_len, d = x.shape
    depth = w_in.shape[0]
    bt, pt, imat = _nsa_tables(rel_bias, s_len)
    x2 = x.reshape(b * s_len, d)
    p3 = p.reshape(depth, b * s_len, p.shape[-1])
    w_p_all = _prep_w_in(w_in)
    wo_all, pp_all, pg_all = w_out.astype(BF16), ple_proj.astype(BF16), ple_gate.astype(BF16)
    per_layer = (g_pre, g_post, nsa_pos, nsa_w1, nsa_w2, sgu_ln_g, sgu_ln_b, sgu_w, sgu_b,
                 conv_w, conv_b, conv_ln_g, conv_ln_b, conv_pw, conv_pw_b, rwkv_mu, rwkv_w0, rwkv_w_up,
                 rwkv_a0, rwkv_a_up, rwkv_k_k, rwkv_k_a, rwkv_r_k, rwkv_gn_g, rwkv_gn_b)
    for i in range(depth):
        x2 = _layer(x2, p3, i, b, s_len, bt, pt, imat, w_p_all, wo_all, pp_all, pg_all,
                    *[a[i] for a in per_layer])
    return x2.reshape(b, s_len, d)
```

```python
import functools
import math

import numpy as np
import jax
import jax.numpy as jnp
from jax import lax
from jax.experimental import pallas as pl
from jax.experimental.pallas import tpu as pltpu

F32 = jnp.float32
BF16 = jnp.bfloat16

W_GRP = 256
HEAD_DIM = 64
N_HEADS = 4
NSA_DK = 64
CMP_STRIDE = 16
CMP_BLOCK = 32
CMP_HIDDEN = 128
SEL_BLOCK = 64
N_SEL = 16
WINDOW = 512
REL_BUCKETS = 32
REL_MAX_EXACT = 16
REL_MAX_DIST = 128
CONV_WIDTH = 31
RWKV_LORA = 32
RWKV_GN_EPS = 64e-5
NORM_EPS = 1e-6
LN_EPS = 1e-5
NEG = -1e30
FORCE = 1e4
LOG2E = math.log2(math.e)

LANES = 128
SUBLANES = 8
Q_TILE = 128
SEL_GROUP = 4
NSA_STREAMS = 4
RW_CHUNK = 64
RW_SUB = 4
VMEM_LIMIT = 48 * 1024 * 1024
NSA_VMEM_LIMIT = 56 * 1024 * 1024

NSA_W, GM_W, CV_W, RW_W = 1024, 768, 768, 1152
N_PROJ = NSA_W + GM_W + CV_W + RW_W


def _proj_segments():
    names = ["q", "kc", "vc", "ks", "vs", "kw", "vw", "g", "za", "u", "v", "zb", "ga", "gb", "zc", "rw", "zd"]
    widths = [256, 64, 64, 64, 64, 64, 64, 12, 256, 256, 256, 256, 256, 256, 256, 832, 256]
    o, off = {}, 0
    for n, w in zip(names, widths):
        o[n] = (off, off + w)
        off += w
    return [o["q"], o["za"], (o["ks"][0], o["vw"][1]), (o["kc"][0], o["vc"][1]), o["g"], (None, 116),
            (o["u"][0], o["zc"][1]), o["rw"], (None, 64), o["zd"]]


_SEGMENTS = _proj_segments()


def _rel_buckets(n):
    d = np.arange(n)
    nf = np.maximum(d, REL_MAX_EXACT).astype(np.float32)
    large = REL_MAX_EXACT + (np.log(nf / np.float32(REL_MAX_EXACT)) / np.float32(math.log(REL_MAX_DIST / REL_MAX_EXACT))
                             * np.float32(REL_BUCKETS - REL_MAX_EXACT)).astype(np.int32)
    large = np.minimum(large, REL_BUCKETS - 1)
    return np.where(d < REL_MAX_EXACT, d, large)


def _bdot(a, b):
    return jnp.dot(a.astype(BF16), b.astype(BF16), preferred_element_type=F32)


def _nt(a, b):
    return lax.dot_general(a, b, (((1,), (1,)), ((), ())), preferred_element_type=F32)


def _split2(a):
    hi = a.astype(BF16)
    lo = (a - hi.astype(F32)).astype(BF16)
    return hi, lo


def _split3(a):
    hi = a.astype(BF16)
    r = a - hi.astype(F32)
    mid = r.astype(BF16)
    lo = (r - mid.astype(F32)).astype(BF16)
    return hi, mid, lo


def _dot3(a, b):
    ah, al = _split2(a)
    bh, bl = _split2(b)
    d = lambda x, y: jnp.dot(x, y, preferred_element_type=F32)
    return d(ah, bh) + (d(ah, bl) + d(al, bh))


def _dot3_nt(a, b):
    ah, al = _split2(a)
    bh, bl = _split2(b)
    return _nt(ah, bh) + (_nt(ah, bl) + _nt(al, bh))


def _dot_sel_lhs(a01, b):
    a = a01.astype(BF16)
    bh, bm, bl = _split3(b)
    d = lambda y: jnp.dot(a, y, preferred_element_type=F32)
    return d(bh) + (d(bm) + d(bl))


def _sigmoid(x):
    return 1.0 / (1.0 + jnp.exp(-x))


def _silu(x):
    return x * _sigmoid(x)


def _gelu_tanh(x):
    c = math.sqrt(2.0 / math.pi)
    return 0.5 * x * (1.0 + jnp.tanh(c * (x + 0.044715 * (x * x * x))))


def _layer_norm(x, g, b):
    mu = jnp.mean(x, axis=-1, keepdims=True)
    xc = x - mu
    var = jnp.mean(xc * xc, axis=-1, keepdims=True)
    return xc * lax.rsqrt(var + LN_EPS) * g + b


def _block_diag(x, n):
    r, c = x.shape
    t = jnp.concatenate([x] * n, axis=0)
    ri = lax.broadcasted_iota(jnp.int32, t.shape, 0) // r
    ci = lax.broadcasted_iota(jnp.int32, t.shape, 1) // (c // n)
    return jnp.where(ri == ci, t, 0.0)


def _cparams(sem, vmem=VMEM_LIMIT):
    return pltpu.CompilerParams(dimension_semantics=sem, vmem_limit_bytes=vmem)


def _cmp_kernel(kv_ref, pos_ref, w1_ref, w2_ref, o_ref, ot_ref):
    nc = o_ref.shape[0]
    dk = o_ref.shape[1]
    first = None
    second = None
    for r in range(CMP_STRIDE):
        x = kv_ref[pl.ds(r, nc, stride=CMP_STRIDE), :]
        t = _dot3(x + pos_ref[r], w1_ref[r])
        first = t if first is None else first + t
        t = _dot3(x + pos_ref[CMP_STRIDE + r], w1_ref[CMP_STRIDE + r])
        second = t if second is None else second + t
    hid = first + pltpu.roll(second, nc - 1, axis=0)
    out = _dot3(_silu(hid), w2_ref[...])
    o_ref[...] = out[:, :dk]
    ot_ref[...] = out.T[dk:, :]


def _nsa_compress(nsa3, pos, w1, w2):
    b, s_len, _ = nsa3.shape
    nc = s_len // CMP_STRIDE
    dk = NSA_DK
    zw = jnp.zeros((CMP_BLOCK, dk, CMP_HIDDEN), F32)
    w1r = w1.reshape(2, CMP_BLOCK, dk, CMP_HIDDEN)
    w1bd = jnp.concatenate([jnp.concatenate([w1r[0], zw], axis=2),
                            jnp.concatenate([zw, w1r[1]], axis=2)], axis=1)
    z2 = jnp.zeros((CMP_HIDDEN, dk), F32)
    w2bd = jnp.concatenate([jnp.concatenate([w2[0], z2], axis=1),
                            jnp.concatenate([z2, w2[1]], axis=1)], axis=0)
    posr = jnp.concatenate([pos[0], pos[1]], axis=1).reshape(CMP_BLOCK, 1, 2 * dk)
    full = lambda a: pl.BlockSpec(a.shape, lambda i: (0,) * a.ndim)
    return pl.pallas_call(
        _cmp_kernel,
        grid=(b,),
        in_specs=[pl.BlockSpec((None, s_len, LANES), lambda i: (i, 0, 6)),
                  full(posr), full(w1bd), full(w2bd)],
        out_specs=[pl.BlockSpec((None, nc, dk), lambda i: (i, 0, 0)),
                   pl.BlockSpec((None, dk, nc), lambda i: (i, 0, 0))],
        out_shape=[jax.ShapeDtypeStruct((b, nc, dk), F32),
                   jax.ShapeDtypeStruct((b, dk, nc), F32)],
        compiler_params=_cparams(("parallel",)),
        name="nsa_compress",
    )(nsa3, posr, w1bd, w2bd)


def _nsa_kernel(q_ref, za_ref, g_ref, ks_ref, kw_ref, kc_ref, vct_ref, pt_ref, bt_ref, imat_ref,
                      o_ref, ksb, vst, kwb, vwt, madd, val_s, cnt_s, pc_s, yc_s, m_s, acc_s, *, n_top, n_win):
    c = pl.program_id(1)
    n_qt = pl.num_programs(1)
    n_st = q_ref.shape[0]
    n_kt = ksb.shape[1]
    hd = HEAD_DIM
    nh = N_HEADS
    n_blk = imat_ref.shape[0]
    per_q = Q_TILE // SEL_BLOCK
    v_rows = vst.shape[2]
    streams = range(n_st)

    @pl.when(c == 0)
    def _prep():
        key_blk = lax.broadcasted_iota(jnp.int32, (Q_TILE, n_blk), 0) // SEL_BLOCK
        col = lax.broadcasted_iota(jnp.int32, (Q_TILE, n_blk), 1)
        ones_row = jnp.where(lax.broadcasted_iota(jnp.int32, (v_rows - hd, Q_TILE), 0) == 0, 1.0, 0.0)

        def body(i, carry):
            r0 = pl.multiple_of(i * Q_TILE, Q_TILE)
            onehot = jnp.where(col == key_blk + per_q * i, 1.0, 0.0)
            for s in streams:
                t = ks_ref[s, pl.ds(r0, Q_TILE), :]
                ksb[s, i] = jnp.concatenate([t[:, :hd], onehot], axis=1).astype(BF16)
                vst[s, i] = jnp.concatenate([t.T[hd:, :], ones_row], axis=0).astype(BF16)
                t = kw_ref[s, pl.ds(r0, Q_TILE), :]
                kwb[s, i] = t[:, :hd].astype(BF16)
                vwt[s, i] = jnp.concatenate([t.T[hd:, :], ones_row], axis=0).astype(BF16)
            return carry
        lax.fori_loop(0, n_kt, body, 0)

    def reset_state():
        m_s[...] = jnp.full(m_s.shape, NEG, F32)
        acc_s[...] = jnp.zeros(acc_s.shape, F32)

    def update(groups):
        m_old = [m_s[s] for s in streams]
        m_new = []
        for s in streams:
            m = m_old[s]
            for s_i in groups[s][0]:
                m = jnp.maximum(m, jnp.max(s_i, axis=0, keepdims=True))
            m_new.append(m)
        for s in streams:
            ss, vs = groups[s]
            pv = None
            for j in range(0, len(ss), 2):
                p_j = jnp.concatenate([jnp.exp2(s_i - m_new[s]).astype(BF16) for s_i in ss[j:j + 2]], axis=0)
                v_j = jnp.concatenate(vs[j:j + 2], axis=1)
                t = jnp.dot(v_j, p_j, preferred_element_type=F32)
                pv = t if pv is None else pv + t
            acc_s[s] = jnp.exp2(m_old[s] - m_new[s]) * acc_s[s] + pv
            m_s[s] = m_new[s]

    def branch_out(s, g):
        acc = acc_s[s]
        return (g / acc[hd:hd + 1, :]) * acc[:hd, :]

    q_all, qb, gate = [], [], []
    for s in streams:
        q = q_ref[s] * (hd ** -0.5 * LOG2E)
        qa = jnp.concatenate([q[:, hd * h:hd * (h + 1)] for h in range(nh)], axis=0)
        q_all.append(qa)
        qb.append(qa.astype(BF16))
        g_t = _sigmoid(g_ref[s]).T
        gate.append([jnp.concatenate([g_t[3 * h + br:3 * h + br + 1, :] for h in range(nh)], axis=1)
                     for br in range(3)])

    reset_state()
    groups = []
    for s in streams:
        ss, vts = [], []
        for i in range(n_win + 1):
            kt = jnp.maximum(c - i, 0)
            s_i = _nt(kwb[s, kt], qb[s])
            if i == 0:
                s_i = s_i + bt_ref[0]
            else:
                if i == 1:
                    s_i = s_i + bt_ref[1]
                elif i == n_win:
                    s_i = s_i + bt_ref[2]
                s_i = s_i + jnp.where(c >= i, 0.0, NEG)
            ss.append(s_i)
            vts.append(vwt[s, kt])
        groups.append((ss, vts))
    update(groups)
    y_acc = [branch_out(s, gate[s][2]) for s in streams]

    n_c = kc_ref.shape[1]
    per_tile = Q_TILE // CMP_STRIDE
    off = pl.multiple_of((n_qt - 1 - c) * per_tile, SUBLANES)

    def cmp_branch(rows):
        bias = pt_ref[pl.ds(off, rows), :]
        valid = bias > 0.5 * NEG
        scs = [_dot3_nt(kc_ref[s, 0:rows, :], q_all[s]) + bias for s in streams]
        ps = [jnp.where(valid, jnp.exp2(sc - jnp.max(sc, axis=0, keepdims=True)), 0.0) for sc in scs]
        ls = [jnp.sum(p, axis=0, keepdims=True) for p in ps]
        pns = [p * jnp.where(l > 0.0, 1.0 / l, 0.0) for p, l in zip(ps, ls)]
        for s in streams:
            yc_s[s] = gate[s][0] * _bdot(vct_ref[s, :, 0:rows], pns[s])
            pc = pns[s][:, 0:Q_TILE]
            for h in range(1, nh):
                pc = pc + pns[s][:, h * Q_TILE:(h + 1) * Q_TILE]
            pc_s[s, 0:rows, :] = pc
            if rows < n_c:
                pc_s[s, rows:n_c, :] = jnp.zeros((n_c - rows, Q_TILE), F32)

    half = n_c // 2
    few = (c + 1) * per_tile <= half
    pl.when(few)(lambda: cmp_branch(half))
    pl.when(jnp.logical_not(few))(lambda: cmp_branch(n_c))

    tl = lax.broadcasted_iota(jnp.int32, (n_blk, Q_TILE), 1)
    jb = lax.broadcasted_iota(jnp.int32, (n_blk, Q_TILE), 0)
    cur = c * per_q + tl // SEL_BLOCK
    causal = jb <= cur
    need_rank = (c + 1) * per_q > n_top

    @pl.when(jnp.logical_not(need_rank))
    def _all_causal():
        for s in streams:
            madd[s] = jnp.where(causal, 0.0, NEG)

    @pl.when(need_rank)
    def _rank():
        forced = (jb == 0) | (jb == cur) | (jb == cur - 1)
        n_r = n_blk // SUBLANES
        jrow = lax.broadcasted_iota(jnp.int32, (SUBLANES, Q_TILE), 0)
        for s in streams:
            imp = _dot_sel_lhs(imat_ref[...], pc_s[s])
            val_s[s] = jnp.where(causal, jnp.where(forced, FORCE, imp), NEG)
            cnt_s[s] = jnp.zeros((n_blk, Q_TILE), F32)
        for ci in range(n_r):
            @pl.when(SUBLANES * ci < (c + 1) * per_q)
            def _chunk():
                for s in streams:
                    val = val_s[s]
                    blocks = [val[SUBLANES * r:SUBLANES * (r + 1), :] for r in range(n_r)]
                    cnts = [None] * n_r
                    for i in range(SUBLANES * ci, SUBLANES * (ci + 1)):
                        vi = jnp.broadcast_to(val[i:i + 1, :], (SUBLANES, Q_TILE))
                        for r in range(n_r):
                            if r > ci:
                                t = jnp.where(vi >= blocks[r], 1.0, 0.0)
                            elif r < ci:
                                t = jnp.where(vi > blocks[r], 1.0, 0.0)
                            else:
                                ge = jnp.where(vi >= blocks[r], 1.0, 0.0)
                                gt = jnp.where(vi > blocks[r], 1.0, 0.0)
                                t = jnp.where(jrow + SUBLANES * r > i, ge, gt)
                            cnts[r] = t if cnts[r] is None else cnts[r] + t
                    cnt_s[s] = cnt_s[s] + jnp.concatenate(cnts, axis=0)
        for s in streams:
            madd[s] = jnp.where(causal & (cnt_s[s] < float(n_top)), 0.0, NEG)

    reset_state()
    pad = jnp.zeros((LANES - n_blk, Q_TILE), F32)
    kt1 = jnp.maximum(c - 1, 0)
    q_far, groups = [], []
    for s in streams:
        m_t = jnp.concatenate([madd[s], pad], axis=0).T[:, :n_blk]
        blk_lane = lax.broadcasted_iota(jnp.int32, m_t.shape, 1)
        m_far = jnp.where(blk_lane < per_q * (c - 1), m_t, NEG)
        q_near = jnp.concatenate([q_all[s], jnp.concatenate([m_t] * nh, axis=0)], axis=1).astype(BF16)
        q_far.append(jnp.concatenate([q_all[s], jnp.concatenate([m_far] * nh, axis=0)], axis=1).astype(BF16))
        s0 = _nt(ksb[s, c], q_near) + bt_ref[0]
        s1 = _nt(ksb[s, kt1], q_near) + bt_ref[1] + jnp.where(c >= 1, 0.0, NEG)
        groups.append(([s0, s1], [vst[s, c], vst[s, kt1]]))

    update(groups)

    n_far = jnp.maximum(c - 1, 0)

    def far_body(g, carry):
        kts = [g * SEL_GROUP + j for j in range(SEL_GROUP)]
        update([([_nt(ksb[s, kt], q_far[s]) for kt in kts], [vst[s, kt] for kt in kts]) for s in streams])
        return carry
    lax.fori_loop(0, (n_far + SEL_GROUP - 1) // SEL_GROUP, far_body, 0)

    for s in streams:
        y = y_acc[s] + yc_s[s] + branch_out(s, gate[s][1])
        y_t = jnp.concatenate([y[:, h * Q_TILE:(h + 1) * Q_TILE] for h in range(nh)], axis=0)
        o_ref[s] = y_t.T * _silu(za_ref[s])


def _bucket_lookup(relc, idx, visible):
    r, cc = idx.shape
    onehot = (jnp.asarray(idx.reshape(1, -1)) == jnp.arange(relc.shape[0])[:, None]).astype(F32)
    vals = jnp.einsum("bh,bn->hn", relc, onehot, precision=lax.Precision.HIGHEST).reshape(-1, r, cc)
    vals = jnp.where(jnp.asarray(visible)[None], vals, NEG)
    return jnp.transpose(vals, (1, 0, 2)).reshape(r, -1).astype(F32)


def _nsa_tables(rel_bias, s_len):
    n_qt = s_len // Q_TILE
    n_c = s_len // CMP_STRIDE
    relc = rel_bias - rel_bias[REL_BUCKETS - 1][None, :]
    bk = _rel_buckets(s_len + Q_TILE)
    kl = np.arange(Q_TILE)[:, None]
    tq = np.arange(Q_TILE)[None, :]
    d0 = tq - kl
    diag = _bucket_lookup(relc, bk[np.clip(d0, 0, None)], d0 >= 0)
    prev = _bucket_lookup(relc, bk[Q_TILE + d0], np.ones_like(d0, bool))
    edge = jnp.asarray(np.tile(np.where(kl > tq, 0.0, NEG).astype(np.float32), (1, N_HEADS)))
    bt = jnp.stack([diag, prev, edge], axis=0) * LOG2E
    per_q = Q_TILE // CMP_STRIDE
    n_rows = n_c + per_q * (n_qt - 1)
    r = np.arange(n_rows)[:, None]
    dc = tq - CMP_STRIDE * (r - per_q * (n_qt - 1)) - (CMP_BLOCK - 1)
    pt = _bucket_lookup(relc, bk[np.clip(dc, 0, None)], dc >= 0) * LOG2E
    n_blk = s_len // SEL_BLOCK
    ratio = SEL_BLOCK // CMP_STRIDE
    jj = np.arange(n_blk)[:, None]
    ii = np.arange(n_c)[None, :]
    imat = ((ii >= ratio * jj - 1) & (ii <= ratio * jj + ratio - 1) & (ii < n_c - 1)).astype(np.float32)
    return bt, pt, jnp.asarray(imat)


def _nsa_attention(nsa3, kc, vct, bt, pt, imat):
    b, s_len, _ = nsa3.shape
    n_qt = s_len // Q_TILE
    n_c = s_len // CMP_STRIDE
    n_blk = s_len // SEL_BLOCK
    n_win = WINDOW // Q_TILE
    hd = HEAD_DIM
    hq = N_HEADS * Q_TILE
    v_rows = hd + 2 * SUBLANES
    n_st = NSA_STREAMS if b % NSA_STREAMS == 0 else 1
    kern = functools.partial(_nsa_kernel, n_top=min(N_SEL, n_blk), n_win=n_win)
    return pl.pallas_call(
        kern,
        grid=(b // n_st, n_qt),
        in_specs=[pl.BlockSpec((n_st, Q_TILE, 256), lambda i, c: (i, c, 0)),
                  pl.BlockSpec((n_st, Q_TILE, 256), lambda i, c: (i, c, 1)),
                  pl.BlockSpec((n_st, Q_TILE, LANES), lambda i, c: (i, c, 7)),
                  pl.BlockSpec((n_st, s_len, LANES), lambda i, c: (i, 0, 4),
                               pipeline_mode=pl.Buffered(1)),
                  pl.BlockSpec((n_st, s_len, LANES), lambda i, c: (i, 0, 5),
                               pipeline_mode=pl.Buffered(1)),
                  pl.BlockSpec((n_st, n_c, hd), lambda i, c: (i, 0, 0)),
                  pl.BlockSpec((n_st, hd, n_c), lambda i, c: (i, 0, 0)),
                  pl.BlockSpec(pt.shape, lambda i, c: (0, 0)),
                  pl.BlockSpec(bt.shape, lambda i, c: (0, 0, 0)),
                  pl.BlockSpec(imat.shape, lambda i, c: (0, 0))],
        out_specs=pl.BlockSpec((n_st, Q_TILE, 256), lambda i, c: (i, c, 0)),
        out_shape=jax.ShapeDtypeStruct((b, s_len, 256), F32),
        scratch_shapes=[pltpu.VMEM((n_st, n_qt, Q_TILE, hd + n_blk), BF16),
                        pltpu.VMEM((n_st, n_qt, v_rows, Q_TILE), BF16),
                        pltpu.VMEM((n_st, n_qt, Q_TILE, hd), BF16),
                        pltpu.VMEM((n_st, n_qt, v_rows, Q_TILE), BF16),
                        pltpu.VMEM((n_st, n_blk, Q_TILE), F32),
                        pltpu.VMEM((n_st, n_blk, Q_TILE), F32),
                        pltpu.VMEM((n_st, n_blk, Q_TILE), F32),
                        pltpu.VMEM((n_st, n_c, Q_TILE), F32),
                        pltpu.VMEM((n_st, hd, hq), F32),
                        pltpu.VMEM((n_st, 1, hq), F32),
                        pltpu.VMEM((n_st, v_rows, hq), F32)],
        compiler_params=_cparams(("arbitrary", "arbitrary"), NSA_VMEM_LIMIT),
        name="nsa_attention",
    )(nsa3, nsa3, nsa3, nsa3, nsa3, kc, vct, pt, bt, imat)


def _proj_gc_kernel(x_ref, g_ref, w_ref, wall_ref, sb_ref, lng_ref, lnb_ref, cw_ref, cb_ref, clg_ref, clb_ref,
                    cpw_ref, cpb_ref, o_nsa, o_rw, ob_ref, oc_ref, xs, ph_s, *, tiles_per_seq, t_len):
    first = (pl.program_id(0) % tiles_per_seq) == 0
    tm = x_ref.shape[0]
    halo = xs.shape[0] - tm - SUBLANES
    w = W_GRP
    x = x_ref[...]
    ms = jnp.mean(x * x, axis=-1, keepdims=True)
    h = (x * lax.rsqrt(ms + NORM_EPS) * g_ref[...]).astype(BF16)
    c_gm, c_cv, c_rw = NSA_W, NSA_W + GM_W, NSA_W + GM_W + CV_W
    gm = jnp.dot(h, w_ref[:, c_gm:c_cv], preferred_element_type=F32)
    cv = jnp.dot(h, w_ref[:, c_cv:c_rw], preferred_element_type=F32)

    u = _gelu_tanh(gm[:, :w])
    v = _layer_norm(_gelu_tanh(gm[:, w:2 * w]), lng_ref[...], lnb_ref[...])
    zb = _silu(gm[:, 2 * w:])
    wall = wall_ref[...]
    ti = lax.broadcasted_iota(jnp.int32, wall.shape, 0)
    si = lax.broadcasted_iota(jnp.int32, wall.shape, 1) % t_len
    wall = jnp.where(si <= ti, wall, 0.0).astype(BF16)
    for ch in range(tm // t_len):
        r0, r1 = ch * t_len, (ch + 1) * t_len
        sv = jnp.dot(wall, _block_diag(v[r0:r1], N_HEADS).astype(BF16), preferred_element_type=F32) + sb_ref[...]
        ob_ref[r0:r1, :] = u[r0:r1] * sv * zb[r0:r1]

    o_nsa[...] = jnp.dot(h, w_ref[:, :c_gm], preferred_element_type=F32)

    o_rw[...] = jnp.dot(h, w_ref[:, c_rw:], preferred_element_type=F32)
    xs[0:halo, :] = jnp.where(first, 0.0, xs[0:halo, :])
    xs[halo:halo + tm, :] = cv[:, :w] * _sigmoid(cv[:, w:2 * w])
    xs[halo + tm:, :] = jnp.zeros((xs.shape[0] - halo - tm, w), F32)
    cw = cw_ref[...]
    for ph in range(1, SUBLANES):
        ph_s[ph - 1] = xs[ph:ph + halo + tm, :]
    base = halo - (CONV_WIDTH - 1)
    acc = jnp.zeros((tm, w), F32)
    for j in range(CONV_WIDTH):
        a, ph = divmod(base + j, SUBLANES)
        r0 = SUBLANES * a
        win = xs[r0:r0 + tm, :] if ph == 0 else ph_s[ph - 1, r0:r0 + tm, :]
        acc = acc + cw[j:j + 1, :] * win
    xs[0:halo, :] = xs[tm:tm + halo, :]
    y = _layer_norm(acc + cb_ref[...], clg_ref[...], clb_ref[...])
    y = _bdot(_silu(y), cpw_ref[...]) + cpb_ref[...]
    oc_ref[...] = y * _silu(cv[:, 2 * w:])


def _in_proj_gmlp_conv(x2, g_pre, w_p_all, layer, s_len, wall, sb, lng, lnb, cw, cb, clg, clb, cpw, cpb, t_len, tm=256, halo=32):
    n, d = x2.shape
    w = W_GRP
    row = lambda a: a.reshape(1, w)
    full = lambda a: pl.BlockSpec(a.shape, lambda i: (0,) * a.ndim)
    cwp = jnp.concatenate([cw, jnp.zeros((32 - CONV_WIDTH, w), F32)], axis=0)
    args = (wall, sb, row(lng), row(lnb), cwp, row(cb), row(clg), row(clb), cpw.astype(BF16), row(cpb))
    widths = (NSA_W, RW_W, w, w)
    return pl.pallas_call(
        functools.partial(_proj_gc_kernel, tiles_per_seq=s_len // tm, t_len=t_len),
        grid=(n // tm,),
        in_specs=[pl.BlockSpec((tm, d), lambda i: (i, 0)),
                  pl.BlockSpec((1, d), lambda i: (0, 0)),
                  pl.BlockSpec((None, d, N_PROJ), lambda i: (layer, 0, 0))] + [full(a) for a in args],
        out_specs=[pl.BlockSpec((tm, wd), lambda i: (i, 0)) for wd in widths],
        out_shape=[jax.ShapeDtypeStruct((n, wd), F32) for wd in widths],
        scratch_shapes=[pltpu.VMEM((halo + tm + SUBLANES, w), F32),
                        pltpu.VMEM((SUBLANES - 1, halo + tm, w), F32)],
        compiler_params=_cparams(("arbitrary",)),
        name="in_proj_gmlp_conv",
    )(x2, g_pre.reshape(1, d), w_p_all, *args)


def _rwkv_kernel(rw_ref, prev_ref, mu_ref, wup_ref, aup_ref, vec_ref, o_ref, st):
    c = pl.program_id(0)
    n_b = rw_ref.shape[0]
    L = RW_CHUNK
    n_sub = rw_ref.shape[1] // L
    w = W_GRP
    nh = N_HEADS
    hd = HEAD_DIM

    @pl.when(c == 0)
    def _init():
        st[...] = jnp.zeros(st.shape, F32)

    vec = vec_ref[...]
    w0, a0, k_k, k_a, r_k, gn_g, gn_b = [vec[i:i + 1, :] for i in range(7)]
    mu = mu_ref[...]
    rows = lax.broadcasted_iota(jnp.int32, (L, w), 0)
    lane = lax.broadcasted_iota(jnp.int32, (L, w), 1)
    s_of = lane % hd
    ones_bd = jnp.where((lax.broadcasted_iota(jnp.int32, (w, w), 0) // hd)
                        == (lax.broadcasted_iota(jnp.int32, (w, w), 1) // hd), 1.0, 0.0)
    tri = jnp.where(lax.broadcasted_iota(jnp.int32, (L, L), 1) <= lax.broadcasted_iota(jnp.int32, (L, L), 0), 1.0, 0.0)
    bd_mask = (lax.broadcasted_iota(jnp.int32, (w, w), 0) // hd) == (lax.broadcasted_iota(jnp.int32, (w, w), 1) // hd)
    strict = s_of < rows
    incl = s_of <= rows
    eye_all = jnp.where(s_of == rows, 1.0, 0.0)

    def bd(x):
        xb = x.astype(BF16)
        return jnp.where(bd_mask, jnp.concatenate([xb] * nh, axis=0), jnp.zeros((), BF16))

    def mm(a, b):
        return jnp.dot(a.astype(BF16), b.astype(BF16), preferred_element_type=F32)

    nb = range(n_b)
    probs = range(n_b * n_sub)
    stack = lambda parts: jnp.concatenate(parts, axis=0)
    part = lambda x, p: x[p * L:(p + 1) * L]
    zs = [rw_ref[b] for b in nb]
    zrow = lax.broadcasted_iota(jnp.int32, zs[0].shape, 0)
    n_prev = prev_ref.shape[1]
    xs = []
    for b in nb:
        last = jnp.where(c == 0, 0.0, prev_ref[b, n_prev - 1:n_prev, :])
        zprev = jnp.where(zrow == 0, last, pltpu.roll(zs[b], 1, axis=0))
        xs.append(zs[b] + mu * (zprev - zs[b]))
    xs = stack(xs)
    r = xs[:, 0:w]
    k = xs[:, w:2 * w]
    v = xs[:, 2 * w:3 * w]
    wa = xs[:, 3 * w:3 * w + LANES]
    zd = stack([z[:, 3 * w + LANES:] for z in zs])

    zz = w0 + _dot3(jnp.tanh(wa), wup_ref[...])
    lw = (-math.exp(-0.5)) * _sigmoid(zz)
    a = _sigmoid(a0 + _dot3(wa, aup_ref[...]))
    kkr = k * k_k
    kk = kkr * lax.rsqrt(jnp.maximum(mm(kkr * kkr, ones_bd), 1e-24))
    k2 = k * (1.0 + (a - 1.0) * k_a)
    bb = kk * a

    lw_wide = jnp.concatenate([part(lw, p) for p in probs], axis=1)
    cs_wide = _dot_sel_lhs(tri, lw_wide)
    cs = stack([cs_wide[:, p * w:(p + 1) * w] for p in probs])
    g_t = jnp.exp(cs)
    g_inv = jnp.exp(-cs)
    kq = (kk * jnp.exp(cs - lw)).astype(BF16)
    rq = (r * g_t).astype(BF16)
    kt = k2 * g_inv
    bt = bb * g_inv

    lhs = [stack([part(kq, p), part(rq, p)]) for p in probs]
    ab_b = [_nt(lhs[p], bd(part(bt, p))) for p in probs]
    ab_k = [_nt(lhs[p], bd(part(kt, p))) for p in probs]
    a_b = [jnp.where(strict, x[:L], 0.0) for x in ab_b]
    b_b = [jnp.where(incl, x[L:], 0.0) for x in ab_b]
    ak_bk = [stack([jnp.where(strict, x[:L], 0.0), jnp.where(incl, x[L:], 0.0)]) for x in ab_k]
    akv = [mm(ak_bk[p], bd(part(v, p))) for p in probs]

    npow = a_b
    tinv = [eye_all - x for x in a_b]
    for i in range(int(math.log2(L)) - 1):
        npow = [mm(x, bd(x)) for x in npow]
        tinv = [t + mm(t, bd(x)) for t, x in zip(tinv, npow)]

    zpad = jnp.zeros((w - 2 * L, w), F32)
    y = [None] * len(probs)
    for sub in range(n_sub):
        ps = [b * n_sub + sub for b in nb]
        s0 = [st[b] for b in nb]
        kh = [_nt(lhs[p], s0[b].astype(BF16)) for b, p in zip(nb, ps)]
        u = [mm(tinv[p], bd(kh[b][:L] + akv[p][:L])) for b, p in zip(nb, ps)]
        for b, p in zip(nb, ps):
            y[p] = kh[b][L:] + akv[p][L:] - mm(b_b[p], bd(u[b]))
        for b, p in zip(nb, ps):
            vu_t = stack([part(v, p), u[b], zpad]).T
            kb = stack([part(kt, p), -part(bt, p), zpad])
            d = mm(vu_t, kb)
            st[b] = g_t[(p + 1) * L - 1:(p + 1) * L, :] * (s0[b] + jnp.where(bd_mask, d, 0.0))

    y = stack(y)
    y_hi, y_lo = _split2(y)
    mean = (mm(y_hi, ones_bd) + mm(y_lo, ones_bd)) * (1.0 / hd)
    yc = y - mean
    var = mm(yc * yc, ones_bd) * (1.0 / hd)
    yn = yc * lax.rsqrt(var + RWKV_GN_EPS) * gn_g + gn_b
    bonus = mm(r * k2 * r_k, ones_bd) * v
    out = (yn + bonus) * _silu(zd)
    for b in nb:
        o_ref[b] = out[b * n_sub * L:(b + 1) * n_sub * L]


def _rwkv(rw3, mu_p, wup_p, aup_p, vec):
    b, s_len, _ = rw3.shape
    rows = RW_CHUNK * RW_SUB
    w = W_GRP
    full = lambda a: pl.BlockSpec(a.shape, lambda c: (0,) * a.ndim)
    return pl.pallas_call(
        _rwkv_kernel,
        grid=(s_len // rows,),
        in_specs=[pl.BlockSpec((b, rows, RW_W), lambda c: (0, c, 0)),
                  pl.BlockSpec((b, SUBLANES, RW_W), lambda c: (0, jnp.maximum(c * (rows // SUBLANES) - 1, 0), 0)),
                  full(mu_p), full(wup_p), full(aup_p), full(vec)],
        out_specs=pl.BlockSpec((b, rows, w), lambda c: (0, c, 0)),
        out_shape=jax.ShapeDtypeStruct((b, s_len, w), F32),
        scratch_shapes=[pltpu.VMEM((b, w, w), F32)],
        compiler_params=_cparams(("arbitrary",)),
        name="rwkv7",
    )(rw3, rw3, mu_p, wup_p, aup_p, vec)


def _out_kernel(ya_ref, yb_ref, yc_ref, yd_ref, x_ref, p_ref, wo_ref, gp_ref, pp_ref, pg_ref, o_ref):
    w = W_GRP
    acc = None
    for i, y in enumerate((ya_ref, yb_ref, yc_ref, yd_ref)):
        t = jnp.dot(y[...].astype(BF16), wo_ref[i * w:(i + 1) * w, :], preferred_element_type=F32)
        acc = t if acc is None else acc + t
    ms = jnp.mean(acc * acc, axis=-1, keepdims=True)
    x1 = x_ref[...] + acc * lax.rsqrt(ms + NORM_EPS) * gp_ref[...]
    gate = _sigmoid(jnp.dot(x1.astype(BF16), pg_ref[...], preferred_element_type=F32))
    pe = jnp.dot(p_ref[...].astype(BF16), pp_ref[...], preferred_element_type=F32)
    o_ref[...] = x1 + gate * pe


def _out_proj(ys, x2, p3, layer, wo_all, g_post, pp_all, pg_all, tm=512):
    n, d = x2.shape
    w = W_GRP
    gp = g_post.reshape(1, d)
    of_layer = lambda a: pl.BlockSpec((None,) + a.shape[1:], lambda i: (layer,) + (0,) * (a.ndim - 1))
    return pl.pallas_call(
        _out_kernel,
        grid=(n // tm,),
        in_specs=[pl.BlockSpec((tm, w), lambda i: (i, 0))] * 4
                 + [pl.BlockSpec((tm, d), lambda i: (i, 0)),
                    pl.BlockSpec((None, tm, p3.shape[2]), lambda i: (layer, i, 0)),
                    of_layer(wo_all), pl.BlockSpec(gp.shape, lambda i: (0, 0)), of_layer(pp_all), of_layer(pg_all)],
        out_specs=pl.BlockSpec((tm, d), lambda i: (i, 0)),
        out_shape=jax.ShapeDtypeStruct((n, d), F32),
        compiler_params=_cparams(("parallel",)),
        name="out_proj",
    )(*ys, x2, p3, wo_all, gp, pp_all, pg_all)


def _prep_w_kernel(w_ref, o_ref):
    wv = w_ref[...]
    rows = wv.shape[0]
    o_ref[...] = jnp.concatenate([jnp.zeros((rows, b_), F32) if a_ is None else wv[:, a_:b_]
                                  for a_, b_ in _SEGMENTS], axis=1).astype(BF16)


def _prep_w_in(w_in, tr=128):
    depth, d, n_in = w_in.shape
    return pl.pallas_call(
        _prep_w_kernel,
        grid=(depth, d // tr),
        in_specs=[pl.BlockSpec((None, tr, n_in), lambda l, r: (l, r, 0))],
        out_specs=pl.BlockSpec((None, tr, N_PROJ), lambda l, r: (l, r, 0)),
        out_shape=jax.ShapeDtypeStruct((depth, d, N_PROJ), BF16),
        compiler_params=_cparams(("parallel", "parallel")),
        name="prep_w_in",
    )(w_in)


def _layer(x2, p3, layer, b, s_len, bt, pt, imat, w_p_all, wo_all, pp_all, pg_all,
           g_pre, g_post, nsa_pos, nsa_w1, nsa_w2,
           sgu_ln_g, sgu_ln_b, sgu_w, sgu_b, conv_w, conv_b, conv_ln_g, conv_ln_b, conv_pw, conv_pw_b,
           rwkv_mu, rwkv_w0, rwkv_w_up, rwkv_a0, rwkv_a_up, rwkv_k_k, rwkv_k_a, rwkv_r_k,
           rwkv_gn_g, rwkv_gn_b):
    n = x2.shape[0]
    w = W_GRP
    t_len = sgu_w.shape[-1]
    wall = jnp.transpose(sgu_w, (1, 0, 2)).reshape(t_len, N_HEADS * t_len)
    sb = jnp.repeat(sgu_b.T, HEAD_DIM, axis=1)
    nsa, rw, yb, yc = _in_proj_gmlp_conv(x2, g_pre, w_p_all, layer, s_len, wall, sb, sgu_ln_g, sgu_ln_b, conv_w, conv_b,
                                         conv_ln_g, conv_ln_b, conv_pw, conv_pw_b, t_len)

    nsa3 = nsa.reshape(b, s_len, NSA_W)
    kc, vct = _nsa_compress(nsa3, nsa_pos, nsa_w1, nsa_w2)
    ya = _nsa_attention(nsa3, kc, vct, bt, pt, imat).reshape(n, w)

    lora = RWKV_LORA
    mu_p = jnp.concatenate([rwkv_mu, jnp.zeros((RW_W - rwkv_mu.shape[0],), F32)]).reshape(1, RW_W)
    wup_p = jnp.concatenate([rwkv_w_up, jnp.zeros((LANES - lora, w), F32)], axis=0)
    aup_p = jnp.concatenate([jnp.zeros((lora, w), F32), rwkv_a_up, jnp.zeros((LANES - 2 * lora, w), F32)], axis=0)
    vec = jnp.stack([rwkv_w0, rwkv_a0, rwkv_k_k, rwkv_k_a, rwkv_r_k.reshape(w), rwkv_gn_g, rwkv_gn_b,
                     jnp.zeros((w,), F32)], axis=0)
    yd = _rwkv(rw.reshape(b, s_len, RW_W), mu_p, wup_p, aup_p, vec).reshape(n, w)

    return _out_proj((ya, yb, yc, yd), x2, p3, layer, wo_all, g_post, pp_all, pg_all)


def kernel(x, p, rel_bias, w_in, w_out, g_pre, g_post, nsa_pos, nsa_w1, nsa_w2, sgu_ln_g, sgu_ln_b, sgu_w, sgu_b, conv_w, conv_b, conv_ln_g, conv_ln_b, conv_pw, conv_pw_b, rwkv_mu, rwkv_w0, rwkv_w_up, rwkv_a0, rwkv_a_up, rwkv_k_k, rwkv_k_a, rwkv_r_k, rwkv_gn_g, rwkv_gn_b, ple_proj, ple_gate):
    b, s_len, d = x.shape
    depth = w_in.shape[0]
    bt, pt, imat = _nsa_tables(rel_bias, s_len)
    x2 = x.reshape(b * s_len, d)
    p3 = p.reshape(depth, b * s_len, p.shape[-1])
    w_p_all = _prep_w_in(w_in)
    wo_all, pp_all, pg_all = w_out.astype(BF16), ple_proj.astype(BF16), ple_gate.astype(BF16)
    per_layer = (g_pre, g_post, nsa_pos, nsa_w1, nsa_w2, sgu_ln_g, sgu_ln_b, sgu_w, sgu_b,
                 conv_w, conv_b, conv_ln_g, conv_ln_b, conv_pw, conv_pw_b, rwkv_mu, rwkv_w0, rwkv_w_up,
                 rwkv_a0, rwkv_a_up, rwkv_k_k, rwkv_k_a, rwkv_r_k, rwkv_gn_g, rwkv_gn_b)
    for i in range(depth):
        x2 = _layer(x2, p3, i, b, s_len, bt, pt, imat, w_p_all, wo_all, pp_all, pg_all,
                    *[a[i] for a in per_layer])
    return x2.reshape(b, s_len, d)
```

```python
import functools
import math

import numpy as np
import jax
import jax.numpy as jnp
from jax import lax
from jax.experimental import pallas as pl
from jax.experimental.pallas import tpu as pltpu

F32 = jnp.float32
BF16 = jnp.bfloat16

W_GRP = 256
HEAD_DIM = 64
N_HEADS = 4
NSA_DK = 64
CMP_STRIDE = 16
CMP_BLOCK = 32
CMP_HIDDEN = 128
SEL_BLOCK = 64
N_SEL = 16
WINDOW = 512
REL_BUCKETS = 32
REL_MAX_EXACT = 16
REL_MAX_DIST = 128
CONV_WIDTH = 31
RWKV_LORA = 32
RWKV_GN_EPS = 64e-5
NORM_EPS = 1e-6
LN_EPS = 1e-5
NEG = -1e30
FORCE = 1e4
LOG2E = math.log2(math.e)

LANES = 128
SUBLANES = 8
Q_TILE = 128
SEL_GROUP = 4
NSA_STREAMS = 4
RW_CHUNK = 64
RW_SUB = 4
VMEM_LIMIT = 48 * 1024 * 1024
NSA_VMEM_LIMIT = 56 * 1024 * 1024

NSA_W, GM_W, CV_W, RW_W = 1024, 768, 768, 1152
N_PROJ = NSA_W + GM_W + CV_W + RW_W


def _proj_segments():
    names = ["q", "kc", "vc", "ks", "vs", "kw", "vw", "g", "za", "u", "v", "zb", "ga", "gb", "zc", "rw", "zd"]
    widths = [256, 64, 64, 64, 64, 64, 64, 12, 256, 256, 256, 256, 256, 256, 256, 832, 256]
    o, off = {}, 0
    for n, w in zip(names, widths):
        o[n] = (off, off + w)
        off += w
    return [o["q"], o["za"], (o["ks"][0], o["vw"][1]), (o["kc"][0], o["vc"][1]), o["g"], (None, 116),
            (o["u"][0], o["zc"][1]), o["rw"], (None, 64), o["zd"]]


_SEGMENTS = _proj_segments()


def _rel_buckets(n):
    d = np.arange(n)
    nf = np.maximum(d, REL_MAX_EXACT).astype(np.float32)
    large = REL_MAX_EXACT + (np.log(nf / np.float32(REL_MAX_EXACT)) / np.float32(math.log(REL_MAX_DIST / REL_MAX_EXACT))
                             * np.float32(REL_BUCKETS - REL_MAX_EXACT)).astype(np.int32)
    large = np.minimum(large, REL_BUCKETS - 1)
    return np.where(d < REL_MAX_EXACT, d, large)


def _bdot(a, b):
    return jnp.dot(a.astype(BF16), b.astype(BF16), preferred_element_type=F32)


def _nt(a, b):
    return lax.dot_general(a, b, (((1,), (1,)), ((), ())), preferred_element_type=F32)


def _split2(a):
    hi = a.astype(BF16)
    lo = (a - hi.astype(F32)).astype(BF16)
    return hi, lo


def _split3(a):
    hi = a.astype(BF16)
    r = a - hi.astype(F32)
    mid = r.astype(BF16)
    lo = (r - mid.astype(F32)).astype(BF16)
    return hi, mid, lo


def _dot3(a, b):
    ah, al = _split2(a)
    bh, bl = _split2(b)
    d = lambda x, y: jnp.dot(x, y, preferred_element_type=F32)
    return d(ah, bh) + (d(ah, bl) + d(al, bh))


def _dot_sel_lhs(a01, b):
    a = a01.astype(BF16)
    bh, bm, bl = _split3(b)
    d = lambda y: jnp.dot(a, y, preferred_element_type=F32)
    return d(bh) + (d(bm) + d(bl))


def _sigmoid(x):
    return 1.0 / (1.0 + jnp.exp(-x))


def _silu(x):
    return x * _sigmoid(x)


def _gelu_tanh(x):
    c = math.sqrt(2.0 / math.pi)
    return 0.5 * x * (1.0 + jnp.tanh(c * (x + 0.044715 * (x * x * x))))


def _layer_norm(x, g, b):
    mu = jnp.mean(x, axis=-1, keepdims=True)
    xc = x - mu
    var = jnp.mean(xc * xc, axis=-1, keepdims=True)
    return xc * lax.rsqrt(var + LN_EPS) * g + b


def _block_diag(x, n):
    r, c = x.shape
    t = jnp.concatenate([x] * n, axis=0)
    ri = lax.broadcasted_iota(jnp.int32, t.shape, 0) // r
    ci = lax.broadcasted_iota(jnp.int32, t.shape, 1) // (c // n)
    return jnp.where(ri == ci, t, 0.0)


def _cparams(sem, vmem=VMEM_LIMIT):
    return pltpu.CompilerParams(dimension_semantics=sem, vmem_limit_bytes=vmem)


def _cmp_kernel(kv_ref, pos_ref, w1_ref, w2_ref, o_ref, ot_ref):
    nc = o_ref.shape[0]
    dk = o_ref.shape[1]
    first = None
    second = None
    for r in range(CMP_STRIDE):
        x = kv_ref[pl.ds(r, nc, stride=CMP_STRIDE), :]
        t = _bdot(x + pos_ref[r], w1_ref[r])
        first = t if first is None else first + t
        t = _bdot(x + pos_ref[CMP_STRIDE + r], w1_ref[CMP_STRIDE + r])
        second = t if second is None else second + t
    hid = first + pltpu.roll(second, nc - 1, axis=0)
    out = _bdot(_silu(hid), w2_ref[...])
    o_ref[...] = out[:, :dk]
    ot_ref[...] = out.T[dk:, :]


def _nsa_compress(nsa3, pos, w1, w2):
    b, s_len, _ = nsa3.shape
    nc = s_len // CMP_STRIDE
    dk = NSA_DK
    zw = jnp.zeros((CMP_BLOCK, dk, CMP_HIDDEN), F32)
    w1r = w1.reshape(2, CMP_BLOCK, dk, CMP_HIDDEN)
    w1bd = jnp.concatenate([jnp.concatenate([w1r[0], zw], axis=2),
                            jnp.concatenate([zw, w1r[1]], axis=2)], axis=1).astype(BF16)
    z2 = jnp.zeros((CMP_HIDDEN, dk), F32)
    w2bd = jnp.concatenate([jnp.concatenate([w2[0], z2], axis=1),
                            jnp.concatenate([z2, w2[1]], axis=1)], axis=0).astype(BF16)
    posr = jnp.concatenate([pos[0], pos[1]], axis=1).reshape(CMP_BLOCK, 1, 2 * dk)
    full = lambda a: pl.BlockSpec(a.shape, lambda i: (0,) * a.ndim)
    return pl.pallas_call(
        _cmp_kernel,
        grid=(b,),
        in_specs=[pl.BlockSpec((None, s_len, LANES), lambda i: (i, 0, 6)),
                  full(posr), full(w1bd), full(w2bd)],
        out_specs=[pl.BlockSpec((None, nc, dk), lambda i: (i, 0, 0)),
                   pl.BlockSpec((None, dk, nc), lambda i: (i, 0, 0))],
        out_shape=[jax.ShapeDtypeStruct((b, nc, dk), F32),
                   jax.ShapeDtypeStruct((b, dk, nc), F32)],
        compiler_params=_cparams(("parallel",)),
        name="nsa_compress",
    )(nsa3, posr, w1bd, w2bd)


def _nsa_kernel(q_ref, za_ref, g_ref, ks_ref, kw_ref, kc_ref, vct_ref, pt_ref, bt_ref, imat_ref,
                      o_ref, ksb, vst, kwb, vwt, madd, val_s, cnt_s, pc_s, yc_s, m_s, acc_s, *, n_top, n_win):
    c = pl.program_id(1)
    n_qt = pl.num_programs(1)
    n_st = q_ref.shape[0]
    n_kt = ksb.shape[1]
    hd = HEAD_DIM
    nh = N_HEADS
    n_blk = imat_ref.shape[0]
    per_q = Q_TILE // SEL_BLOCK
    v_rows = vst.shape[2]
    streams = range(n_st)

    @pl.when(c == 0)
    def _prep():
        key_blk = lax.broadcasted_iota(jnp.int32, (Q_TILE, n_blk), 0) // SEL_BLOCK
        col = lax.broadcasted_iota(jnp.int32, (Q_TILE, n_blk), 1)
        ones_row = jnp.where(lax.broadcasted_iota(jnp.int32, (v_rows - hd, Q_TILE), 0) == 0, 1.0, 0.0)

        def body(i, carry):
            r0 = pl.multiple_of(i * Q_TILE, Q_TILE)
            onehot = jnp.where(col == key_blk + per_q * i, 1.0, 0.0)
            for s in streams:
                t = ks_ref[s, pl.ds(r0, Q_TILE), :]
                ksb[s, i] = jnp.concatenate([t[:, :hd], onehot], axis=1).astype(BF16)
                vst[s, i] = jnp.concatenate([t.T[hd:, :], ones_row], axis=0).astype(BF16)
                t = kw_ref[s, pl.ds(r0, Q_TILE), :]
                kwb[s, i] = t[:, :hd].astype(BF16)
                vwt[s, i] = jnp.concatenate([t.T[hd:, :], ones_row], axis=0).astype(BF16)
            return carry
        lax.fori_loop(0, n_kt, body, 0)

    def reset_state():
        m_s[...] = jnp.full(m_s.shape, NEG, F32)
        acc_s[...] = jnp.zeros(acc_s.shape, F32)

    def update(groups):
        m_old = [m_s[s] for s in streams]
        m_new = []
        for s in streams:
            m = m_old[s]
            for s_i in groups[s][0]:
                m = jnp.maximum(m, jnp.max(s_i, axis=0, keepdims=True))
            m_new.append(m)
        for s in streams:
            ss, vs = groups[s]
            pv = None
            for j in range(0, len(ss), 2):
                p_j = jnp.concatenate([jnp.exp2(s_i - m_new[s]).astype(BF16) for s_i in ss[j:j + 2]], axis=0)
                v_j = jnp.concatenate(vs[j:j + 2], axis=1)
                t = jnp.dot(v_j, p_j, preferred_element_type=F32)
                pv = t if pv is None else pv + t
            acc_s[s] = jnp.exp2(m_old[s] - m_new[s]) * acc_s[s] + pv
            m_s[s] = m_new[s]

    def branch_out(s, g):
        acc = acc_s[s]
        return (g / acc[hd:hd + 1, :]) * acc[:hd, :]

    q_all, qb, gate = [], [], []
    for s in streams:
        q = q_ref[s] * (hd ** -0.5 * LOG2E)
        qa = jnp.concatenate([q[:, hd * h:hd * (h + 1)] for h in range(nh)], axis=0)
        q_all.append(qa)
        qb.append(qa.astype(BF16))
        g_t = _sigmoid(g_ref[s]).T
        gate.append([jnp.concatenate([g_t[3 * h + br:3 * h + br + 1, :] for h in range(nh)], axis=1)
                     for br in range(3)])

    reset_state()
    groups = []
    for s in streams:
        ss, vts = [], []
        for i in range(n_win + 1):
            kt = jnp.maximum(c - i, 0)
            s_i = _nt(kwb[s, kt], qb[s])
            if i == 0:
                s_i = s_i + bt_ref[0]
            else:
                if i == 1:
                    s_i = s_i + bt_ref[1]
                elif i == n_win:
                    s_i = s_i + bt_ref[2]
                s_i = s_i + jnp.where(c >= i, 0.0, NEG)
            ss.append(s_i)
            vts.append(vwt[s, kt])
        groups.append((ss, vts))
    update(groups)
    y_acc = [branch_out(s, gate[s][2]) for s in streams]

    n_c = kc_ref.shape[1]
    per_tile = Q_TILE // CMP_STRIDE
    off = pl.multiple_of((n_qt - 1 - c) * per_tile, SUBLANES)

    def cmp_branch(rows):
        bias = pt_ref[pl.ds(off, rows), :]
        valid = bias > 0.5 * NEG
        scs = [_nt(kc_ref[s, 0:rows, :].astype(BF16), qb[s]) + bias for s in streams]
        ps = [jnp.where(valid, jnp.exp2(sc - jnp.max(sc, axis=0, keepdims=True)), 0.0) for sc in scs]
        ls = [jnp.sum(p, axis=0, keepdims=True) for p in ps]
        pns = [p * jnp.where(l > 0.0, 1.0 / l, 0.0) for p, l in zip(ps, ls)]
        for s in streams:
            yc_s[s] = gate[s][0] * _bdot(vct_ref[s, :, 0:rows], pns[s])
            pc = pns[s][:, 0:Q_TILE]
            for h in range(1, nh):
                pc = pc + pns[s][:, h * Q_TILE:(h + 1) * Q_TILE]
            pc_s[s, 0:rows, :] = pc
            if rows < n_c:
                pc_s[s, rows:n_c, :] = jnp.zeros((n_c - rows, Q_TILE), F32)

    half = n_c // 2
    few = (c + 1) * per_tile <= half
    pl.when(few)(lambda: cmp_branch(half))
    pl.when(jnp.logical_not(few))(lambda: cmp_branch(n_c))

    tl = lax.broadcasted_iota(jnp.int32, (n_blk, Q_TILE), 1)
    jb = lax.broadcasted_iota(jnp.int32, (n_blk, Q_TILE), 0)
    cur = c * per_q + tl // SEL_BLOCK
    causal = jb <= cur
    need_rank = (c + 1) * per_q > n_top

    @pl.when(jnp.logical_not(need_rank))
    def _all_causal():
        for s in streams:
            madd[s] = jnp.where(causal, 0.0, NEG)

    @pl.when(need_rank)
    def _rank():
        forced = (jb == 0) | (jb == cur) | (jb == cur - 1)
        n_r = n_blk // SUBLANES
        jrow = lax.broadcasted_iota(jnp.int32, (SUBLANES, Q_TILE), 0)
        for s in streams:
            imp = _dot_sel_lhs(imat_ref[...], pc_s[s])
            val_s[s] = jnp.where(causal, jnp.where(forced, FORCE, imp), NEG)
            cnt_s[s] = jnp.zeros((n_blk, Q_TILE), F32)
        for ci in range(n_r):
            @pl.when(SUBLANES * ci < (c + 1) * per_q)
            def _chunk():
                for s in streams:
                    val = val_s[s]
                    blocks = [val[SUBLANES * r:SUBLANES * (r + 1), :] for r in range(n_r)]
                    cnts = [None] * n_r
                    for i in range(SUBLANES * ci, SUBLANES * (ci + 1)):
                        vi = jnp.broadcast_to(val[i:i + 1, :], (SUBLANES, Q_TILE))
                        for r in range(n_r):
                            if r > ci:
                                t = jnp.where(vi >= blocks[r], 1.0, 0.0)
                            elif r < ci:
                                t = jnp.where(vi > blocks[r], 1.0, 0.0)
                            else:
                                ge = jnp.where(vi >= blocks[r], 1.0, 0.0)
                                gt = jnp.where(vi > blocks[r], 1.0, 0.0)
                                t = jnp.where(jrow + SUBLANES * r > i, ge, gt)
                            cnts[r] = t if cnts[r] is None else cnts[r] + t
                    cnt_s[s] = cnt_s[s] + jnp.concatenate(cnts, axis=0)
        for s in streams:
            madd[s] = jnp.where(causal & (cnt_s[s] < float(n_top)), 0.0, NEG)

    reset_state()
    pad = jnp.zeros((LANES - n_blk, Q_TILE), F32)
    kt1 = jnp.maximum(c - 1, 0)
    q_far, groups = [], []
    for s in streams:
        m_t = jnp.concatenate([madd[s], pad], axis=0).T[:, :n_blk]
        q_mask = jnp.concatenate([q_all[s], jnp.concatenate([m_t] * nh, axis=0)], axis=1)
        lane = lax.broadcasted_iota(jnp.int32, q_mask.shape, 1)
        q_near = q_mask.astype(BF16)
        q_far.append(jnp.where(lane >= hd + per_q * jnp.maximum(c - 1, 0), NEG, q_mask).astype(BF16))
        s0 = _nt(ksb[s, c], q_near) + bt_ref[0]
        s1 = _nt(ksb[s, kt1], q_near) + bt_ref[1] + jnp.where(c >= 1, 0.0, NEG)
        groups.append(([s0, s1], [vst[s, c], vst[s, kt1]]))

    update(groups)

    n_far = jnp.maximum(c - 1, 0)

    def far_body(g, carry):
        kts = [g * SEL_GROUP + j for j in range(SEL_GROUP)]
        update([([_nt(ksb[s, kt], q_far[s]) for kt in kts], [vst[s, kt] for kt in kts]) for s in streams])
        return carry
    lax.fori_loop(0, (n_far + SEL_GROUP - 1) // SEL_GROUP, far_body, 0)

    for s in streams:
        y = y_acc[s] + yc_s[s] + branch_out(s, gate[s][1])
        y_t = jnp.concatenate([y[:, h * Q_TILE:(h + 1) * Q_TILE] for h in range(nh)], axis=0)
        o_ref[s] = y_t.T * _silu(za_ref[s])


def _bucket_lookup(relc, idx, visible):
    r, cc = idx.shape
    onehot = (jnp.asarray(idx.reshape(1, -1)) == jnp.arange(relc.shape[0])[:, None]).astype(F32)
    vals = jnp.einsum("bh,bn->hn", relc, onehot, precision=lax.Precision.HIGHEST).reshape(-1, r, cc)
    vals = jnp.where(jnp.asarray(visible)[None], vals, NEG)
    return jnp.transpose(vals, (1, 0, 2)).reshape(r, -1).astype(F32)


def _nsa_tables(rel_bias, s_len):
    n_qt = s_len // Q_TILE
    n_c = s_len // CMP_STRIDE
    relc = rel_bias - rel_bias[REL_BUCKETS - 1][None, :]
    bk = _rel_buckets(s_len + Q_TILE)
    kl = np.arange(Q_TILE)[:, None]
    tq = np.arange(Q_TILE)[None, :]
    d0 = tq - kl
    diag = _bucket_lookup(relc, bk[np.clip(d0, 0, None)], d0 >= 0)
    prev = _bucket_lookup(relc, bk[Q_TILE + d0], np.ones_like(d0, bool))
    edge = jnp.asarray(np.tile(np.where(kl > tq, 0.0, NEG).astype(np.float32), (1, N_HEADS)))
    bt = jnp.stack([diag, prev, edge], axis=0) * LOG2E
    per_q = Q_TILE // CMP_STRIDE
    n_rows = n_c + per_q * (n_qt - 1)
    r = np.arange(n_rows)[:, None]
    dc = tq - CMP_STRIDE * (r - per_q * (n_qt - 1)) - (CMP_BLOCK - 1)
    pt = _bucket_lookup(relc, bk[np.clip(dc, 0, None)], dc >= 0) * LOG2E
    n_blk = s_len // SEL_BLOCK
    ratio = SEL_BLOCK // CMP_STRIDE
    jj = np.arange(n_blk)[:, None]
    ii = np.arange(n_c)[None, :]
    imat = ((ii >= ratio * jj - 1) & (ii <= ratio * jj + ratio - 1) & (ii < n_c - 1)).astype(np.float32)
    return bt, pt, jnp.asarray(imat)


def _nsa_attention(nsa3, kc, vct, bt, pt, imat):
    b, s_len, _ = nsa3.shape
    n_qt = s_len // Q_TILE
    n_c = s_len // CMP_STRIDE
    n_blk = s_len // SEL_BLOCK
    n_win = WINDOW // Q_TILE
    hd = HEAD_DIM
    hq = N_HEADS * Q_TILE
    v_rows = hd + 2 * SUBLANES
    n_st = NSA_STREAMS if b % NSA_STREAMS == 0 else 1
    kern = functools.partial(_nsa_kernel, n_top=min(N_SEL, n_blk), n_win=n_win)
    return pl.pallas_call(
        kern,
        grid=(b // n_st, n_qt),
        in_specs=[pl.BlockSpec((n_st, Q_TILE, 256), lambda i, c: (i, c, 0)),
                  pl.BlockSpec((n_st, Q_TILE, 256), lambda i, c: (i, c, 1)),
                  pl.BlockSpec((n_st, Q_TILE, LANES), lambda i, c: (i, c, 7)),
                  pl.BlockSpec((n_st, s_len, LANES), lambda i, c: (i, 0, 4),
                               pipeline_mode=pl.Buffered(1)),
                  pl.BlockSpec((n_st, s_len, LANES), lambda i, c: (i, 0, 5),
                               pipeline_mode=pl.Buffered(1)),
                  pl.BlockSpec((n_st, n_c, hd), lambda i, c: (i, 0, 0)),
                  pl.BlockSpec((n_st, hd, n_c), lambda i, c: (i, 0, 0)),
                  pl.BlockSpec(pt.shape, lambda i, c: (0, 0)),
                  pl.BlockSpec(bt.shape, lambda i, c: (0, 0, 0)),
                  pl.BlockSpec(imat.shape, lambda i, c: (0, 0))],
        out_specs=pl.BlockSpec((n_st, Q_TILE, 256), lambda i, c: (i, c, 0)),
        out_shape=jax.ShapeDtypeStruct((b, s_len, 256), F32),
        scratch_shapes=[pltpu.VMEM((n_st, n_qt, Q_TILE, hd + n_blk), BF16),
                        pltpu.VMEM((n_st, n_qt, v_rows, Q_TILE), BF16),
                        pltpu.VMEM((n_st, n_qt, Q_TILE, hd), BF16),
                        pltpu.VMEM((n_st, n_qt, v_rows, Q_TILE), BF16),
                        pltpu.VMEM((n_st, n_blk, Q_TILE), F32),
                        pltpu.VMEM((n_st, n_blk, Q_TILE), F32),
                        pltpu.VMEM((n_st, n_blk, Q_TILE), F32),
                        pltpu.VMEM((n_st, n_c, Q_TILE), F32),
                        pltpu.VMEM((n_st, hd, hq), F32),
                        pltpu.VMEM((n_st, 1, hq), F32),
                        pltpu.VMEM((n_st, v_rows, hq), F32)],
        compiler_params=_cparams(("arbitrary", "arbitrary"), NSA_VMEM_LIMIT),
        name="nsa_attention",
    )(nsa3, nsa3, nsa3, nsa3, nsa3, kc, vct, pt, bt, imat)


def _proj_gc_kernel(x_ref, g_ref, w_ref, wall_ref, sb_ref, lng_ref, lnb_ref, cw_ref, cb_ref, clg_ref, clb_ref,
                    cpw_ref, cpb_ref, o_nsa, o_rw, ob_ref, oc_ref, xs, ph_s, *, tiles_per_seq, t_len):
    first = (pl.program_id(0) % tiles_per_seq) == 0
    tm = x_ref.shape[0]
    halo = xs.shape[0] - tm - SUBLANES
    w = W_GRP
    x = x_ref[...]
    ms = jnp.mean(x * x, axis=-1, keepdims=True)
    h = (x * lax.rsqrt(ms + NORM_EPS) * g_ref[...]).astype(BF16)
    c_gm, c_cv, c_rw = NSA_W, NSA_W + GM_W, NSA_W + GM_W + CV_W
    gm = jnp.dot(h, w_ref[:, c_gm:c_cv], preferred_element_type=F32)
    cv = jnp.dot(h, w_ref[:, c_cv:c_rw], preferred_element_type=F32)

    u = _gelu_tanh(gm[:, :w])
    v = _layer_norm(_gelu_tanh(gm[:, w:2 * w]), lng_ref[...], lnb_ref[...])
    zb = _silu(gm[:, 2 * w:])
    wall = wall_ref[...]
    ti = lax.broadcasted_iota(jnp.int32, wall.shape, 0)
    si = lax.broadcasted_iota(jnp.int32, wall.shape, 1) % t_len
    wall = jnp.where(si <= ti, wall, 0.0).astype(BF16)
    for ch in range(tm // t_len):
        r0, r1 = ch * t_len, (ch + 1) * t_len
        sv = jnp.dot(wall, _block_diag(v[r0:r1], N_HEADS).astype(BF16), preferred_element_type=F32) + sb_ref[...]
        ob_ref[r0:r1, :] = u[r0:r1] * sv * zb[r0:r1]

    o_nsa[...] = jnp.dot(h, w_ref[:, :c_gm], preferred_element_type=F32)

    o_rw[...] = jnp.dot(h, w_ref[:, c_rw:], preferred_element_type=F32)
    xs[0:halo, :] = jnp.where(first, 0.0, xs[0:halo, :])
    xs[halo:halo + tm, :] = cv[:, :w] * _sigmoid(cv[:, w:2 * w])
    xs[halo + tm:, :] = jnp.zeros((xs.shape[0] - halo - tm, w), F32)
    cw = cw_ref[...]
    for ph in range(1, SUBLANES):
        ph_s[ph - 1] = xs[ph:ph + halo + tm, :]
    base = halo - (CONV_WIDTH - 1)
    acc = jnp.zeros((tm, w), F32)
    for j in range(CONV_WIDTH):
        a, ph = divmod(base + j, SUBLANES)
        r0 = SUBLANES * a
        win = xs[r0:r0 + tm, :] if ph == 0 else ph_s[ph - 1, r0:r0 + tm, :]
        acc = acc + cw[j:j + 1, :] * win
    xs[0:halo, :] = xs[tm:tm + halo, :]
    y = _layer_norm(acc + cb_ref[...], clg_ref[...], clb_ref[...])
    y = _bdot(_silu(y), cpw_ref[...]) + cpb_ref[...]
    oc_ref[...] = y * _silu(cv[:, 2 * w:])


def _in_proj_gmlp_conv(x2, g_pre, w_p_all, layer, s_len, wall, sb, lng, lnb, cw, cb, clg, clb, cpw, cpb, t_len, tm=512, halo=32):
    n, d = x2.shape
    w = W_GRP
    row = lambda a: a.reshape(1, w)
    full = lambda a: pl.BlockSpec(a.shape, lambda i: (0,) * a.ndim)
    cwp = jnp.concatenate([cw, jnp.zeros((32 - CONV_WIDTH, w), F32)], axis=0)
    args = (wall, sb, row(lng), row(lnb), cwp, row(cb), row(clg), row(clb), cpw.astype(BF16), row(cpb))
    widths = (NSA_W, RW_W, w, w)
    return pl.pallas_call(
        functools.partial(_proj_gc_kernel, tiles_per_seq=s_len // tm, t_len=t_len),
        grid=(n // tm,),
        in_specs=[pl.BlockSpec((tm, d), lambda i: (i, 0)),
                  pl.BlockSpec((1, d), lambda i: (0, 0)),
                  pl.BlockSpec((None, d, N_PROJ), lambda i: (layer, 0, 0))] + [full(a) for a in args],
        out_specs=[pl.BlockSpec((tm, wd), lambda i: (i, 0)) for wd in widths],
        out_shape=[jax.ShapeDtypeStruct((n, wd), F32) for wd in widths],
        scratch_shapes=[pltpu.VMEM((halo + tm + SUBLANES, w), F32),
                        pltpu.VMEM((SUBLANES - 1, halo + tm, w), F32)],
        compiler_params=_cparams(("arbitrary",)),
        name="in_proj_gmlp_conv",
    )(x2, g_pre.reshape(1, d), w_p_all, *args)


def _rwkv_kernel(rw_ref, prev_ref, mu_ref, wup_ref, aup_ref, vec_ref, o_ref, st):
    c = pl.program_id(0)
    n_b = rw_ref.shape[0]
    L = RW_CHUNK
    n_sub = rw_ref.shape[1] // L
    w = W_GRP
    nh = N_HEADS
    hd = HEAD_DIM

    @pl.when(c == 0)
    def _init():
        st[...] = jnp.zeros(st.shape, F32)

    vec = vec_ref[...]
    w0, a0, k_k, k_a, r_k, gn_g, gn_b = [vec[i:i + 1, :] for i in range(7)]
    mu = mu_ref[...]
    rows = lax.broadcasted_iota(jnp.int32, (L, w), 0)
    lane = lax.broadcasted_iota(jnp.int32, (L, w), 1)
    s_of = lane % hd
    ones_bd = jnp.where((lax.broadcasted_iota(jnp.int32, (w, w), 0) // hd)
                        == (lax.broadcasted_iota(jnp.int32, (w, w), 1) // hd), 1.0, 0.0)
    tri = jnp.where(lax.broadcasted_iota(jnp.int32, (L, L), 1) <= lax.broadcasted_iota(jnp.int32, (L, L), 0), 1.0, 0.0)
    bd_mask = (lax.broadcasted_iota(jnp.int32, (w, w), 0) // hd) == (lax.broadcasted_iota(jnp.int32, (w, w), 1) // hd)
    strict = s_of < rows
    incl = s_of <= rows
    eye_all = jnp.where(s_of == rows, 1.0, 0.0)

    def bd(x):
        xb = x.astype(BF16)
        return jnp.where(bd_mask, jnp.concatenate([xb] * nh, axis=0), jnp.zeros((), BF16))

    def mm(a, b):
        return jnp.dot(a.astype(BF16), b.astype(BF16), preferred_element_type=F32)

    nb = range(n_b)
    probs = range(n_b * n_sub)
    stack = lambda parts: jnp.concatenate(parts, axis=0)
    part = lambda x, p: x[p * L:(p + 1) * L]
    zs = [rw_ref[b] for b in nb]
    zrow = lax.broadcasted_iota(jnp.int32, zs[0].shape, 0)
    n_prev = prev_ref.shape[1]
    xs = []
    for b in nb:
        last = jnp.where(c == 0, 0.0, prev_ref[b, n_prev - 1:n_prev, :])
        zprev = jnp.where(zrow == 0, last, pltpu.roll(zs[b], 1, axis=0))
        xs.append(zs[b] + mu * (zprev - zs[b]))
    xs = stack(xs)
    r = xs[:, 0:w]
    k = xs[:, w:2 * w]
    v = xs[:, 2 * w:3 * w]
    wa = xs[:, 3 * w:3 * w + LANES]
    zd = stack([z[:, 3 * w + LANES:] for z in zs])

    zz = w0 + _dot3(jnp.tanh(wa), wup_ref[...])
    lw = (-math.exp(-0.5)) * _sigmoid(zz)
    a = _sigmoid(a0 + _dot3(wa, aup_ref[...]))
    kkr = k * k_k
    kk = kkr * lax.rsqrt(jnp.maximum(mm(kkr * kkr, ones_bd), 1e-24))
    k2 = k * (1.0 + (a - 1.0) * k_a)
    bb = kk * a

    lw_wide = jnp.concatenate([part(lw, p) for p in probs], axis=1)
    cs_wide = _dot_sel_lhs(tri, lw_wide)
    cs = stack([cs_wide[:, p * w:(p + 1) * w] for p in probs])
    g_t = jnp.exp(cs)
    g_inv = jnp.exp(-cs)
    kq = (kk * jnp.exp(cs - lw)).astype(BF16)
    rq = (r * g_t).astype(BF16)
    kt = k2 * g_inv
    bt = bb * g_inv

    lhs = [stack([part(kq, p), part(rq, p)]) for p in probs]
    ab_b = [_nt(lhs[p], bd(part(bt, p))) for p in probs]
    ab_k = [_nt(lhs[p], bd(part(kt, p))) for p in probs]
    a_b = [jnp.where(strict, x[:L], 0.0) for x in ab_b]
    b_b = [jnp.where(incl, x[L:], 0.0) for x in ab_b]
    ak_bk = [stack([jnp.where(strict, x[:L], 0.0), jnp.where(incl, x[L:], 0.0)]) for x in ab_k]
    akv = [mm(ak_bk[p], bd(part(v, p))) for p in probs]

    npow = a_b
    tinv = [eye_all - x for x in a_b]
    for i in range(int(math.log2(L)) - 1):
        npow = [mm(x, bd(x)) for x in npow]
        tinv = [t + mm(t, bd(x)) for t, x in zip(tinv, npow)]

    zpad = jnp.zeros((w - 2 * L, w), F32)
    y = [None] * len(probs)
    for sub in range(n_sub):
        ps = [b * n_sub + sub for b in nb]
        s0 = [st[b] for b in nb]
        kh = [_nt(lhs[p], s0[b].astype(BF16)) for b, p in zip(nb, ps)]
        u = [mm(tinv[p], bd(kh[b][:L] + akv[p][:L])) for b, p in zip(nb, ps)]
        for b, p in zip(nb, ps):
            y[p] = kh[b][L:] + akv[p][L:] - mm(b_b[p], bd(u[b]))
        for b, p in zip(nb, ps):
            vu_t = stack([part(v, p), u[b], zpad]).T
            kb = stack([part(kt, p), -part(bt, p), zpad])
            d = mm(vu_t, kb)
            st[b] = g_t[(p + 1) * L - 1:(p + 1) * L, :] * (s0[b] + jnp.where(bd_mask, d, 0.0))

    y = stack(y)
    y_hi, y_lo = _split2(y)
    mean = (mm(y_hi, ones_bd) + mm(y_lo, ones_bd)) * (1.0 / hd)
    yc = y - mean
    var = mm(yc * yc, ones_bd) * (1.0 / hd)
    yn = yc * lax.rsqrt(var + RWKV_GN_EPS) * gn_g + gn_b
    bonus = mm(r * k2 * r_k, ones_bd) * v
    out = (yn + bonus) * _silu(zd)
    for b in nb:
        o_ref[b] = out[b * n_sub * L:(b + 1) * n_sub * L]


def _rwkv(rw3, mu_p, wup_p, aup_p, vec):
    b, s_len, _ = rw3.shape
    rows = RW_CHUNK * RW_SUB
    w = W_GRP
    full = lambda a: pl.BlockSpec(a.shape, lambda c: (0,) * a.ndim)
    return pl.pallas_call(
        _rwkv_kernel,
        grid=(s_len // rows,),
        in_specs=[pl.BlockSpec((b, rows, RW_W), lambda c: (0, c, 0)),
                  pl.BlockSpec((b, SUBLANES, RW_W), lambda c: (0, jnp.maximum(c * (rows // SUBLANES) - 1, 0), 0)),
                  full(mu_p), full(wup_p), full(aup_p), full(vec)],
        out_specs=pl.BlockSpec((b, rows, w), lambda c: (0, c, 0)),
        out_shape=jax.ShapeDtypeStruct((b, s_len, w), F32),
        scratch_shapes=[pltpu.VMEM((b, w, w), F32)],
        compiler_params=_cparams(("arbitrary",)),
        name="rwkv7",
    )(rw3, rw3, mu_p, wup_p, aup_p, vec)


def _out_kernel(ya_ref, yb_ref, yc_ref, yd_ref, x_ref, p_ref, wo_ref, gp_ref, pp_ref, pg_ref, o_ref):
    w = W_GRP
    acc = None
    for i, y in enumerate((ya_ref, yb_ref, yc_ref, yd_ref)):
        t = jnp.dot(y[...].astype(BF16), wo_ref[i * w:(i + 1) * w, :], preferred_element_type=F32)
        acc = t if acc is None else acc + t
    ms = jnp.mean(acc * acc, axis=-1, keepdims=True)
    x1 = x_ref[...] + acc * lax.rsqrt(ms + NORM_EPS) * gp_ref[...]
    gate = _sigmoid(jnp.dot(x1.astype(BF16), pg_ref[...], preferred_element_type=F32))
    pe = jnp.dot(p_ref[...].astype(BF16), pp_ref[...], preferred_element_type=F32)
    o_ref[...] = x1 + gate * pe


def _out_proj(ys, x2, p3, layer, wo_all, g_post, pp_all, pg_all, tm=512):
    n, d = x2.shape
    w = W_GRP
    gp = g_post.reshape(1, d)
    of_layer = lambda a: pl.BlockSpec((None,) + a.shape[1:], lambda i: (layer,) + (0,) * (a.ndim - 1))
    return pl.pallas_call(
        _out_kernel,
        grid=(n // tm,),
        in_specs=[pl.BlockSpec((tm, w), lambda i: (i, 0))] * 4
                 + [pl.BlockSpec((tm, d), lambda i: (i, 0)),
                    pl.BlockSpec((None, tm, p3.shape[2]), lambda i: (layer, i, 0)),
                    of_layer(wo_all), pl.BlockSpec(gp.shape, lambda i: (0, 0)), of_layer(pp_all), of_layer(pg_all)],
        out_specs=pl.BlockSpec((tm, d), lambda i: (i, 0)),
        out_shape=jax.ShapeDtypeStruct((n, d), F32),
        compiler_params=_cparams(("parallel",)),
        name="out_proj",
    )(*ys, x2, p3, wo_all, gp, pp_all, pg_all)


def _prep_w_kernel(w_ref, o_ref):
    wv = w_ref[...]
    rows = wv.shape[0]
    o_ref[...] = jnp.concatenate([jnp.zeros((rows, b_), F32) if a_ is None else wv[:, a_:b_]
                                  for a_, b_ in _SEGMENTS], axis=1).astype(BF16)


def _prep_w_in(w_in, tr=128):
    depth, d, n_in = w_in.shape
    return pl.pallas_call(
        _prep_w_kernel,
        grid=(depth, d // tr),
        in_specs=[pl.BlockSpec((None, tr, n_in), lambda l, r: (l, r, 0))],
        out_specs=pl.BlockSpec((None, tr, N_PROJ), lambda l, r: (l, r, 0)),
        out_shape=jax.ShapeDtypeStruct((depth, d, N_PROJ), BF16),
        compiler_params=_cparams(("parallel", "parallel")),
        name="prep_w_in",
    )(w_in)


def _layer(x2, p3, layer, b, s_len, bt, pt, imat, w_p_all, wo_all, pp_all, pg_all,
           g_pre, g_post, nsa_pos, nsa_w1, nsa_w2,
           sgu_ln_g, sgu_ln_b, sgu_w, sgu_b, conv_w, conv_b, conv_ln_g, conv_ln_b, conv_pw, conv_pw_b,
           rwkv_mu, rwkv_w0, rwkv_w_up, rwkv_a0, rwkv_a_up, rwkv_k_k, rwkv_k_a, rwkv_r_k,
           rwkv_gn_g, rwkv_gn_b):
    n = x2.shape[0]
    w = W_GRP
    t_len = sgu_w.shape[-1]
    wall = jnp.transpose(sgu_w, (1, 0, 2)).reshape(t_len, N_HEADS * t_len)
    sb = jnp.repeat(sgu_b.T, HEAD_DIM, axis=1)
    nsa, rw, yb, yc = _in_proj_gmlp_conv(x2, g_pre, w_p_all, layer, s_len, wall, sb, sgu_ln_g, sgu_ln_b, conv_w, conv_b,
                                         conv_ln_g, conv_ln_b, conv_pw, conv_pw_b, t_len)

    nsa3 = nsa.reshape(b, s_len, NSA_W)
    kc, vct = _nsa_compress(nsa3, nsa_pos, nsa_w1, nsa_w2)
    ya = _nsa_attention(nsa3, kc, vct, bt, pt, imat).reshape(n, w)

    lora = RWKV_LORA
    mu_p = jnp.concatenate([rwkv_mu, jnp.zeros((RW_W - rwkv_mu.shape[0],), F32)]).reshape(1, RW_W)
    wup_p = jnp.concatenate([rwkv_w_up, jnp.zeros((LANES - lora, w), F32)], axis=0)
    aup_p = jnp.concatenate([jnp.zeros((lora, w), F32), rwkv_a_up, jnp.zeros((LANES - 2 * lora, w), F32)], axis=0)
    vec = jnp.stack([rwkv_w0, rwkv_a0, rwkv_k_k, rwkv_k_a, rwkv_r_k.reshape(w), rwkv_gn_g, rwkv_gn_b,
                     jnp.zeros((w,), F32)], axis=0)
    yd = _rwkv(rw.reshape(b, s_len, RW_W), mu_p, wup_p, aup_p, vec).reshape(n, w)

    return _out_proj((ya, yb, yc, yd), x2, p3, layer, wo_all, g_post, pp_all, pg_all)


def kernel(x, p, rel_bias, w_in, w_out, g_pre, g_post, nsa_pos, nsa_w1, nsa_w2, sgu_ln_g, sgu_ln_b, sgu_w, sgu_b, conv_w, conv_b, conv_ln_g, conv_ln_b, conv_pw, conv_pw_b, rwkv_mu, rwkv_w0, rwkv_w_up, rwkv_a0, rwkv_a_up, rwkv_k_k, rwkv_k_a, rwkv_r_k, rwkv_gn_g, rwkv_gn_b, ple_proj, ple_gate):
    b, s_len, d = x.shape
    depth = w_in.shape[0]
    bt, pt, imat = _nsa_tables(rel_bias, s_len)
    x2 = x.reshape(b * s_len, d)
    p3 = p.reshape(depth, b * s_len, p.shape[-1])
    w_p_all = _prep_w_in(w_in)
    wo_all, pp_all, pg_all = w_out.astype(BF16), ple_proj.astype(BF16), ple_gate.astype(BF16)
    per_layer = (g_pre, g_post, nsa_pos, nsa_w1, nsa_w2, sgu_ln_g, sgu_ln_b, sgu_w, sgu_b,
                 conv_w, conv_b, conv_ln_g, conv_ln_b, conv_pw, conv_pw_b, rwkv_mu, rwkv_w0, rwkv_w_up,
                 rwkv_a0, rwkv_a_up, rwkv_k_k, rwkv_k_a, rwkv_r_k, rwkv_gn_g, rwkv_gn_b)
    for i in range(depth):
        x2 = _layer(x2, p3, i, b, s_len, bt, pt, imat, w_p_all, wo_all, pp_all, pg_all,
                    *[a[i] for a in per_layer])
    return x2.reshape(b, s_len, d)
```

```python
import functools
import math

import numpy as np
import jax
import jax.numpy as jnp
from jax import lax
from jax.experimental import pallas as pl
from jax.experimental.pallas import tpu as pltpu

F32 = jnp.float32
BF16 = jnp.bfloat16

W_GRP = 256
HEAD_DIM = 64
N_HEADS = 4
NSA_DK = 64
CMP_STRIDE = 16
CMP_BLOCK = 32
CMP_HIDDEN = 128
SEL_BLOCK = 64
N_SEL = 16
WINDOW = 512
REL_BUCKETS = 32
REL_MAX_EXACT = 16
REL_MAX_DIST = 128
CONV_WIDTH = 31
RWKV_LORA = 32
RWKV_GN_EPS = 64e-5
NORM_EPS = 1e-6
LN_EPS = 1e-5
NEG = -1e30
FORCE = 1e4
LOG2E = math.log2(math.e)

LANES = 128
SUBLANES = 8
Q_TILE = 128
SEL_GROUP = 4
NSA_STREAMS = 4
RW_CHUNK = 64
RW_SUB = 4
VMEM_LIMIT = 48 * 1024 * 1024
NSA_VMEM_LIMIT = 56 * 1024 * 1024

NSA_W, GM_W, CV_W, RW_W = 1024, 768, 768, 1152
N_PROJ = NSA_W + GM_W + CV_W + RW_W


def _proj_segments():
    names = ["q", "kc", "vc", "ks", "vs", "kw", "vw", "g", "za", "u", "v", "zb", "ga", "gb", "zc", "rw", "zd"]
    widths = [256, 64, 64, 64, 64, 64, 64, 12, 256, 256, 256, 256, 256, 256, 256, 832, 256]
    o, off = {}, 0
    for n, w in zip(names, widths):
        o[n] = (off, off + w)
        off += w
    return [o["q"], o["za"], (o["ks"][0], o["vw"][1]), (o["kc"][0], o["vc"][1]), o["g"], (None, 116),
            (o["u"][0], o["zc"][1]), o["rw"], (None, 64), o["zd"]]


_SEGMENTS = _proj_segments()


def _rel_buckets(n):
    d = np.arange(n)
    nf = np.maximum(d, REL_MAX_EXACT).astype(np.float32)
    large = REL_MAX_EXACT + (np.log(nf / np.float32(REL_MAX_EXACT)) / np.float32(math.log(REL_MAX_DIST / REL_MAX_EXACT))
                             * np.float32(REL_BUCKETS - REL_MAX_EXACT)).astype(np.int32)
    large = np.minimum(large, REL_BUCKETS - 1)
    return np.where(d < REL_MAX_EXACT, d, large)


def _bdot(a, b):
    return jnp.dot(a.astype(BF16), b.astype(BF16), preferred_element_type=F32)


def _nt(a, b):
    return lax.dot_general(a, b, (((1,), (1,)), ((), ())), preferred_element_type=F32)


def _split2(a):
    hi = a.astype(BF16)
    lo = (a - hi.astype(F32)).astype(BF16)
    return hi, lo


def _split3(a):
    hi = a.astype(BF16)
    r = a - hi.astype(F32)
    mid = r.astype(BF16)
    lo = (r - mid.astype(F32)).astype(BF16)
    return hi, mid, lo


def _dot3(a, b):
    ah, al = _split2(a)
    bh, bl = _split2(b)
    d = lambda x, y: jnp.dot(x, y, preferred_element_type=F32)
    return d(ah, bh) + (d(ah, bl) + d(al, bh))


def _dot_sel_lhs(a01, b):
    a = a01.astype(BF16)
    bh, bm, bl = _split3(b)
    d = lambda y: jnp.dot(a, y, preferred_element_type=F32)
    return d(bh) + (d(bm) + d(bl))


def _sigmoid(x):
    return 1.0 / (1.0 + jnp.exp(-x))


def _silu(x):
    return x * _sigmoid(x)


def _gelu_tanh(x):
    c = math.sqrt(2.0 / math.pi)
    return 0.5 * x * (1.0 + jnp.tanh(c * (x + 0.044715 * (x * x * x))))


def _layer_norm(x, g, b):
    mu = jnp.mean(x, axis=-1, keepdims=True)
    xc = x - mu
    var = jnp.mean(xc * xc, axis=-1, keepdims=True)
    return xc * lax.rsqrt(var + LN_EPS) * g + b


def _block_diag(x, n):
    r, c = x.shape
    t = jnp.concatenate([x] * n, axis=0)
    ri = lax.broadcasted_iota(jnp.int32, t.shape, 0) // r
    ci = lax.broadcasted_iota(jnp.int32, t.shape, 1) // (c // n)
    return jnp.where(ri == ci, t, 0.0)


def _cparams(sem, vmem=VMEM_LIMIT):
    return pltpu.CompilerParams(dimension_semantics=sem, vmem_limit_bytes=vmem)


def _cmp_kernel(kv_ref, pos_ref, w1_ref, w2_ref, o_ref, ot_ref):
    nc = o_ref.shape[0]
    dk = o_ref.shape[1]
    first = None
    second = None
    for r in range(CMP_STRIDE):
        x = kv_ref[pl.ds(r, nc, stride=CMP_STRIDE), :]
        t = _bdot(x + pos_ref[r], w1_ref[r])
        first = t if first is None else first + t
        t = _bdot(x + pos_ref[CMP_STRIDE + r], w1_ref[CMP_STRIDE + r])
        second = t if second is None else second + t
    hid = first + pltpu.roll(second, nc - 1, axis=0)
    out = _bdot(_silu(hid), w2_ref[...])
    o_ref[...] = out[:, :dk]
    ot_ref[...] = out.T[dk:, :]


def _nsa_compress(nsa3, pos, w1, w2):
    b, s_len, _ = nsa3.shape
    nc = s_len // CMP_STRIDE
    dk = NSA_DK
    zw = jnp.zeros((CMP_BLOCK, dk, CMP_HIDDEN), F32)
    w1r = w1.reshape(2, CMP_BLOCK, dk, CMP_HIDDEN)
    w1bd = jnp.concatenate([jnp.concatenate([w1r[0], zw], axis=2),
                            jnp.concatenate([zw, w1r[1]], axis=2)], axis=1).astype(BF16)
    z2 = jnp.zeros((CMP_HIDDEN, dk), F32)
    w2bd = jnp.concatenate([jnp.concatenate([w2[0], z2], axis=1),
                            jnp.concatenate([z2, w2[1]], axis=1)], axis=0).astype(BF16)
    posr = jnp.concatenate([pos[0], pos[1]], axis=1).reshape(CMP_BLOCK, 1, 2 * dk)
    full = lambda a: pl.BlockSpec(a.shape, lambda i: (0,) * a.ndim)
    return pl.pallas_call(
        _cmp_kernel,
        grid=(b,),
        in_specs=[pl.BlockSpec((None, s_len, LANES), lambda i: (i, 0, 6)),
                  full(posr), full(w1bd), full(w2bd)],
        out_specs=[pl.BlockSpec((None, nc, dk), lambda i: (i, 0, 0)),
                   pl.BlockSpec((None, dk, nc), lambda i: (i, 0, 0))],
        out_shape=[jax.ShapeDtypeStruct((b, nc, dk), F32),
                   jax.ShapeDtypeStruct((b, dk, nc), F32)],
        compiler_params=_cparams(("parallel",)),
        name="nsa_compress",
    )(nsa3, posr, w1bd, w2bd)


def _nsa_kernel(q_ref, za_ref, g_ref, ks_ref, kw_ref, kc_ref, vct_ref, pt_ref, bt_ref, imat_ref,
                      o_ref, ksb, vst, kwb, vwt, madd, val_s, cnt_s, pc_s, yc_s, m_s, acc_s, *, n_top, n_win):
    c = pl.program_id(1)
    n_qt = pl.num_programs(1)
    n_st = q_ref.shape[0]
    n_kt = ksb.shape[1]
    hd = HEAD_DIM
    nh = N_HEADS
    n_blk = imat_ref.shape[0]
    per_q = Q_TILE // SEL_BLOCK
    v_rows = vst.shape[2]
    streams = range(n_st)

    @pl.when(c == 0)
    def _prep():
        key_blk = lax.broadcasted_iota(jnp.int32, (Q_TILE, n_blk), 0) // SEL_BLOCK
        col = lax.broadcasted_iota(jnp.int32, (Q_TILE, n_blk), 1)
        ones_row = jnp.where(lax.broadcasted_iota(jnp.int32, (v_rows - hd, Q_TILE), 0) == 0, 1.0, 0.0)

        def body(i, carry):
            r0 = pl.multiple_of(i * Q_TILE, Q_TILE)
            onehot = jnp.where(col == key_blk + per_q * i, 1.0, 0.0)
            for s in streams:
                t = ks_ref[s, pl.ds(r0, Q_TILE), :]
                ksb[s, i] = jnp.concatenate([t[:, :hd], onehot], axis=1).astype(BF16)
                vst[s, i] = jnp.concatenate([t.T[hd:, :], ones_row], axis=0).astype(BF16)
                t = kw_ref[s, pl.ds(r0, Q_TILE), :]
                kwb[s, i] = t[:, :hd].astype(BF16)
                vwt[s, i] = jnp.concatenate([t.T[hd:, :], ones_row], axis=0).astype(BF16)
            return carry
        lax.fori_loop(0, n_kt, body, 0)

    def reset_state():
        m_s[...] = jnp.full(m_s.shape, NEG, F32)
        acc_s[...] = jnp.zeros(acc_s.shape, F32)

    def update(groups):
        m_old = [m_s[s] for s in streams]
        m_new = []
        for s in streams:
            m = m_old[s]
            for s_i in groups[s][0]:
                m = jnp.maximum(m, jnp.max(s_i, axis=0, keepdims=True))
            m_new.append(m)
        for s in streams:
            ss, vs = groups[s]
            pv = None
            for j in range(0, len(ss), 2):
                p_j = jnp.concatenate([jnp.exp2(s_i - m_new[s]).astype(BF16) for s_i in ss[j:j + 2]], axis=0)
                v_j = jnp.concatenate(vs[j:j + 2], axis=1)
                t = jnp.dot(v_j, p_j, preferred_element_type=F32)
                pv = t if pv is None else pv + t
            acc_s[s] = jnp.exp2(m_old[s] - m_new[s]) * acc_s[s] + pv
            m_s[s] = m_new[s]

    def branch_out(s, g):
        acc = acc_s[s]
        return (g / acc[hd:hd + 1, :]) * acc[:hd, :]

    q_all, qb, gate = [], [], []
    for s in streams:
        q = q_ref[s] * (hd ** -0.5 * LOG2E)
        qa = jnp.concatenate([q[:, hd * h:hd * (h + 1)] for h in range(nh)], axis=0)
        q_all.append(qa)
        qb.append(qa.astype(BF16))
        g_t = _sigmoid(g_ref[s]).T
        gate.append([jnp.concatenate([g_t[3 * h + br:3 * h + br + 1, :] for h in range(nh)], axis=1)
                     for br in range(3)])

    reset_state()
    groups = []
    for s in streams:
        ss, vts = [], []
        for i in range(n_win + 1):
            kt = jnp.maximum(c - i, 0)
            s_i = _nt(kwb[s, kt], qb[s])
            if i == 0:
                s_i = s_i + bt_ref[0]
            else:
                if i == 1:
                    s_i = s_i + bt_ref[1]
                elif i == n_win:
                    s_i = s_i + bt_ref[2]
                s_i = s_i + jnp.where(c >= i, 0.0, NEG)
            ss.append(s_i)
            vts.append(vwt[s, kt])
        groups.append((ss, vts))
    update(groups)
    y_acc = [branch_out(s, gate[s][2]) for s in streams]

    n_c = kc_ref.shape[1]
    per_tile = Q_TILE // CMP_STRIDE
    off = pl.multiple_of((n_qt - 1 - c) * per_tile, SUBLANES)

    def cmp_branch(rows):
        bias = pt_ref[pl.ds(off, rows), :]
        valid = bias > 0.5 * NEG
        scs = [_nt(kc_ref[s, 0:rows, :].astype(BF16), qb[s]) + bias for s in streams]
        ps = [jnp.where(valid, jnp.exp2(sc - jnp.max(sc, axis=0, keepdims=True)), 0.0) for sc in scs]
        ls = [jnp.sum(p, axis=0, keepdims=True) for p in ps]
        pns = [p * jnp.where(l > 0.0, 1.0 / l, 0.0) for p, l in zip(ps, ls)]
        for s in streams:
            yc_s[s] = gate[s][0] * _bdot(vct_ref[s, :, 0:rows], pns[s])
            pc = pns[s][:, 0:Q_TILE]
            for h in range(1, nh):
                pc = pc + pns[s][:, h * Q_TILE:(h + 1) * Q_TILE]
            pc_s[s, 0:rows, :] = pc
            if rows < n_c:
                pc_s[s, rows:n_c, :] = jnp.zeros((n_c - rows, Q_TILE), F32)

    half = n_c // 2
    few = (c + 1) * per_tile <= half
    pl.when(few)(lambda: cmp_branch(half))
    pl.when(jnp.logical_not(few))(lambda: cmp_branch(n_c))

    tl = lax.broadcasted_iota(jnp.int32, (n_blk, Q_TILE), 1)
    jb = lax.broadcasted_iota(jnp.int32, (n_blk, Q_TILE), 0)
    cur = c * per_q + tl // SEL_BLOCK
    causal = jb <= cur
    need_rank = (c + 1) * per_q > n_top

    @pl.when(jnp.logical_not(need_rank))
    def _all_causal():
        for s in streams:
            madd[s] = jnp.where(causal, 0.0, NEG)

    @pl.when(need_rank)
    def _rank():
        forced = (jb == 0) | (jb == cur) | (jb == cur - 1)
        n_r = n_blk // SUBLANES
        jrow = lax.broadcasted_iota(jnp.int32, (SUBLANES, Q_TILE), 0)
        for s in streams:
            imp = _dot_sel_lhs(imat_ref[...], pc_s[s])
            val_s[s] = jnp.where(causal, jnp.where(forced, FORCE, imp), NEG)
            cnt_s[s] = jnp.zeros((n_blk, Q_TILE), F32)
        for ci in range(n_r):
            @pl.when(SUBLANES * ci < (c + 1) * per_q)
            def _chunk():
                for s in streams:
                    val = val_s[s]
                    blocks = [val[SUBLANES * r:SUBLANES * (r + 1), :] for r in range(n_r)]
                    cnts = [None] * n_r
                    for i in range(SUBLANES * ci, SUBLANES * (ci + 1)):
                        vi = jnp.broadcast_to(val[i:i + 1, :], (SUBLANES, Q_TILE))
                        for r in range(n_r):
                            if r > ci:
                                t = jnp.where(vi >= blocks[r], 1.0, 0.0)
                            elif r < ci:
                                t = jnp.where(vi > blocks[r], 1.0, 0.0)
                            else:
                                ge = jnp.where(vi >= blocks[r], 1.0, 0.0)
                                gt = jnp.where(vi > blocks[r], 1.0, 0.0)
                                t = jnp.where(jrow + SUBLANES * r > i, ge, gt)
                            cnts[r] = t if cnts[r] is None else cnts[r] + t
                    cnt_s[s] = cnt_s[s] + jnp.concatenate(cnts, axis=0)
        for s in streams:
            madd[s] = jnp.where(causal & (cnt_s[s] < float(n_top)), 0.0, NEG)

    reset_state()
    pad = jnp.zeros((LANES - n_blk, Q_TILE), F32)
    kt1 = jnp.maximum(c - 1, 0)
    q_far, groups = [], []
    for s in streams:
        m_t = jnp.concatenate([madd[s], pad], axis=0).T[:, :n_blk]
        q_mask = jnp.concatenate([q_all[s], jnp.concatenate([m_t] * nh, axis=0)], axis=1)
        lane = lax.broadcasted_iota(jnp.int32, q_mask.shape, 1)
        q_near = q_mask.astype(BF16)
        q_far.append(jnp.where(lane >= hd + per_q * jnp.maximum(c - 1, 0), NEG, q_mask).astype(BF16))
        s0 = _nt(ksb[s, c], q_near) + bt_ref[0]
        s1 = _nt(ksb[s, kt1], q_near) + bt_ref[1] + jnp.where(c >= 1, 0.0, NEG)
        groups.append(([s0, s1], [vst[s, c], vst[s, kt1]]))

    update(groups)

    n_far = jnp.maximum(c - 1, 0)

    def far_body(g, carry):
        kts = [g * SEL_GROUP + j for j in range(SEL_GROUP)]
        update([([_nt(ksb[s, kt], q_far[s]) for kt in kts], [vst[s, kt] for kt in kts]) for s in streams])
        return carry
    lax.fori_loop(0, (n_far + SEL_GROUP - 1) // SEL_GROUP, far_body, 0)

    for s in streams:
        y = y_acc[s] + yc_s[s] + branch_out(s, gate[s][1])
        y_t = jnp.concatenate([y[:, h * Q_TILE:(h + 1) * Q_TILE] for h in range(nh)], axis=0)
        o_ref[s] = y_t.T * _silu(za_ref[s])


def _bucket_lookup(relc, idx, visible):
    r, cc = idx.shape
    onehot = (jnp.asarray(idx.reshape(1, -1)) == jnp.arange(relc.shape[0])[:, None]).astype(F32)
    vals = jnp.einsum("bh,bn->hn", relc, onehot, precision=lax.Precision.HIGHEST).reshape(-1, r, cc)
    vals = jnp.where(jnp.asarray(visible)[None], vals, NEG)
    return jnp.transpose(vals, (1, 0, 2)).reshape(r, -1).astype(F32)


def _nsa_tables(rel_bias, s_len):
    n_qt = s_len // Q_TILE
    n_c = s_len // CMP_STRIDE
    relc = rel_bias - rel_bias[REL_BUCKETS - 1][None, :]
    bk = _rel_buckets(s_len + Q_TILE)
    kl = np.arange(Q_TILE)[:, None]
    tq = np.arange(Q_TILE)[None, :]
    d0 = tq - kl
    diag = _bucket_lookup(relc, bk[np.clip(d0, 0, None)], d0 >= 0)
    prev = _bucket_lookup(relc, bk[Q_TILE + d0], np.ones_like(d0, bool))
    edge = jnp.asarray(np.tile(np.where(kl > tq, 0.0, NEG).astype(np.float32), (1, N_HEADS)))
    bt = jnp.stack([diag, prev, edge], axis=0) * LOG2E
    per_q = Q_TILE // CMP_STRIDE
    n_rows = n_c + per_q * (n_qt - 1)
    r = np.arange(n_rows)[:, None]
    dc = tq - CMP_STRIDE * (r - per_q * (n_qt - 1)) - (CMP_BLOCK - 1)
    pt = _bucket_lookup(relc, bk[np.clip(dc, 0, None)], dc >= 0) * LOG2E
    n_blk = s_len // SEL_BLOCK
    ratio = SEL_BLOCK // CMP_STRIDE
    jj = np.arange(n_blk)[:, None]
    ii = np.arange(n_c)[None, :]
    imat = ((ii >= ratio * jj - 1) & (ii <= ratio * jj + ratio - 1) & (ii < n_c - 1)).astype(np.float32)
    return bt, pt, jnp.asarray(imat)


def _nsa_attention(nsa3, kc, vct, bt, pt, imat):
    b, s_len, _ = nsa3.shape
    n_qt = s_len // Q_TILE
    n_c = s_len // CMP_STRIDE
    n_blk = s_len // SEL_BLOCK
    n_win = WINDOW // Q_TILE
    hd = HEAD_DIM
    hq = N_HEADS * Q_TILE
    v_rows = hd + 2 * SUBLANES
    n_st = NSA_STREAMS if b % NSA_STREAMS == 0 else 1
    kern = functools.partial(_nsa_kernel, n_top=min(N_SEL, n_blk), n_win=n_win)
    return pl.pallas_call(
        kern,
        grid=(b // n_st, n_qt),
        in_specs=[pl.BlockSpec((n_st, Q_TILE, 256), lambda i, c: (i, c, 0)),
                  pl.BlockSpec((n_st, Q_TILE, 256), lambda i, c: (i, c, 1)),
                  pl.BlockSpec((n_st, Q_TILE, LANES), lambda i, c: (i, c, 7)),
                  pl.BlockSpec((n_st, s_len, LANES), lambda i, c: (i, 0, 4),
                               pipeline_mode=pl.Buffered(1)),
                  pl.BlockSpec((n_st, s_len, LANES), lambda i, c: (i, 0, 5),
                               pipeline_mode=pl.Buffered(1)),
                  pl.BlockSpec((n_st, n_c, hd), lambda i, c: (i, 0, 0)),
                  pl.BlockSpec((n_st, hd, n_c), lambda i, c: (i, 0, 0)),
                  pl.BlockSpec(pt.shape, lambda i, c: (0, 0)),
                  pl.BlockSpec(bt.shape, lambda i, c: (0, 0, 0)),
                  pl.BlockSpec(imat.shape, lambda i, c: (0, 0))],
        out_specs=pl.BlockSpec((n_st, Q_TILE, 256), lambda i, c: (i, c, 0)),
        out_shape=jax.ShapeDtypeStruct((b, s_len, 256), F32),
        scratch_shapes=[pltpu.VMEM((n_st, n_qt, Q_TILE, hd + n_blk), BF16),
                        pltpu.VMEM((n_st, n_qt, v_rows, Q_TILE), BF16),
                        pltpu.VMEM((n_st, n_qt, Q_TILE, hd), BF16),
                        pltpu.VMEM((n_st, n_qt, v_rows, Q_TILE), BF16),
                        pltpu.VMEM((n_st, n_blk, Q_TILE), F32),
                        pltpu.VMEM((n_st, n_blk, Q_TILE), F32),
                        pltpu.VMEM((n_st, n_blk, Q_TILE), F32),
                        pltpu.VMEM((n_st, n_c, Q_TILE), F32),
                        pltpu.VMEM((n_st, hd, hq), F32),
                        pltpu.VMEM((n_st, 1, hq), F32),
                        pltpu.VMEM((n_st, v_rows, hq), F32)],
        compiler_params=_cparams(("arbitrary", "arbitrary"), NSA_VMEM_LIMIT),
        name="nsa_attention",
    )(nsa3, nsa3, nsa3, nsa3, nsa3, kc, vct, pt, bt, imat)


def _proj_gc_kernel(x_ref, g_ref, w_ref, wall_ref, sb_ref, lng_ref, lnb_ref, cw_ref, cb_ref, clg_ref, clb_ref,
                    cpw_ref, cpb_ref, o_nsa, o_rw, ob_ref, oc_ref, xs, ph_s, *, tiles_per_seq, t_len):
    first = (pl.program_id(0) % tiles_per_seq) == 0
    tm = x_ref.shape[0]
    halo = xs.shape[0] - tm - SUBLANES
    w = W_GRP
    x = x_ref[...]
    ms = jnp.mean(x * x, axis=-1, keepdims=True)
    h = (x * lax.rsqrt(ms + NORM_EPS) * g_ref[...]).astype(BF16)
    c_gm, c_cv, c_rw = NSA_W, NSA_W + GM_W, NSA_W + GM_W + CV_W
    gm = jnp.dot(h, w_ref[:, c_gm:c_cv], preferred_element_type=F32)
    cv = jnp.dot(h, w_ref[:, c_cv:c_rw], preferred_element_type=F32)

    u = _gelu_tanh(gm[:, :w])
    v = _layer_norm(_gelu_tanh(gm[:, w:2 * w]), lng_ref[...], lnb_ref[...])
    zb = _silu(gm[:, 2 * w:])
    wall = wall_ref[...]
    ti = lax.broadcasted_iota(jnp.int32, wall.shape, 0)
    si = lax.broadcasted_iota(jnp.int32, wall.shape, 1) % t_len
    wall = jnp.where(si <= ti, wall, 0.0).astype(BF16)
    for ch in range(tm // t_len):
        r0, r1 = ch * t_len, (ch + 1) * t_len
        sv = jnp.dot(wall, _block_diag(v[r0:r1], N_HEADS).astype(BF16), preferred_element_type=F32) + sb_ref[...]
        ob_ref[r0:r1, :] = u[r0:r1] * sv * zb[r0:r1]

    o_nsa[...] = jnp.dot(h, w_ref[:, :c_gm], preferred_element_type=F32)

    o_rw[...] = jnp.dot(h, w_ref[:, c_rw:], preferred_element_type=F32)
    xs[0:halo, :] = jnp.where(first, 0.0, xs[0:halo, :])
    xs[halo:halo + tm, :] = cv[:, :w] * _sigmoid(cv[:, w:2 * w])
    xs[halo + tm:, :] = jnp.zeros((xs.shape[0] - halo - tm, w), F32)
    cw = cw_ref[...]
    for ph in range(1, SUBLANES):
        ph_s[ph - 1] = xs[ph:ph + halo + tm, :]
    base = halo - (CONV_WIDTH - 1)
    acc = jnp.zeros((tm, w), F32)
    for j in range(CONV_WIDTH):
        a, ph = divmod(base + j, SUBLANES)
        r0 = SUBLANES * a
        win = xs[r0:r0 + tm, :] if ph == 0 else ph_s[ph - 1, r0:r0 + tm, :]
        acc = acc + cw[j:j + 1, :] * win
    xs[0:halo, :] = xs[tm:tm + halo, :]
    y = _layer_norm(acc + cb_ref[...], clg_ref[...], clb_ref[...])
    y = _bdot(_silu(y), cpw_ref[...]) + cpb_ref[...]
    oc_ref[...] = y * _silu(cv[:, 2 * w:])


def _in_proj_gmlp_conv(x2, g_pre, w_p_all, layer, s_len, wall, sb, lng, lnb, cw, cb, clg, clb, cpw, cpb, t_len, tm=512, halo=32):
    n, d = x2.shape
    w = W_GRP
    row = lambda a: a.reshape(1, w)
    full = lambda a: pl.BlockSpec(a.shape, lambda i: (0,) * a.ndim)
    cwp = jnp.concatenate([cw, jnp.zeros((32 - CONV_WIDTH, w), F32)], axis=0)
    args = (wall, sb, row(lng), row(lnb), cwp, row(cb), row(clg), row(clb), cpw.astype(BF16), row(cpb))
    widths = (NSA_W, RW_W, w, w)
    return pl.pallas_call(
        functools.partial(_proj_gc_kernel, tiles_per_seq=s_len // tm, t_len=t_len),
        grid=(n // tm,),
        in_specs=[pl.BlockSpec((tm, d), lambda i: (i, 0)),
                  pl.BlockSpec((1, d), lambda i: (0, 0)),
                  pl.BlockSpec((None, d, N_PROJ), lambda i: (layer, 0, 0))] + [full(a) for a in args],
        out_specs=[pl.BlockSpec((tm, wd), lambda i: (i, 0)) for wd in widths],
        out_shape=[jax.ShapeDtypeStruct((n, wd), F32) for wd in widths],
        scratch_shapes=[pltpu.VMEM((halo + tm + SUBLANES, w), F32),
                        pltpu.VMEM((SUBLANES - 1, halo + tm, w), F32)],
        compiler_params=_cparams(("arbitrary",)),
        name="in_proj_gmlp_conv",
    )(x2, g_pre.reshape(1, d), w_p_all, *args)


def _rwkv_kernel(rw_ref, prev_ref, mu_ref, wup_ref, aup_ref, vec_ref, o_ref, st):
    c = pl.program_id(0)
    n_b = rw_ref.shape[0]
    L = RW_CHUNK
    n_sub = rw_ref.shape[1] // L
    w = W_GRP
    nh = N_HEADS
    hd = HEAD_DIM

    @pl.when(c == 0)
    def _init():
        st[...] = jnp.zeros(st.shape, F32)

    vec = vec_ref[...]
    w0, a0, k_k, k_a, r_k, gn_g, gn_b = [vec[i:i + 1, :] for i in range(7)]
    mu = mu_ref[...]
    rows = lax.broadcasted_iota(jnp.int32, (L, w), 0)
    lane = lax.broadcasted_iota(jnp.int32, (L, w), 1)
    s_of = lane % hd
    ones_bd = jnp.where((lax.broadcasted_iota(jnp.int32, (w, w), 0) // hd)
                        == (lax.broadcasted_iota(jnp.int32, (w, w), 1) // hd), 1.0, 0.0)
    tri = jnp.where(lax.broadcasted_iota(jnp.int32, (L, L), 1) <= lax.broadcasted_iota(jnp.int32, (L, L), 0), 1.0, 0.0)
    bd_mask = (lax.broadcasted_iota(jnp.int32, (w, w), 0) // hd) == (lax.broadcasted_iota(jnp.int32, (w, w), 1) // hd)
    strict = s_of < rows
    incl = s_of <= rows
    eye_all = jnp.where(s_of == rows, 1.0, 0.0)

    def bd(x):
        xb = x.astype(BF16)
        return jnp.where(bd_mask, jnp.concatenate([xb] * nh, axis=0), jnp.zeros((), BF16))

    def mm(a, b):
        return jnp.dot(a.astype(BF16), b.astype(BF16), preferred_element_type=F32)

    nb = range(n_b)
    probs = range(n_b * n_sub)
    stack = lambda parts: jnp.concatenate(parts, axis=0)
    part = lambda x, p: x[p * L:(p + 1) * L]
    zs = [rw_ref[b] for b in nb]
    zrow = lax.broadcasted_iota(jnp.int32, zs[0].shape, 0)
    n_prev = prev_ref.shape[1]
    xs = []
    for b in nb:
        last = jnp.where(c == 0, 0.0, prev_ref[b, n_prev - 1:n_prev, :])
        zprev = jnp.where(zrow == 0, last, pltpu.roll(zs[b], 1, axis=0))
        xs.append(zs[b] + mu * (zprev - zs[b]))
    xs = stack(xs)
    r = xs[:, 0:w]
    k = xs[:, w:2 * w]
    v = xs[:, 2 * w:3 * w]
    wa = xs[:, 3 * w:3 * w + LANES]
    zd = stack([z[:, 3 * w + LANES:] for z in zs])

    zz = w0 + _dot3(jnp.tanh(wa), wup_ref[...])
    lw = (-math.exp(-0.5)) * _sigmoid(zz)
    a = _sigmoid(a0 + _dot3(wa, aup_ref[...]))
    kkr = k * k_k
    kk = kkr * lax.rsqrt(jnp.maximum(mm(kkr * kkr, ones_bd), 1e-24))
    k2 = k * (1.0 + (a - 1.0) * k_a)
    bb = kk * a

    lw_wide = jnp.concatenate([part(lw, p) for p in probs], axis=1)
    cs_wide = _dot_sel_lhs(tri, lw_wide)
    cs = stack([cs_wide[:, p * w:(p + 1) * w] for p in probs])
    g_t = jnp.exp(cs)
    g_inv = jnp.exp(-cs)
    kq = (kk * jnp.exp(cs - lw)).astype(BF16)
    rq = (r * g_t).astype(BF16)
    kt = k2 * g_inv
    bt = bb * g_inv

    lhs = [stack([part(kq, p), part(rq, p)]) for p in probs]
    ab_b = [_nt(lhs[p], bd(part(bt, p))) for p in probs]
    ab_k = [_nt(lhs[p], bd(part(kt, p))) for p in probs]
    a_b = [jnp.where(strict, x[:L], 0.0) for x in ab_b]
    b_b = [jnp.where(incl, x[L:], 0.0) for x in ab_b]
    ak_bk = [stack([jnp.where(strict, x[:L], 0.0), jnp.where(incl, x[L:], 0.0)]) for x in ab_k]
    akv = [mm(ak_bk[p], bd(part(v, p))) for p in probs]

    npow = a_b
    tinv = [eye_all - x for x in a_b]
    for i in range(int(math.log2(L)) - 1):
        npow = [mm(x, bd(x)) for x in npow]
        tinv = [t + mm(t, bd(x)) for t, x in zip(tinv, npow)]

    zpad = jnp.zeros((w - 2 * L, w), F32)
    y = [None] * len(probs)
    for sub in range(n_sub):
        ps = [b * n_sub + sub for b in nb]
        s0 = [st[b] for b in nb]
        kh = [_nt(lhs[p], s0[b].astype(BF16)) for b, p in zip(nb, ps)]
        u = [mm(tinv[p], bd(kh[b][:L] + akv[p][:L])) for b, p in zip(nb, ps)]
        for b, p in zip(nb, ps):
            y[p] = kh[b][L:] + akv[p][L:] - mm(b_b[p], bd(u[b]))
        for b, p in zip(nb, ps):
            vu_t = stack([part(v, p), u[b], zpad]).T
            kb = stack([part(kt, p), -part(bt, p), zpad])
            d = mm(vu_t, kb)
            st[b] = g_t[(p + 1) * L - 1:(p + 1) * L, :] * (s0[b] + jnp.where(bd_mask, d, 0.0))

    y = stack(y)
    y_hi, y_lo = _split2(y)
    mean = (mm(y_hi, ones_bd) + mm(y_lo, ones_bd)) * (1.0 / hd)
    yc = y - mean
    var = mm(yc * yc, ones_bd) * (1.0 / hd)
    yn = yc * lax.rsqrt(var + RWKV_GN_EPS) * gn_g + gn_b
    bonus = mm(r * k2 * r_k, ones_bd) * v
    out = (yn + bonus) * _silu(zd)
    for b in nb:
        o_ref[b] = out[b * n_sub * L:(b + 1) * n_sub * L]


def _rwkv(rw3, mu_p, wup_p, aup_p, vec):
    b, s_len, _ = rw3.shape
    rows = RW_CHUNK * RW_SUB
    w = W_GRP
    full = lambda a: pl.BlockSpec(a.shape, lambda c: (0,) * a.ndim)
    return pl.pallas_call(
        _rwkv_kernel,
        grid=(s_len // rows,),
        in_specs=[pl.BlockSpec((b, rows, RW_W), lambda c: (0, c, 0)),
                  pl.BlockSpec((b, SUBLANES, RW_W), lambda c: (0, jnp.maximum(c * (rows // SUBLANES) - 1, 0), 0)),
                  full(mu_p), full(wup_p), full(aup_p), full(vec)],
        out_specs=pl.BlockSpec((b, rows, w), lambda c: (0, c, 0)),
        out_shape=jax.ShapeDtypeStruct((b, s_len, w), F32),
        scratch_shapes=[pltpu.VMEM((b, w, w), F32)],
        compiler_params=_cparams(("arbitrary",)),
        name="rwkv7",
    )(rw3, rw3, mu_p, wup_p, aup_p, vec)


def _out_kernel(ya_ref, yb_ref, yc_ref, yd_ref, x_ref, p_ref, wo_ref, gp_ref, pp_ref, pg_ref, o_ref):
    w = W_GRP
    acc = None
    for i, y in enumerate((ya_ref, yb_ref, yc_ref, yd_ref)):
        t = jnp.dot(y[...].astype(BF16), wo_ref[i * w:(i + 1) * w, :], preferred_element_type=F32)
        acc = t if acc is None else acc + t
    ms = jnp.mean(acc * acc, axis=-1, keepdims=True)
    x1 = x_ref[...] + acc * lax.rsqrt(ms + NORM_EPS) * gp_ref[...]
    gate = _sigmoid(jnp.dot(x1.astype(BF16), pg_ref[...], preferred_element_type=F32))
    pe = jnp.dot(p_ref[...].astype(BF16), pp_ref[...], preferred_element_type=F32)
    o_ref[...] = x1 + gate * pe


def _out_proj(ys, x2, p3, layer, wo_all, g_post, pp_all, pg_all, tm=512):
    n, d = x2.shape
    w = W_GRP
    gp = g_post.reshape(1, d)
    of_layer = lambda a: pl.BlockSpec((None,) + a.shape[1:], lambda i: (layer,) + (0,) * (a.ndim - 1))
    return pl.pallas_call(
        _out_kernel,
        grid=(n // tm,),
        in_specs=[pl.BlockSpec((tm, w), lambda i: (i, 0))] * 4
                 + [pl.BlockSpec((tm, d), lambda i: (i, 0)),
                    pl.BlockSpec((None, tm, p3.shape[2]), lambda i: (layer, i, 0)),
                    of_layer(wo_all), pl.BlockSpec(gp.shape, lambda i: (0, 0)), of_layer(pp_all), of_layer(pg_all)],
        out_specs=pl.BlockSpec((tm, d), lambda i: (i, 0)),
        out_shape=jax.ShapeDtypeStruct((n, d), F32),
        compiler_params=_cparams(("parallel",)),
        name="out_proj",
    )(*ys, x2, p3, wo_all, gp, pp_all, pg_all)


def _prep_chunks():
    cols = []
    for a_, b_ in _SEGMENTS:
        cols += [-1] * b_ if a_ is None else list(range(a_, b_))
    src, valid = [], []
    for i in range(0, N_PROJ, LANES):
        seg = cols[i:i + LANES]
        n = sum(1 for c in seg if c >= 0)
        assert seg[:n] == list(range(seg[0], seg[0] + n)) and all(c < 0 for c in seg[n:])
        src.append(seg[0])
        valid.append(n)
    return np.asarray(src, np.int32), np.asarray(valid, np.int32)


def _prep_w_kernel(src_ref, valid_ref, wt_ref, o_ref):
    i = pl.program_id(0)
    src = src_ref[i]
    row = lax.broadcasted_iota(jnp.int32, (LANES, wt_ref.shape[2]), 0)
    for l in range(wt_ref.shape[1]):
        x = wt_ref[pl.ds(src, LANES), l, :]
        x = jnp.where(row < valid_ref[i], x, 0.0)
        o_ref[l] = x.T.astype(BF16)


def _prep_w_in(w_in):
    depth, d, n_in = w_in.shape
    src, valid = _prep_chunks()
    w_t = jnp.transpose(w_in, (2, 0, 1))
    grid_spec = pltpu.PrefetchScalarGridSpec(
        num_scalar_prefetch=2,
        grid=(N_PROJ // LANES,),
        in_specs=[pl.BlockSpec((n_in, depth, d), lambda i, s, v: (0, 0, 0), pipeline_mode=pl.Buffered(1))],
        out_specs=pl.BlockSpec((depth, d, LANES), lambda i, s, v: (0, 0, i)),
    )
    return pl.pallas_call(
        _prep_w_kernel,
        grid_spec=grid_spec,
        out_shape=jax.ShapeDtypeStruct((depth, d, N_PROJ), BF16),
        compiler_params=_cparams(("arbitrary",)),
        name="prep_w_in",
    )(jnp.asarray(src), jnp.asarray(valid), w_t)


def _layer(x2, p3, layer, b, s_len, bt, pt, imat, w_p_all, wo_all, pp_all, pg_all,
           g_pre, g_post, nsa_pos, nsa_w1, nsa_w2,
           sgu_ln_g, sgu_ln_b, sgu_w, sgu_b, conv_w, conv_b, conv_ln_g, conv_ln_b, conv_pw, conv_pw_b,
           rwkv_mu, rwkv_w0, rwkv_w_up, rwkv_a0, rwkv_a_up, rwkv_k_k, rwkv_k_a, rwkv_r_k,
           rwkv_gn_g, rwkv_gn_b):
    n = x2.shape[0]
    w = W_GRP
    t_len = sgu_w.shape[-1]
    wall = jnp.transpose(sgu_w, (1, 0, 2)).reshape(t_len, N_HEADS * t_len)
    sb = jnp.repeat(sgu_b.T, HEAD_DIM, axis=1)
    nsa, rw, yb, yc = _in_proj_gmlp_conv(x2, g_pre, w_p_all, layer, s_len, wall, sb, sgu_ln_g, sgu_ln_b, conv_w, conv_b,
                                         conv_ln_g, conv_ln_b, conv_pw, conv_pw_b, t_len)

    nsa3 = nsa.reshape(b, s_len, NSA_W)
    kc, vct = _nsa_compress(nsa3, nsa_pos, nsa_w1, nsa_w2)
    ya = _nsa_attention(nsa3, kc, vct, bt, pt, imat).reshape(n, w)

    lora = RWKV_LORA
    mu_p = jnp.concatenate([rwkv_mu, jnp.zeros((RW_W - rwkv_mu.shape[0],), F32)]).reshape(1, RW_W)
    wup_p = jnp.concatenate([rwkv_w_up, jnp.zeros((LANES - lora, w), F32)], axis=0)
    aup_p = jnp.concatenate([jnp.zeros((lora, w), F32), rwkv_a_up, jnp.zeros((LANES - 2 * lora, w), F32)], axis=0)
    vec = jnp.stack([rwkv_w0, rwkv_a0, rwkv_k_k, rwkv_k_a, rwkv_r_k.reshape(w), rwkv_gn_g, rwkv_gn_b,
                     jnp.zeros((w,), F32)], axis=0)
    yd = _rwkv(rw.reshape(b, s_len, RW_W), mu_p, wup_p, aup_p, vec).reshape(n, w)

    return _out_proj((ya, yb, yc, yd), x2, p3, layer, wo_all, g_post, pp_all, pg_all)


def kernel(x, p, rel_bias, w_in, w_out, g_pre, g_post, nsa_pos, nsa_w1, nsa_w2, sgu_ln_g, sgu_ln_b, sgu_w, sgu_b, conv_w, conv_b, conv_ln_g, conv_ln_b, conv_pw, conv_pw_b, rwkv_mu, rwkv_w0, rwkv_w_up, rwkv_a0, rwkv_a_up, rwkv_k_k, rwkv_k_a, rwkv_r_k, rwkv_gn_g, rwkv_gn_b, ple_proj, ple_gate):
    b, s_len, d = x.shape
    depth = w_in.shape[0]
    bt, pt, imat = _nsa_tables(rel_bias, s_len)
    x2 = x.reshape(b * s_len, d)
    p3 = p.reshape(depth, b * s_len, p.shape[-1])
    w_p_all = _prep_w_in(w_in)
    wo_all, pp_all, pg_all = w_out.astype(BF16), ple_proj.astype(BF16), ple_gate.astype(BF16)
    per_layer = (g_pre, g_post, nsa_pos, nsa_w1, nsa_w2, sgu_ln_g, sgu_ln_b, sgu_w, sgu_b,
                 conv_w, conv_b, conv_ln_g, conv_ln_b, conv_pw, conv_pw_b, rwkv_mu, rwkv_w0, rwkv_w_up,
                 rwkv_a0, rwkv_a_up, rwkv_k_k, rwkv_k_a, rwkv_r_k, rwkv_gn_g, rwkv_gn_b)
    for i in range(depth):
        x2 = _layer(x2, p3, i, b, s_len, bt, pt, imat, w_p_all, wo_all, pp_all, pg_all,
                    *[a[i] for a in per_layer])
    return x2.reshape(b, s_len, d)
```
